```python
import jax, jax.numpy as jnp
from jax import lax
import numpy as np

D_MODEL = 2048
BATCH = 4
SEQ = 2048
DEPTH = 4
DEC_BATCH = 128
DEC_SEQ = 4
PAST_LEN = 16384
PAGE_SIZE = 128

D_MIX = D_MODEL
N_MIXERS = 4
D_GROUP = D_MIX // N_MIXERS
N_SUB = 4
D_SUB = D_GROUP // N_SUB
D_IN = 8 * D_GROUP
CHUNK = 128
CONV_B_WIDTH = 31
POOL_WINDOWS = (2, 4, 8, 16)
POOL_PREV = max(POOL_WINDOWS) - 1
SCONV_WIDTH = 3
FFN_CONV_WIDTH = 3
D_FF = ((8 * D_MODEL // 3 + 127) // 128) * 128
N_MEM = 256
N_XHEADS = 4
D_XHEAD = 128
D_X = N_XHEADS * D_XHEAD
EPS = 1e-6

kernel_name = "hybrid_parallel_group_decoder_step"


def rms_norm(x, g):
    xf = x.astype(jnp.float32)
    y = xf * lax.rsqrt(jnp.mean(xf * xf, axis=-1, keepdims=True) + EPS)
    return (y * g.astype(jnp.float32)).astype(x.dtype)


def layer_norm(x, g):
    xf = x.astype(jnp.float32)
    mu = jnp.mean(xf, axis=-1, keepdims=True)
    var = jnp.mean(jnp.square(xf - mu), axis=-1, keepdims=True)
    return ((xf - mu) * lax.rsqrt(var + EPS) * g.astype(jnp.float32)).astype(x.dtype)


def causal_dwconv(x_ext, w):
    c = x_ext.shape[-1]
    return lax.conv_general_dilated(
        x_ext, w[:, None, :].astype(x_ext.dtype), window_strides=(1,), padding='VALID',
        dimension_numbers=('NWC', 'WIO', 'NWC'), feature_group_count=c)


def chunk_gating_mixer(u, v, g_v, w_s, b_s):
    bn, t, _ = u.shape
    lc = min(t, CHUNK)
    vn = layer_norm(v, g_v)
    mask = jnp.tril(jnp.ones((lc, lc), dtype=bool))
    w = jnp.where(mask[None], w_s[:, :lc, :lc], 0).astype(vn.dtype)
    vc = vn.reshape(bn, t // lc, lc, N_SUB, D_SUB)
    bias = jnp.transpose(b_s[:, :lc])[None, None, :, :, None].astype(vn.dtype)
    z = jnp.einsum('hij,bcjhd->bcihd', w, vc) + bias
    y = u.reshape(bn, t // lc, lc, N_SUB, D_SUB) * z
    return y.reshape(bn, t, D_GROUP), vn


def conformer_conv_mixer(a, gate, prev, w_conv, b_conv, gn_g, gn_b):
    bn, t, _ = a.shape
    h = a * jax.nn.sigmoid(gate)
    ext = jnp.concatenate([prev.astype(h.dtype), h], axis=1)
    y = causal_dwconv(ext, w_conv) + b_conv
    yf = y.astype(jnp.float32).reshape(bn, t, N_SUB, D_SUB)
    mu = jnp.mean(yf, axis=-1, keepdims=True)
    var = jnp.mean(jnp.square(yf - mu), axis=-1, keepdims=True)
    yf = ((yf - mu) * lax.rsqrt(var + EPS)).reshape(bn, t, D_GROUP)
    yf = yf * gn_g.astype(jnp.float32) + gn_b.astype(jnp.float32)
    return jax.nn.silu(yf).astype(a.dtype), ext[:, -(CONV_B_WIDTH - 1):]


def multiscale_pool_mixer(x, prev, start_pos, w_lin, scale):
    bn, t, _ = x.shape
    p = POOL_PREV
    ext = jnp.concatenate([prev.astype(x.dtype), x], axis=1)
    ext_f = ext.astype(jnp.float32).reshape(bn, p + t, N_SUB, D_SUB)
    cs = jnp.concatenate([jnp.zeros((bn, 1, N_SUB, D_SUB), jnp.float32),
                          jnp.cumsum(ext_f, axis=1)], axis=1)
    pos = start_pos + jnp.arange(t)
    means = []
    for g, win in enumerate(POOL_WINDOWS):
        hi = cs[:, p + 1:p + 1 + t, g]
        lo = cs[:, p + 1 - win:p + 1 - win + t, g]
        cnt = jnp.minimum(pos + 1, win).astype(jnp.float32)
        means.append((hi - lo) / cnt[None, :, None])
    pooled = jnp.stack(means, axis=2) - ext_f[:, p:]
    y = jnp.einsum('btgc,gcd->btgd', pooled.astype(x.dtype), w_lin).reshape(bn, t, D_GROUP)
    return y * scale, ext[:, -p:]


def short_gated_conv_mixer(xt, bg, cg, prev, w_conv):
    h = cg * xt
    ext = jnp.concatenate([prev.astype(h.dtype), h], axis=1)
    return bg * causal_dwconv(ext, w_conv), ext[:, -(SCONV_WIDTH - 1):]


def memory_kv(mem, g_mem, w_k, w_v):
    m = rms_norm(mem, g_mem)
    return m @ w_k, m @ w_v


def memory_cross_attention(x, mem_k, mem_v, w_q, w_o):
    bn, t, _ = x.shape
    nm = mem_k.shape[1]
    q = (x @ w_q).reshape(bn, t, N_XHEADS, D_XHEAD)
    kh = mem_k.astype(q.dtype).reshape(bn, nm, N_XHEADS, D_XHEAD)
    vh = mem_v.astype(q.dtype).reshape(bn, nm, N_XHEADS, D_XHEAD)
    s = jnp.einsum('bthd,bmhd->bhtm', q, kh).astype(jnp.float32) * (D_XHEAD ** -0.5)
    pr = jax.nn.softmax(s, axis=-1).astype(vh.dtype)
    o = jnp.einsum('bhtm,bmhd->bthd', pr, vh).reshape(bn, t, D_X)
    return o @ w_o


def conv_ffn(x, prev, w_up, w_conv, w_down):
    h = x @ w_up
    ext = jnp.concatenate([prev.astype(h.dtype), h], axis=1)
    hc = causal_dwconv(ext, w_conv)
    g, u = jnp.split(hc, 2, axis=-1)
    return (jax.nn.silu(g) * u) @ w_down, ext[:, -(FFN_CONV_WIDTH - 1):]


def trunk_layer(h, mem_k, mem_v, prev_b, prev_pool, prev_sc, prev_ffn, start_pos, p):
    xn = rms_norm(h, p['g_mix_pre'])
    z = xn @ p['w_in']
    a_u, a_v, b_a, b_g, c_x, d_x, d_b, d_c = jnp.split(z, 8, axis=-1)
    y_a, v_rows = chunk_gating_mixer(a_u, a_v, p['a_norm_g'], p['a_ws'], p['a_bs'])
    y_b, new_b = conformer_conv_mixer(b_a, b_g, prev_b, p['b_conv_w'], p['b_conv_b'], p['b_gn_g'], p['b_gn_b'])
    y_c, new_pool = multiscale_pool_mixer(c_x, prev_pool, start_pos, p['c_lin'], p['c_scale'])
    y_d, new_sc = short_gated_conv_mixer(d_x, d_b, d_c, prev_sc, p['d_conv_w'])
    mix = jnp.concatenate([y_a, y_b, y_c, y_d], axis=-1) @ p['w_out']
    h = h + rms_norm(mix, p['g_mix_post'])
    xa = memory_cross_attention(rms_norm(h, p['g_x_pre']), mem_k, mem_v, p['w_xq'], p['w_xo'])
    h = h + rms_norm(xa, p['g_x_post'])
    f, new_ffn = conv_ffn(rms_norm(h, p['g_ffn_pre']), prev_ffn, p['w_up'], p['f_conv_w'], p['w_down'])
    h = h + rms_norm(f, p['g_ffn_post'])
    return h, new_b, new_pool, new_sc, new_ffn, v_rows


def setup_inputs(seed: int = 0) -> dict:
    key = jax.random.key(seed)
    ks = iter(jax.random.split(key, 48))

    def nrm(shape, scale=1.0):
        return jax.random.normal(next(ks), shape, jnp.float32) * scale

    def gain(shape):
        return 1.0 + nrm(shape, 0.02)

    L = DEPTH
    return {
        'x_prompt': nrm((BATCH, SEQ, D_MODEL)),
        'x_sample': nrm((DEC_BATCH, DEC_SEQ, D_MODEL)),
        'cache_mem_k': nrm((L, DEC_BATCH, N_MEM, D_X)),
        'cache_mem_v': nrm((L, DEC_BATCH, N_MEM, D_X)),
        'state_conv_b': nrm((L, DEC_BATCH, CONV_B_WIDTH - 1, D_GROUP), 0.5),
        'state_pool': nrm((L, DEC_BATCH, POOL_PREV, D_GROUP)),
        'state_sconv': nrm((L, DEC_BATCH, SCONV_WIDTH - 1, D_GROUP), 0.5),
        'state_ffn_conv': nrm((L, DEC_BATCH, FFN_CONV_WIDTH - 1, 2 * D_FF)),
        'mem_prompt': nrm((BATCH, N_MEM, D_MODEL)),
        'g_mix_pre': gain((L, D_MODEL)),
        'g_mix_post': gain((L, D_MODEL)),
        'g_mem': gain((L, D_MODEL)),
        'g_x_pre': gain((L, D_MODEL)),
        'g_x_post': gain((L, D_MODEL)),
        'g_ffn_pre': gain((L, D_MODEL)),
        'g_ffn_post': gain((L, D_MODEL)),
        'w_in': nrm((L, D_MODEL, D_IN), D_MODEL ** -0.5),
        'w_out': nrm((L, D_MIX, D_MODEL), D_MIX ** -0.5),
        'a_norm_g': gain((L, D_GROUP)),
        'a_ws': nrm((L, N_SUB, CHUNK, CHUNK), CHUNK ** -0.5),
        'a_bs': 1.0 + nrm((L, N_SUB, CHUNK), 0.01),
        'b_conv_w': nrm((L, CONV_B_WIDTH, D_GROUP), CONV_B_WIDTH ** -0.5),
        'b_conv_b': nrm((L, D_GROUP), 0.02),
        'b_gn_g': gain((L, D_GROUP)),
        'b_gn_b': nrm((L, D_GROUP), 0.02),
        'c_lin': nrm((L, N_SUB, D_SUB, D_SUB), D_SUB ** -0.5),
        'c_scale': gain((L, D_GROUP)),
        'd_conv_w': nrm((L, SCONV_WIDTH, D_GROUP), SCONV_WIDTH ** -0.5),
        'w_xq': nrm((L, D_MODEL, D_X), D_MODEL ** -0.5),
        'w_xk': nrm((L, D_MODEL, D_X), D_MODEL ** -0.5),
        'w_xv': nrm((L, D_MODEL, D_X), D_MODEL ** -0.5),
        'w_xo': nrm((L, D_X, D_MODEL), D_X ** -0.5),
        'w_up': nrm((L, D_MODEL, 2 * D_FF), D_MODEL ** -0.5),
        'f_conv_w': nrm((L, FFN_CONV_WIDTH, 2 * D_FF), FFN_CONV_WIDTH ** -0.5),
        'w_down': nrm((L, D_FF, D_MODEL), D_FF ** -0.5),
    }


def reference(x_prompt, x_sample, cache_mem_k, cache_mem_v, state_conv_b, state_pool, state_sconv,
              state_ffn_conv, mem_prompt, g_mix_pre, g_mix_post, g_mem, g_x_pre, g_x_post, g_ffn_pre,
              g_ffn_post, w_in, w_out, a_norm_g, a_ws, a_bs, b_conv_w, b_conv_b, b_gn_g, b_gn_b, c_lin,
              c_scale, d_conv_w, w_xq, w_xk, w_xv, w_xo, w_up, f_conv_w, w_down):
    bp = x_prompt.shape[0]
    dt = x_prompt.dtype
    zb = jnp.zeros((bp, CONV_B_WIDTH - 1, D_GROUP), dt)
    zpool = jnp.zeros((bp, POOL_PREV, D_GROUP), dt)
    zsc = jnp.zeros((bp, SCONV_WIDTH - 1, D_GROUP), dt)
    zffn = jnp.zeros((bp, FFN_CONV_WIDTH - 1, 2 * D_FF), dt)

    hp = x_prompt
    hs = x_sample
    p_mk, p_mv, p_b, p_pool, p_sc, p_ffn = [], [], [], [], [], []
    s_b, s_pool, s_sc, s_ffn, s_v = [], [], [], [], []
    for l in range(DEPTH):
        prm = {
            'g_mix_pre': g_mix_pre[l], 'g_mix_post': g_mix_post[l],
            'g_x_pre': g_x_pre[l], 'g_x_post': g_x_post[l],
            'g_ffn_pre': g_ffn_pre[l], 'g_ffn_post': g_ffn_post[l],
            'w_in': w_in[l], 'w_out': w_out[l],
            'a_norm_g': a_norm_g[l], 'a_ws': a_ws[l], 'a_bs': a_bs[l],
            'b_conv_w': b_conv_w[l], 'b_conv_b': b_conv_b[l], 'b_gn_g': b_gn_g[l], 'b_gn_b': b_gn_b[l],
            'c_lin': c_lin[l], 'c_scale': c_scale[l], 'd_conv_w': d_conv_w[l],
            'w_xq': w_xq[l], 'w_xo': w_xo[l],
            'w_up': w_up[l], 'f_conv_w': f_conv_w[l], 'w_down': w_down[l],
        }
        mk, mv = memory_kv(mem_prompt, g_mem[l], w_xk[l], w_xv[l])
        hp, nb, npool, nsc, nffn, _ = trunk_layer(hp, mk, mv, zb, zpool, zsc, zffn, 0, prm)
        p_mk.append(mk); p_mv.append(mv); p_b.append(nb); p_pool.append(npool)
        p_sc.append(nsc); p_ffn.append(nffn)
        hs, sb, spool, ssc, sffn, sv = trunk_layer(
            hs, cache_mem_k[l], cache_mem_v[l], state_conv_b[l], state_pool[l], state_sconv[l],
            state_ffn_conv[l], PAST_LEN, prm)
        s_b.append(sb); s_pool.append(spool); s_sc.append(ssc); s_ffn.append(sffn); s_v.append(sv)

    return (hp, hs,
            jnp.stack(p_mk), jnp.stack(p_mv), jnp.stack(p_b), jnp.stack(p_pool), jnp.stack(p_sc), jnp.stack(p_ffn),
            jnp.stack(s_b), jnp.stack(s_pool), jnp.stack(s_sc), jnp.stack(s_ffn), jnp.stack(s_v))
```

```python
import functools

import jax
import jax.numpy as jnp
from jax import lax
from jax.experimental import pallas as pl
from jax.experimental.pallas import tpu as pltpu

F32 = jnp.float32
BF16 = jnp.bfloat16

D_MODEL = 2048
BATCH = 4
SEQ = 2048
DEPTH = 4
DEC_BATCH = 128
DEC_SEQ = 4
PAST_LEN = 16384
D_GROUP = 512
N_SUB = 4
D_SUB = 128
D_IN = 8 * D_GROUP
CHUNK = 128
CONV_B_WIDTH = 31
POOL_WINDOWS = (2, 4, 8, 16)
POOL_PREV = 15
SCONV_WIDTH = 3
FFN_CONV_WIDTH = 3
D_FF = 5504
N_MEM = 256
N_XHEADS = 4
D_XHEAD = 128
D_X = 512
EPS = 1e-6

V7X_SUBLANES = 8
V7X_LANES = 128
V7X_VMEM_LIMIT_BYTES = 56 * 1024 * 1024

HIST_B = 32
HIST_C = 16
HIST_D = 8
HIST_F = 8

TM = 512
TT = 256
ROWS = 32
TN_IN = 1024
TN_UP = 512
TN_DOWN = 512
BB_MIX = 32
BB_ATT = 8


def _params(*sem):
    return pltpu.CompilerParams(dimension_semantics=sem, vmem_limit_bytes=V7X_VMEM_LIMIT_BYTES)


def _rms(x, g):
    return x * lax.rsqrt(jnp.mean(x * x, axis=-1, keepdims=True) + EPS) * g


def _sigmoid(x):
    return 1.0 / (1.0 + jnp.exp(-x))


def _dot(a, b):
    return jnp.dot(a, b, preferred_element_type=F32)


def _norm_matmul_kernel(x_ref, g_ref, w_ref, o_ref, xn_ref):
    @pl.when(pl.program_id(1) == 0)
    def _():
        xn_ref[...] = _rms(x_ref[...], g_ref[...]).astype(BF16)

    o_ref[...] = _dot(xn_ref[...], w_ref[...])


def norm_matmul(x, g, w, tm, tn):
    m, k = x.shape
    n = w.shape[1]
    return pl.pallas_call(
        _norm_matmul_kernel,
        grid=(m // tm, n // tn),
        in_specs=[pl.BlockSpec((tm, k), lambda i, j: (i, 0)),
                  pl.BlockSpec((1, k), lambda i, j: (0, 0)),
                  pl.BlockSpec((k, tn), lambda i, j: (0, j))],
        out_specs=pl.BlockSpec((tm, tn), lambda i, j: (i, j)),
        out_shape=jax.ShapeDtypeStruct((m, n), F32),
        scratch_shapes=[pltpu.VMEM((tm, k), BF16)],
        compiler_params=_params("parallel", "arbitrary"),
        name="norm_matmul",
    )(x, g, w)


def _layer_norm(v, g):
    mu = jnp.mean(v, axis=-1, keepdims=True)
    d = v - mu
    var = jnp.mean(d * d, axis=-1, keepdims=True)
    return d * lax.rsqrt(var + EPS) * g


def _group_norm_silu(y, g, b):
    mu = jnp.mean(y, axis=-1, keepdims=True)
    d = y - mu
    var = jnp.mean(d * d, axis=-1, keepdims=True)
    yn = d * lax.rsqrt(var + EPS) * g + b
    return yn * _sigmoid(yn)


def _mixer_prompt_kernel(z_ref, ag_ref, aws_ref, abt_ref, bw_ref, bb_ref, gng_ref, gnb_ref,
                         clin_ref, cs_ref, dw_ref,
                         y_ref, nb_ref, np_ref, ns_ref,
                         extb, extc, extd, pooled):
    t = pl.program_id(1)
    tt = y_ref.shape[0]

    @pl.when(t == 0)
    def _():
        extb[0:HIST_B, :] = jnp.zeros((HIST_B, D_GROUP), F32)
        extc[0:HIST_C, :] = jnp.zeros((HIST_C, D_GROUP), F32)
        extd[0:HIST_D, :] = jnp.zeros((HIST_D, D_GROUP), F32)

    vn = _layer_norm(z_ref[:, D_GROUP:2 * D_GROUP], ag_ref[...]).astype(BF16)
    row = lax.broadcasted_iota(jnp.int32, (CHUNK, CHUNK), 0)
    col = lax.broadcasted_iota(jnp.int32, (CHUNK, CHUNK), 1)
    for h in range(N_SUB):
        wm = jnp.where(row >= col, aws_ref[h], 0.0).astype(BF16)
        bias = abt_ref[:, h:h + 1]
        for c in range(tt // CHUNK):
            rs = slice(c * CHUNK, (c + 1) * CHUNK)
            ls = slice(h * D_SUB, (h + 1) * D_SUB)
            zz = _dot(wm, vn[rs, ls]) + bias
            y_ref[rs, ls] = (z_ref[rs, ls] * zz).astype(BF16)

    extb[HIST_B:HIST_B + tt, :] = z_ref[:, 2 * D_GROUP:3 * D_GROUP] * _sigmoid(z_ref[:, 3 * D_GROUP:4 * D_GROUP])
    for c in range(tt // ROWS):
        for g in range(N_SUB):
            ls = slice(g * D_SUB, (g + 1) * D_SUB)
            acc = jnp.zeros((ROWS, D_SUB), F32)
            for k in range(CONV_B_WIDTH):
                r0 = c * ROWS + HIST_B - (CONV_B_WIDTH - 1) + k
                acc = acc + bw_ref[k:k + 1, ls] * extb[r0:r0 + ROWS, ls]
            yb = _group_norm_silu(acc + bb_ref[:, ls], gng_ref[:, ls], gnb_ref[:, ls])
            y_ref[c * ROWS:(c + 1) * ROWS, D_GROUP + g * D_SUB:D_GROUP + (g + 1) * D_SUB] = yb.astype(BF16)

    extc[HIST_C:HIST_C + tt, :] = z_ref[:, 4 * D_GROUP:5 * D_GROUP]
    for c in range(tt // ROWS):
        pos = t * tt + c * ROWS + lax.broadcasted_iota(jnp.int32, (ROWS, 1), 0)
        for g, win in enumerate(POOL_WINDOWS):
            ls = slice(g * D_SUB, (g + 1) * D_SUB)
            r0 = c * ROWS + HIST_C
            x = extc[r0:r0 + ROWS, ls]
            s = x
            for i in range(1, win):
                s = s + extc[r0 - i:r0 - i + ROWS, ls]
            cnt = jnp.minimum(pos + 1, win).astype(F32)
            pooled[c * ROWS:(c + 1) * ROWS, ls] = (s / cnt - x).astype(BF16)
    for g in range(N_SUB):
        ls = slice(g * D_SUB, (g + 1) * D_SUB)
        yc = _dot(pooled[:, ls], clin_ref[g].astype(BF16)) * cs_ref[:, ls]
        y_ref[:, 2 * D_GROUP + g * D_SUB:2 * D_GROUP + (g + 1) * D_SUB] = yc.astype(BF16)

    extd[HIST_D:HIST_D + tt, :] = z_ref[:, 7 * D_GROUP:8 * D_GROUP] * z_ref[:, 5 * D_GROUP:6 * D_GROUP]
    for c in range(tt // ROWS):
        r0 = c * ROWS + HIST_D
        conv = (dw_ref[0:1, :] * extd[r0 - 2:r0 - 2 + ROWS, :]
                + dw_ref[1:2, :] * extd[r0 - 1:r0 - 1 + ROWS, :]
                + dw_ref[2:3, :] * extd[r0:r0 + ROWS, :])
        rs = slice(c * ROWS, (c + 1) * ROWS)
        y_ref[rs, 3 * D_GROUP:4 * D_GROUP] = (z_ref[rs, 6 * D_GROUP:7 * D_GROUP] * conv).astype(BF16)

    extb[0:HIST_B, :] = extb[tt:tt + HIST_B, :]
    extc[0:HIST_C, :] = extc[tt:tt + HIST_C, :]
    extd[0:HIST_D, :] = extd[tt:tt + HIST_D, :]

    @pl.when(t == pl.num_programs(1) - 1)
    def _():
        nb_ref[...] = extb[0:HIST_B, :]
        np_ref[...] = extc[0:HIST_C, :]
        ns_ref[...] = extd[0:HIST_D, :]


def mixer_prompt(z, ag, aws, abt, bw, bb, gng, gnb, clin, cs, dw):
    tt = TT
    full = lambda a: pl.BlockSpec(a.shape, lambda b, t: (0,) * a.ndim)
    hist = lambda r: pl.BlockSpec((None, r, D_GROUP), lambda b, t: (b, 0, 0))
    return pl.pallas_call(
        _mixer_prompt_kernel,
        grid=(BATCH, SEQ // tt),
        in_specs=[pl.BlockSpec((None, tt, D_IN), lambda b, t: (b, t, 0)),
                  full(ag), full(aws), full(abt), full(bw), full(bb), full(gng), full(gnb),
                  full(clin), full(cs), full(dw)],
        out_specs=[pl.BlockSpec((None, tt, D_MODEL), lambda b, t: (b, t, 0)),
                   hist(HIST_B), hist(HIST_C), hist(HIST_D)],
        out_shape=[jax.ShapeDtypeStruct((BATCH, SEQ, D_MODEL), BF16),
                   jax.ShapeDtypeStruct((BATCH, HIST_B, D_GROUP), F32),
                   jax.ShapeDtypeStruct((BATCH, HIST_C, D_GROUP), F32),
                   jax.ShapeDtypeStruct((BATCH, HIST_D, D_GROUP), F32)],
        scratch_shapes=[pltpu.VMEM((HIST_B + tt, D_GROUP), F32),
                        pltpu.VMEM((HIST_C + tt, D_GROUP), F32),
                        pltpu.VMEM((HIST_D + tt, D_GROUP), F32),
                        pltpu.VMEM((tt, D_GROUP), BF16)],
        compiler_params=_params("parallel", "arbitrary"),
        name="mixer_prompt",
    )(z, ag, aws, abt, bw, bb, gng, gnb, clin, cs, dw)


def _mixer_sample_kernel(z_ref, cb_ref, cp_ref, csc_ref, ag_ref, aw4_ref, ab4_ref, bw_ref, bb_ref,
                         gng_ref, gnb_ref, clin_ref, cs_ref, dw_ref,
                         y_ref, nb_ref, np_ref, nsc_ref, v_ref, pooled):
    bblk = z_ref.shape[1]
    blk = lambda r: slice(r * D_GROUP, (r + 1) * D_GROUP)

    vn = [_layer_norm(z_ref[t, :, D_GROUP:2 * D_GROUP], ag_ref[...]) for t in range(DEC_SEQ)]
    for i in range(DEC_SEQ):
        v_ref[:, blk(i)] = vn[i]
        zz = ab4_ref[i:i + 1, :]
        for j in range(i + 1):
            zz = zz + aw4_ref[i * DEC_SEQ + j:i * DEC_SEQ + j + 1, :] * vn[j]
        y_ref[i, :, 0:D_GROUP] = (z_ref[i, :, 0:D_GROUP] * zz).astype(BF16)

    nprev = CONV_B_WIDTH - 1
    hb = [z_ref[t, :, 2 * D_GROUP:3 * D_GROUP] * _sigmoid(z_ref[t, :, 3 * D_GROUP:4 * D_GROUP])
          for t in range(DEC_SEQ)]
    ext_b = lambda r: cb_ref[:, blk(r)] if r < nprev else hb[r - nprev]
    for t in range(DEC_SEQ):
        acc = jnp.zeros((bblk, D_GROUP), F32)
        for k in range(CONV_B_WIDTH):
            acc = acc + bw_ref[k:k + 1, :] * ext_b(t + k)
        acc = acc + bb_ref[...]
        for g in range(N_SUB):
            ls = slice(g * D_SUB, (g + 1) * D_SUB)
            yb = _group_norm_silu(acc[:, ls], gng_ref[:, ls], gnb_ref[:, ls])
            y_ref[t, :, D_GROUP + g * D_SUB:D_GROUP + (g + 1) * D_SUB] = yb.astype(BF16)
    for r in range(nprev):
        nb_ref[:, blk(r)] = ext_b(r + DEC_SEQ)

    cx = [z_ref[t, :, 4 * D_GROUP:5 * D_GROUP] for t in range(DEC_SEQ)]
    ext_c = lambda r: cp_ref[:, blk(r)] if r < POOL_PREV else cx[r - POOL_PREV]
    for t in range(DEC_SEQ):
        for g, win in enumerate(POOL_WINDOWS):
            ls = slice(g * D_SUB, (g + 1) * D_SUB)
            s = cx[t][:, ls]
            for i in range(1, win):
                s = s + ext_c(POOL_PREV + t - i)[:, ls]
            cnt = float(min(PAST_LEN + t + 1, win))
            pooled[t * bblk:(t + 1) * bblk, ls] = (s / cnt - cx[t][:, ls]).astype(BF16)
    for g in range(N_SUB):
        ls = slice(g * D_SUB, (g + 1) * D_SUB)
        yc = _dot(pooled[:, ls], clin_ref[g].astype(BF16)) * cs_ref[:, ls]
        for t in range(DEC_SEQ):
            y_ref[t, :, 2 * D_GROUP + g * D_SUB:2 * D_GROUP + (g + 1) * D_SUB] = (
                yc[t * bblk:(t + 1) * bblk].astype(BF16))
    for r in range(POOL_PREV):
        np_ref[:, blk(r)] = ext_c(r + DEC_SEQ)

    nsp = SCONV_WIDTH - 1
    hd = [z_ref[t, :, 7 * D_GROUP:8 * D_GROUP] * z_ref[t, :, 5 * D_GROUP:6 * D_GROUP] for t in range(DEC_SEQ)]
    ext_d = lambda r: csc_ref[:, blk(r)] if r < nsp else hd[r - nsp]
    for t in range(DEC_SEQ):
        conv = dw_ref[0:1, :] * ext_d(t) + dw_ref[1:2, :] * ext_d(t + 1) + dw_ref[2:3, :] * ext_d(t + 2)
        y_ref[t, :, 3 * D_GROUP:4 * D_GROUP] = (z_ref[t, :, 6 * D_GROUP:7 * D_GROUP] * conv).astype(BF16)
    for r in range(nsp):
        nsc_ref[:, blk(r)] = ext_d(r + DEC_SEQ)


def mixer_sample(z, cb, cp, csc, ag, aw4, ab4, bw, bb, gng, gnb, clin, cs, dw):
    bblk = BB_MIX
    full = lambda a: pl.BlockSpec(a.shape, lambda b: (0,) * a.ndim)
    st = lambda a: pl.BlockSpec((bblk, a.shape[1]), lambda b: (b, 0))
    nv = DEC_SEQ * D_GROUP
    return pl.pallas_call(
        _mixer_sample_kernel,
        grid=(DEC_BATCH // bblk,),
        in_specs=[pl.BlockSpec((DEC_SEQ, bblk, D_IN), lambda b: (0, b, 0)),
                  st(cb), st(cp), st(csc),
                  full(ag), full(aw4), full(ab4), full(bw), full(bb), full(gng), full(gnb),
                  full(clin), full(cs), full(dw)],
        out_specs=[pl.BlockSpec((DEC_SEQ, bblk, D_MODEL), lambda b: (0, b, 0)),
                   st(cb), st(cp), st(csc),
                   pl.BlockSpec((bblk, nv), lambda b: (b, 0))],
        out_shape=[jax.ShapeDtypeStruct((DEC_SEQ, DEC_BATCH, D_MODEL), BF16),
                   jax.ShapeDtypeStruct(cb.shape, F32),
                   jax.ShapeDtypeStruct(cp.shape, F32),
                   jax.ShapeDtypeStruct(csc.shape, F32),
                   jax.ShapeDtypeStruct((DEC_BATCH, nv), F32)],
        scratch_shapes=[pltpu.VMEM((DEC_SEQ * bblk, D_GROUP), BF16)],
        compiler_params=_params("parallel"),
        name="mixer_sample",
    )(z, cb, cp, csc, ag, aw4, ab4, bw, bb, gng, gnb, clin, cs, dw)


def _stage_a(y, h, wout_ref, gpost_ref, gpre_ref, wxq_ref):
    h1 = h + _rms(_dot(y, wout_ref[...]), gpost_ref[...])
    q = _dot(_rms(h1, gpre_ref[...]).astype(BF16), wxq_ref[...])
    return h1, q


def _stage_c(o, h1, wxo_ref, gpost_ref, gffn_ref):
    h2 = h1 + _rms(_dot(o, wxo_ref[...]), gpost_ref[...])
    return h2, _rms(h2, gffn_ref[...]).astype(BF16)


def _softmax_rows(s):
    e = jnp.exp(s - jnp.max(s, axis=-1, keepdims=True))
    return e / jnp.sum(e, axis=-1, keepdims=True)


_NT = (((1,), (1,)), ((), ()))


def _mid_prompt_kernel(y_ref, h_ref, wout_ref, gmp_ref, gxp_ref, wxq_ref, mk_ref, mv_ref, wxo_ref,
                       gxo_ref, gffn_ref, h2_ref, xn_ref):
    h1, q = _stage_a(y_ref[...], h_ref[...], wout_ref, gmp_ref, gxp_ref, wxq_ref)
    q = q.astype(BF16)
    k = mk_ref[...].astype(BF16)
    v = mv_ref[...].astype(BF16)
    heads = []
    for hd in range(N_XHEADS):
        ls = slice(hd * D_XHEAD, (hd + 1) * D_XHEAD)
        s = lax.dot_general(q[:, ls], k[:, ls], _NT, preferred_element_type=F32) * (D_XHEAD ** -0.5)
        heads.append(_dot(_softmax_rows(s).astype(BF16), v[:, ls]))
    o = jnp.concatenate(heads, axis=-1).astype(BF16)
    h2, xn = _stage_c(o, h1, wxo_ref, gxo_ref, gffn_ref)
    h2_ref[...] = h2
    xn_ref[...] = xn


def mid_prompt(y, h, wout, gmp, gxp, wxq, mk, mv, wxo, gxo, gffn):
    m = y.shape[0]
    tm = TM
    per_seq = SEQ // tm
    rows = lambda c: pl.BlockSpec((tm, c), lambda i: (i, 0))
    full = lambda a: pl.BlockSpec(a.shape, lambda i: (0,) * a.ndim, pipeline_mode=pl.Buffered(1))
    mem = pl.BlockSpec((None, N_MEM, D_X), lambda i: (i // per_seq, 0, 0))
    return pl.pallas_call(
        _mid_prompt_kernel,
        grid=(m // tm,),
        in_specs=[rows(D_MODEL), rows(D_MODEL), full(wout), full(gmp), full(gxp), full(wxq), mem, mem,
                  full(wxo), full(gxo), full(gffn)],
        out_specs=[rows(D_MODEL), rows(D_MODEL)],
        out_shape=[jax.ShapeDtypeStruct((m, D_MODEL), F32), jax.ShapeDtypeStruct((m, D_MODEL), BF16)],
        compiler_params=_params("parallel"),
        name="mid_prompt",
    )(y, h, wout, gmp, gxp, wxq, mk, mv, wxo, gxo, gffn)


def _mid_a_kernel(y_ref, h_ref, wout_ref, gmp_ref, gxp_ref, wxq_ref, h1_ref, q_ref):
    h1, q = _stage_a(y_ref[...], h_ref[...], wout_ref, gmp_ref, gxp_ref, wxq_ref)
    h1_ref[...] = h1
    q_ref[...] = q.astype(BF16)


def mid_a(y, h, wout, gmp, gxp, wxq):
    m = y.shape[0]
    tm = TM
    rows = lambda c: pl.BlockSpec((tm, c), lambda i: (i, 0))
    full = lambda a: pl.BlockSpec(a.shape, lambda i: (0,) * a.ndim, pipeline_mode=pl.Buffered(1))
    return pl.pallas_call(
        _mid_a_kernel,
        grid=(m // tm,),
        in_specs=[rows(D_MODEL), rows(D_MODEL), full(wout), full(gmp), full(gxp), full(wxq)],
        out_specs=[rows(D_MODEL), rows(D_X)],
        out_shape=[jax.ShapeDtypeStruct((m, D_MODEL), F32), jax.ShapeDtypeStruct((m, D_X), BF16)],
        compiler_params=_params("parallel"),
        name="mid_a",
    )(y, h, wout, gmp, gxp, wxq)


def _mid_c_kernel(o_ref, h1_ref, wxo_ref, gxo_ref, gffn_ref, h2_ref, xn_ref):
    h2, xn = _stage_c(o_ref[...], h1_ref[...], wxo_ref, gxo_ref, gffn_ref)
    h2_ref[...] = h2
    xn_ref[...] = xn


def mid_c(o, h1, wxo, gxo, gffn):
    m = o.shape[0]
    tm = TM
    rows = lambda c: pl.BlockSpec((tm, c), lambda i: (i, 0))
    full = lambda a: pl.BlockSpec(a.shape, lambda i: (0,) * a.ndim, pipeline_mode=pl.Buffered(1))
    return pl.pallas_call(
        _mid_c_kernel,
        grid=(m // tm,),
        in_specs=[rows(D_X), rows(D_MODEL), full(wxo), full(gxo), full(gffn)],
        out_specs=[rows(D_MODEL), rows(D_MODEL)],
        out_shape=[jax.ShapeDtypeStruct((m, D_MODEL), F32), jax.ShapeDtypeStruct((m, D_MODEL), BF16)],
        compiler_params=_params("parallel"),
        name="mid_c",
    )(o, h1, wxo, gxo, gffn)


def _attn_sample_kernel(q_ref, k_ref, v_ref, o_ref):
    bblk = q_ref.shape[0]
    lane_head = lax.broadcasted_iota(jnp.int32, (V7X_SUBLANES, D_X), 1) // D_XHEAD
    for b in range(bblk):
        q8 = q_ref[b]
        qbd = jnp.concatenate([jnp.where(lane_head == hd, q8, jnp.zeros_like(q8)) for hd in range(N_XHEADS)], axis=0)
        kb = k_ref[b].astype(BF16)
        vb = v_ref[b].astype(BF16)
        s = lax.dot_general(qbd, kb, _NT, preferred_element_type=F32) * (D_XHEAD ** -0.5)
        of = _dot(_softmax_rows(s).astype(BF16), vb)
        o8 = jnp.zeros((V7X_SUBLANES, D_X), F32)
        for hd in range(N_XHEADS):
            o8 = o8 + jnp.where(lane_head == hd, of[hd * V7X_SUBLANES:(hd + 1) * V7X_SUBLANES], 0.0)
        o_ref[b] = o8.astype(BF16)


def attn_sample(q, k, v):
    bblk = BB_ATT
    qs = pl.BlockSpec((bblk, V7X_SUBLANES, D_X), lambda b: (b, 0, 0))
    ms = pl.BlockSpec((bblk, N_MEM, D_X), lambda b: (b, 0, 0))
    return pl.pallas_call(
        _attn_sample_kernel,
        grid=(DEC_BATCH // bblk,),
        in_specs=[qs, ms, ms],
        out_specs=qs,
        out_shape=jax.ShapeDtypeStruct(q.shape, BF16),
        compiler_params=_params("parallel"),
        name="attn_sample",
    )(q, k, v)


def _gate(cg, cu):
    return (cg * _sigmoid(cg) * cu).astype(BF16)


def _up_prompt_kernel(xn_ref, wg_ref, wu_ref, fwg_ref, fwu_ref, o_ref, tg_ref, tu_ref,
                      hsg, hsu, carg, caru, *, per_seq):
    i = pl.program_id(0)
    j = pl.program_id(1)
    tm = xn_ref.shape[0]
    xn = xn_ref[...]
    hsg[HIST_F:HIST_F + tm, :] = _dot(xn, wg_ref[...])
    hsu[HIST_F:HIST_F + tm, :] = _dot(xn, wu_ref[...])

    @pl.when(i % per_seq == 0)
    def _():
        hsg[0:HIST_F, :] = jnp.zeros((HIST_F, hsg.shape[1]), F32)
        hsu[0:HIST_F, :] = jnp.zeros((HIST_F, hsu.shape[1]), F32)

    @pl.when(i % per_seq != 0)
    def _():
        hsg[0:HIST_F, :] = carg[j]
        hsu[0:HIST_F, :] = caru[j]

    tail_g = hsg[tm:tm + HIST_F, :]
    tail_u = hsu[tm:tm + HIST_F, :]
    carg[j] = tail_g
    caru[j] = tail_u
    tg_ref[...] = tail_g
    tu_ref[...] = tail_u

    def conv(hs, fw_ref):
        return (fw_ref[0:1, :] * hs[HIST_F - 2:HIST_F - 2 + tm, :]
                + fw_ref[1:2, :] * hs[HIST_F - 1:HIST_F - 1 + tm, :]
                + fw_ref[2:3, :] * hs[HIST_F:HIST_F + tm, :])

    o_ref[...] = _gate(conv(hsg, fwg_ref), conv(hsu, fwu_ref))


def up_prompt(xn, wg, wu, fwg, fwu):
    m, k = xn.shape
    tm, tn = TM, TN_UP
    nj = pl.cdiv(D_FF, tn)
    col = lambda r: pl.BlockSpec((r, tn), lambda i, j: (0, j))
    tail = pl.BlockSpec((HIST_F, tn), lambda i, j: (i, j))
    return pl.pallas_call(
        functools.partial(_up_prompt_kernel, per_seq=SEQ // tm),
        grid=(m // tm, nj),
        in_specs=[pl.BlockSpec((tm, k), lambda i, j: (i, 0)), col(k), col(k), col(HIST_F), col(HIST_F)],
        out_specs=[pl.BlockSpec((tm, tn), lambda i, j: (i, j)), tail, tail],
        out_shape=[jax.ShapeDtypeStruct((m, D_FF), BF16),
                   jax.ShapeDtypeStruct((m // tm * HIST_F, D_FF), F32),
                   jax.ShapeDtypeStruct((m // tm * HIST_F, D_FF), F32)],
        scratch_shapes=[pltpu.VMEM((HIST_F + tm, tn), F32), pltpu.VMEM((HIST_F + tm, tn), F32),
                        pltpu.VMEM((nj, HIST_F, tn), F32), pltpu.VMEM((nj, HIST_F, tn), F32)],
        compiler_params=_params("arbitrary", "arbitrary"),
        name="up_prompt",
    )(xn, wg, wu, fwg, fwu)


def _up_sample_kernel(xn_ref, wg_ref, wu_ref, fwg_ref, fwu_ref, p0g_ref, p1g_ref, p0u_ref, p1u_ref,
                      o_ref, tg_ref, tu_ref):
    nb = DEC_BATCH
    xn = xn_ref[...]
    hg = _dot(xn, wg_ref[...])
    hu = _dot(xn, wu_ref[...])
    tg_ref[...] = hg[(DEC_SEQ - 2) * nb:, :]
    tu_ref[...] = hu[(DEC_SEQ - 2) * nb:, :]
    ext_g = [p0g_ref[...], p1g_ref[...]] + [hg[t * nb:(t + 1) * nb] for t in range(DEC_SEQ)]
    ext_u = [p0u_ref[...], p1u_ref[...]] + [hu[t * nb:(t + 1) * nb] for t in range(DEC_SEQ)]
    for t in range(DEC_SEQ):
        cg = fwg_ref[0:1, :] * ext_g[t] + fwg_ref[1:2, :] * ext_g[t + 1] + fwg_ref[2:3, :] * ext_g[t + 2]
        cu = fwu_ref[0:1, :] * ext_u[t] + fwu_ref[1:2, :] * ext_u[t + 1] + fwu_ref[2:3, :] * ext_u[t + 2]
        o_ref[t * nb:(t + 1) * nb, :] = _gate(cg, cu)


def up_sample(xn, wg, wu, fwg, fwu, p0g, p1g, p0u, p1u):
    m, k = xn.shape
    tn = TN_UP
    nj = pl.cdiv(D_FF, tn)
    col = lambda r: pl.BlockSpec((r, tn), lambda j: (0, j))
    nt = (FFN_CONV_WIDTH - 1) * DEC_BATCH
    return pl.pallas_call(
        _up_sample_kernel,
        grid=(nj,),
        in_specs=[pl.BlockSpec((m, k), lambda j: (0, 0)), col(k), col(k), col(HIST_F), col(HIST_F),
                  col(DEC_BATCH), col(DEC_BATCH), col(DEC_BATCH), col(DEC_BATCH)],
        out_specs=[col(m), col(nt), col(nt)],
        out_shape=[jax.ShapeDtypeStruct((m, D_FF), BF16),
                   jax.ShapeDtypeStruct((nt, D_FF), F32),
                   jax.ShapeDtypeStruct((nt, D_FF), F32)],
        compiler_params=_params("parallel"),
        name="up_sample",
    )(xn, wg, wu, fwg, fwu, p0g, p1g, p0u, p1u)


def _down_kernel(g_ref, w_ref, h_ref, gn_ref, o_ref, acc):
    j = pl.program_id(1)
    tn = w_ref.shape[1]
    acc[:, pl.ds(pl.multiple_of(j * tn, tn), tn)] = _dot(g_ref[...], w_ref[...])

    @pl.when(j == pl.num_programs(1) - 1)
    def _():
        o_ref[...] = h_ref[...] + _rms(acc[...], gn_ref[...])


def down(g, w, h, gn):
    m, k = g.shape
    tm, tn = TM, TN_DOWN
    return pl.pallas_call(
        _down_kernel,
        grid=(m // tm, D_MODEL // tn),
        in_specs=[pl.BlockSpec((tm, k), lambda i, j: (i, 0)),
                  pl.BlockSpec((k, tn), lambda i, j: (0, j)),
                  pl.BlockSpec((tm, D_MODEL), lambda i, j: (i, 0)),
                  pl.BlockSpec((1, D_MODEL), lambda i, j: (0, 0))],
        out_specs=pl.BlockSpec((tm, D_MODEL), lambda i, j: (i, 0)),
        out_shape=jax.ShapeDtypeStruct((m, D_MODEL), F32),
        scratch_shapes=[pltpu.VMEM((tm, D_MODEL), F32)],
        compiler_params=_params("parallel", "arbitrary"),
        name="down",
    )(g, w, h, gn)


def _pad_rows(a, rows):
    return jnp.pad(a, ((0, rows - a.shape[0]), (0, 0)))


def kernel(x_prompt, x_sample, cache_mem_k, cache_mem_v, state_conv_b, state_pool, state_sconv, state_ffn_conv, mem_prompt, g_mix_pre, g_mix_post, g_mem, g_x_pre, g_x_post, g_ffn_pre, g_ffn_post, w_in, w_out, a_norm_g, a_ws, a_bs, b_conv_w, b_conv_b, b_gn_g, b_gn_b, c_lin, c_scale, d_conv_w, w_xq, w_xk, w_xv, w_xo, w_up, f_conv_w, w_down):
    nb, ns = DEC_BATCH, DEC_SEQ
    w_in_b = w_in.astype(BF16)
    w_out_b = w_out.astype(BF16)
    w_xq_b = w_xq.astype(BF16)
    w_kv_b = jnp.concatenate([w_xk, w_xv], axis=-1).astype(BF16)
    w_xo_b = w_xo.astype(BF16)
    w_g_b = w_up[:, :, :D_FF].astype(BF16)
    w_u_b = w_up[:, :, D_FF:].astype(BF16)
    w_down_b = w_down.astype(BF16)

    hp = x_prompt.reshape(BATCH * SEQ, D_MODEL)
    hs = jnp.transpose(x_sample, (1, 0, 2)).reshape(ns * nb, D_MODEL)
    mem = mem_prompt.reshape(BATCH * N_MEM, D_MODEL)
    row = lambda a: a.reshape(1, -1)

    outs = [[] for _ in range(11)]
    for l in range(DEPTH):
        ag, bb, gng, gnb, cs = (row(a[l]) for a in (a_norm_g, b_conv_b, b_gn_g, b_gn_b, c_scale))
        abt = jnp.transpose(a_bs[l])
        bw = _pad_rows(b_conv_w[l], HIST_B)
        dw = _pad_rows(d_conv_w[l], HIST_D)
        fwg = _pad_rows(f_conv_w[l, :, :D_FF], HIST_F)
        fwu = _pad_rows(f_conv_w[l, :, D_FF:], HIST_F)
        aw4 = jnp.repeat(jnp.transpose(a_ws[l, :, :ns, :ns], (1, 2, 0)).reshape(ns * ns, N_SUB), D_SUB, axis=1)
        ab4 = jnp.repeat(jnp.transpose(a_bs[l, :, :ns]), D_SUB, axis=1)
        gmp, gxp, gxo, gfp, gfo = (row(a[l]) for a in (g_mix_post, g_x_pre, g_x_post, g_ffn_pre, g_ffn_post))
        gpre = row(g_mix_pre[l])

        kv = norm_matmul(mem, row(g_mem[l]), w_kv_b[l], TM, D_X)
        mk = kv[:, :D_X].reshape(BATCH, N_MEM, D_X)
        mv = kv[:, D_X:].reshape(BATCH, N_MEM, D_X)

        z = norm_matmul(hp, gpre, w_in_b[l], TM, TN_IN)
        y, nbp, npp, nsp = mixer_prompt(z.reshape(BATCH, SEQ, D_IN), ag, a_ws[l], abt, bw, bb, gng, gnb,
                                        c_lin[l], cs, dw)
        h2, xn = mid_prompt(y.reshape(BATCH * SEQ, D_MODEL), hp, w_out_b[l], gmp, gxp, w_xq_b[l], mk, mv,
                            w_xo_b[l], gxo, gfp)
        gate, tg, tu = up_prompt(xn, w_g_b[l], w_u_b[l], fwg, fwu)
        hp = down(gate, w_down_b[l], h2, gfo)
        per_seq = SEQ // TM
        tails = jnp.concatenate([tg, tu], axis=-1).reshape(BATCH, per_seq, HIST_F, 2 * D_FF)
        p_out = (mk, mv, nbp[:, HIST_B - (CONV_B_WIDTH - 1):], npp[:, HIST_C - POOL_PREV:],
                 nsp[:, HIST_D - (SCONV_WIDTH - 1):], tails[:, -1, HIST_F - (FFN_CONV_WIDTH - 1):])

        zs = norm_matmul(hs, gpre, w_in_b[l], TM, TN_IN)
        ys, nbs, nps, nss, vs = mixer_sample(
            zs.reshape(ns, nb, D_IN), state_conv_b[l].reshape(nb, -1), state_pool[l].reshape(nb, -1),
            state_sconv[l].reshape(nb, -1), ag, aw4, ab4, bw, bb, gng, gnb, c_lin[l], cs, dw)
        h1s, qs = mid_a(ys.reshape(ns * nb, D_MODEL), hs, w_out_b[l], gmp, gxp, w_xq_b[l])
        q8 = jnp.pad(jnp.transpose(qs.reshape(ns, nb, D_X), (1, 0, 2)), ((0, 0), (0, V7X_SUBLANES - ns), (0, 0)))
        o8 = attn_sample(q8, cache_mem_k[l], cache_mem_v[l])
        os_ = jnp.transpose(o8[:, :ns], (1, 0, 2)).reshape(ns * nb, D_X)
        h2s, xns = mid_c(os_, h1s, w_xo_b[l], gxo, gfp)
        sf = state_ffn_conv[l]
        gates, tgs, tus = up_sample(xns, w_g_b[l], w_u_b[l], fwg, fwu,
                                    sf[:, 0, :D_FF], sf[:, 1, :D_FF], sf[:, 0, D_FF:], sf[:, 1, D_FF:])
        hs = down(gates, w_down_b[l], h2s, gfo)
        nffn_s = jnp.transpose(jnp.concatenate([tgs, tus], axis=-1).reshape(FFN_CONV_WIDTH - 1, nb, 2 * D_FF),
                               (1, 0, 2))
        s_out = (nbs.reshape(nb, CONV_B_WIDTH - 1, D_GROUP), nps.reshape(nb, POOL_PREV, D_GROUP),
                 nss.reshape(nb, SCONV_WIDTH - 1, D_GROUP), nffn_s, vs.reshape(nb, ns, D_GROUP))
        for acc, val in zip(outs, p_out + s_out):
            acc.append(val)

    return (hp.reshape(BATCH, SEQ, D_MODEL),
            jnp.transpose(hs.reshape(ns, nb, D_MODEL), (1, 0, 2)),
            *(jnp.stack(o) for o in outs))
```

```python
import functools

import jax
import jax.numpy as jnp
from jax import lax
from jax.experimental import pallas as pl
from jax.experimental.pallas import tpu as pltpu

F32 = jnp.float32
BF16 = jnp.bfloat16

D_MODEL = 2048
BATCH = 4
SEQ = 2048
DEPTH = 4
DEC_BATCH = 128
DEC_SEQ = 4
PAST_LEN = 16384
D_GROUP = 512
N_SUB = 4
D_SUB = 128
D_IN = 8 * D_GROUP
CHUNK = 128
CONV_B_WIDTH = 31
POOL_WINDOWS = (2, 4, 8, 16)
POOL_PREV = 15
SCONV_WIDTH = 3
FFN_CONV_WIDTH = 3
D_FF = 5504
N_MEM = 256
N_XHEADS = 4
D_XHEAD = 128
D_X = 512
EPS = 1e-6

V7X_SUBLANES = 8
V7X_LANES = 128
V7X_VMEM_LIMIT_BYTES = 56 * 1024 * 1024

HIST_B = 32
HIST_C = 16
HIST_D = 8
HIST_F = 8

TM = 512
TM_UP = 1024
SUB_UP = 256
TM_IN = 1024
SUB_MID = 256
TT = 256
ROWS = 32
TN_IN = 1024
TN_UP = 512
TN_DOWN = 512
BB_MIX = 32
BB_ATT = 8


def _params(*sem):
    return pltpu.CompilerParams(dimension_semantics=sem, vmem_limit_bytes=V7X_VMEM_LIMIT_BYTES)


def _rms(x, g):
    return x * lax.rsqrt(jnp.mean(x * x, axis=-1, keepdims=True) + EPS) * g


def _sigmoid(x):
    return 1.0 / (1.0 + jnp.exp(-x))


def _dot(a, b):
    return jnp.dot(a, b, preferred_element_type=F32)


def _cast_kernel(x_ref, o_ref):
    o_ref[...] = x_ref[...].astype(BF16)


def cast_weights(w, rows, cols=None, col_block=0):
    depth, r, c = w.shape
    cols = c if cols is None else cols
    return pl.pallas_call(
        _cast_kernel,
        grid=(depth, r // rows),
        in_specs=[pl.BlockSpec((None, rows, cols), lambda l, i: (l, i, col_block))],
        out_specs=pl.BlockSpec((None, rows, cols), lambda l, i: (l, i, 0)),
        out_shape=jax.ShapeDtypeStruct((depth, r, cols), BF16),
        compiler_params=_params("parallel", "parallel"),
        name="cast_weights",
    )(w)


def _cast_pair_kernel(a_ref, b_ref, o_ref):
    n = a_ref.shape[-1]
    o_ref[:, 0:n] = a_ref[...].astype(BF16)
    o_ref[:, n:2 * n] = b_ref[...].astype(BF16)


def cast_weight_pair(a, b):
    depth, r, c = a.shape
    spec = pl.BlockSpec((None, r, c), lambda l: (l, 0, 0))
    return pl.pallas_call(
        _cast_pair_kernel,
        grid=(depth,),
        in_specs=[spec, spec],
        out_specs=pl.BlockSpec((None, r, 2 * c), lambda l: (l, 0, 0)),
        out_shape=jax.ShapeDtypeStruct((depth, r, 2 * c), BF16),
        compiler_params=_params("parallel"),
        name="cast_weight_pair",
    )(a, b)


def _norm_matmul_kernel(x_ref, g_ref, w_ref, o_ref, xn_ref):
    @pl.when(pl.program_id(1) == 0)
    def _():
        xn_ref[...] = _rms(x_ref[...], g_ref[...]).astype(BF16)

    o_ref[...] = _dot(xn_ref[...], w_ref[...])


def norm_matmul(x, g, w, l, tm, tn):
    m, k = x.shape
    n = w.shape[2]
    return pl.pallas_call(
        _norm_matmul_kernel,
        grid=(m // tm, n // tn),
        in_specs=[pl.BlockSpec((tm, k), lambda i, j: (i, 0)),
                  pl.BlockSpec((1, k), lambda i, j: (0, 0)),
                  pl.BlockSpec((None, k, tn), lambda i, j: (l, 0, j))],
        out_specs=pl.BlockSpec((tm, tn), lambda i, j: (i, j)),
        out_shape=jax.ShapeDtypeStruct((m, n), F32),
        scratch_shapes=[pltpu.VMEM((tm, k), BF16)],
        compiler_params=_params("parallel", "arbitrary"),
        name="norm_matmul",
    )(x, g, w)


def _layer_norm(v, g):
    mu = jnp.mean(v, axis=-1, keepdims=True)
    d = v - mu
    var = jnp.mean(d * d, axis=-1, keepdims=True)
    return d * lax.rsqrt(var + EPS) * g


def _group_norm_silu(y, g, b):
    mu = jnp.mean(y, axis=-1, keepdims=True)
    d = y - mu
    var = jnp.mean(d * d, axis=-1, keepdims=True)
    yn = d * lax.rsqrt(var + EPS) * g + b
    return yn * _sigmoid(yn)


def _mixer_prompt_kernel(z_ref, ag_ref, aws_ref, abt_ref, bw_ref, bb_ref, gng_ref, gnb_ref,
                         clin_ref, cs_ref, dw_ref,
                         y_ref, nb_ref, np_ref, ns_ref,
                         extb, extc, extd, pooled):
    t = pl.program_id(1)
    tt = y_ref.shape[0]

    @pl.when(t == 0)
    def _():
        for s in range(V7X_SUBLANES):
            extb[s, 0:HIST_B, :] = jnp.zeros((HIST_B, D_GROUP), F32)
            extb[s, tt + HIST_B - V7X_SUBLANES:tt + HIST_B, :] = jnp.zeros((V7X_SUBLANES, D_GROUP), F32)
        extc[0:HIST_C, :] = jnp.zeros((HIST_C, D_GROUP), F32)
        extd[0:HIST_D, :] = jnp.zeros((HIST_D, D_GROUP), F32)

    vn = _layer_norm(z_ref[:, D_GROUP:2 * D_GROUP], ag_ref[...]).astype(BF16)
    row = lax.broadcasted_iota(jnp.int32, (CHUNK, CHUNK), 0)
    col = lax.broadcasted_iota(jnp.int32, (CHUNK, CHUNK), 1)
    for h in range(N_SUB):
        wm = jnp.where(row >= col, aws_ref[h], 0.0).astype(BF16)
        bias = abt_ref[:, h:h + 1]
        for c in range(tt // CHUNK):
            rs = slice(c * CHUNK, (c + 1) * CHUNK)
            ls = slice(h * D_SUB, (h + 1) * D_SUB)
            zz = _dot(wm, vn[rs, ls]) + bias
            y_ref[rs, ls] = (z_ref[rs, ls] * zz).astype(BF16)

    hb = z_ref[:, 2 * D_GROUP:3 * D_GROUP] * _sigmoid(z_ref[:, 3 * D_GROUP:4 * D_GROUP])
    for s in range(V7X_SUBLANES):
        extb[s, HIST_B - s:HIST_B - s + tt, :] = hb
    for c in range(tt // ROWS):
        for g in range(N_SUB):
            ls = slice(g * D_SUB, (g + 1) * D_SUB)
            acc = jnp.zeros((ROWS, D_SUB), F32)
            for k in range(CONV_B_WIDTH):
                tiles, s = divmod(HIST_B - (CONV_B_WIDTH - 1) + k, V7X_SUBLANES)
                r0 = c * ROWS + tiles * V7X_SUBLANES
                acc = acc + bw_ref[k:k + 1, ls] * extb[s, r0:r0 + ROWS, ls]
            yb = _group_norm_silu(acc + bb_ref[:, ls], gng_ref[:, ls], gnb_ref[:, ls])
            y_ref[c * ROWS:(c + 1) * ROWS, D_GROUP + g * D_SUB:D_GROUP + (g + 1) * D_SUB] = yb.astype(BF16)

    extc[HIST_C:HIST_C + tt, :] = z_ref[:, 4 * D_GROUP:5 * D_GROUP]
    for c in range(tt // ROWS):
        pos = t * tt + c * ROWS + lax.broadcasted_iota(jnp.int32, (ROWS, 1), 0)
        for g, win in enumerate(POOL_WINDOWS):
            ls = slice(g * D_SUB, (g + 1) * D_SUB)
            r0 = c * ROWS + HIST_C
            x = extc[r0:r0 + ROWS, ls]
            s = x
            for i in range(1, win):
                s = s + extc[r0 - i:r0 - i + ROWS, ls]
            cnt = jnp.minimum(pos + 1, win).astype(F32)
            pooled[c * ROWS:(c + 1) * ROWS, ls] = (s / cnt - x).astype(BF16)
    for g in range(N_SUB):
        ls = slice(g * D_SUB, (g + 1) * D_SUB)
        yc = _dot(pooled[:, ls], clin_ref[g].astype(BF16)) * cs_ref[:, ls]
        y_ref[:, 2 * D_GROUP + g * D_SUB:2 * D_GROUP + (g + 1) * D_SUB] = yc.astype(BF16)

    extd[HIST_D:HIST_D + tt, :] = z_ref[:, 7 * D_GROUP:8 * D_GROUP] * z_ref[:, 5 * D_GROUP:6 * D_GROUP]
    for c in range(tt // ROWS):
        r0 = c * ROWS + HIST_D
        conv = (dw_ref[0:1, :] * extd[r0 - 2:r0 - 2 + ROWS, :]
                + dw_ref[1:2, :] * extd[r0 - 1:r0 - 1 + ROWS, :]
                + dw_ref[2:3, :] * extd[r0:r0 + ROWS, :])
        rs = slice(c * ROWS, (c + 1) * ROWS)
        y_ref[rs, 3 * D_GROUP:4 * D_GROUP] = (z_ref[rs, 6 * D_GROUP:7 * D_GROUP] * conv).astype(BF16)

    for s in range(V7X_SUBLANES):
        extb[s, 0:HIST_B, :] = extb[s, tt:tt + HIST_B, :]
    extc[0:HIST_C, :] = extc[tt:tt + HIST_C, :]
    extd[0:HIST_D, :] = extd[tt:tt + HIST_D, :]

    @pl.when(t == pl.num_programs(1) - 1)
    def _():
        nb_ref[...] = extb[0, 0:HIST_B, :]
        np_ref[...] = extc[0:HIST_C, :]
        ns_ref[...] = extd[0:HIST_D, :]


def mixer_prompt(z, ag, aws, abt, bw, bb, gng, gnb, clin, cs, dw):
    tt = TT
    full = lambda a: pl.BlockSpec(a.shape, lambda b, t: (0,) * a.ndim)
    hist = lambda r: pl.BlockSpec((None, r, D_GROUP), lambda b, t: (b, 0, 0))
    return pl.pallas_call(
        _mixer_prompt_kernel,
        grid=(BATCH, SEQ // tt),
        in_specs=[pl.BlockSpec((None, tt, D_IN), lambda b, t: (b, t, 0)),
                  full(ag), full(aws), full(abt), full(bw), full(bb), full(gng), full(gnb),
                  full(clin), full(cs), full(dw)],
        out_specs=[pl.BlockSpec((None, tt, D_MODEL), lambda b, t: (b, t, 0)),
                   hist(HIST_B), hist(HIST_C), hist(HIST_D)],
        out_shape=[jax.ShapeDtypeStruct((BATCH, SEQ, D_MODEL), BF16),
                   jax.ShapeDtypeStruct((BATCH, HIST_B, D_GROUP), F32),
                   jax.ShapeDtypeStruct((BATCH, HIST_C, D_GROUP), F32),
                   jax.ShapeDtypeStruct((BATCH, HIST_D, D_GROUP), F32)],
        scratch_shapes=[pltpu.VMEM((V7X_SUBLANES, HIST_B + tt, D_GROUP), F32),
                        pltpu.VMEM((HIST_C + tt, D_GROUP), F32),
                        pltpu.VMEM((HIST_D + tt, D_GROUP), F32),
                        pltpu.VMEM((tt, D_GROUP), BF16)],
        compiler_params=_params("parallel", "arbitrary"),
        name="mixer_prompt",
    )(z, ag, aws, abt, bw, bb, gng, gnb, clin, cs, dw)


def _mixer_sample_kernel(z_ref, cb_ref, cp_ref, csc_ref, ag_ref, aw4_ref, ab4_ref, bw_ref, bb_ref,
                         gng_ref, gnb_ref, clin_ref, cs_ref, dw_ref,
                         y_ref, nb_ref, np_ref, nsc_ref, v_ref, pooled):
    bblk = z_ref.shape[1]
    blk = lambda r: slice(r * D_GROUP, (r + 1) * D_GROUP)

    vn = [_layer_norm(z_ref[t, :, D_GROUP:2 * D_GROUP], ag_ref[...]) for t in range(DEC_SEQ)]
    for i in range(DEC_SEQ):
        v_ref[:, blk(i)] = vn[i]
        zz = ab4_ref[i:i + 1, :]
        for j in range(i + 1):
            zz = zz + aw4_ref[i * DEC_SEQ + j:i * DEC_SEQ + j + 1, :] * vn[j]
        y_ref[i, :, 0:D_GROUP] = (z_ref[i, :, 0:D_GROUP] * zz).astype(BF16)

    nprev = CONV_B_WIDTH - 1
    hb = [z_ref[t, :, 2 * D_GROUP:3 * D_GROUP] * _sigmoid(z_ref[t, :, 3 * D_GROUP:4 * D_GROUP])
          for t in range(DEC_SEQ)]
    ext_b = lambda r: cb_ref[:, blk(r)] if r < nprev else hb[r - nprev]
    for t in range(DEC_SEQ):
        acc = jnp.zeros((bblk, D_GROUP), F32)
        for k in range(CONV_B_WIDTH):
            acc = acc + bw_ref[k:k + 1, :] * ext_b(t + k)
        acc = acc + bb_ref[...]
        for g in range(N_SUB):
            ls = slice(g * D_SUB, (g + 1) * D_SUB)
            yb = _group_norm_silu(acc[:, ls], gng_ref[:, ls], gnb_ref[:, ls])
            y_ref[t, :, D_GROUP + g * D_SUB:D_GROUP + (g + 1) * D_SUB] = yb.astype(BF16)
    for r in range(nprev):
        nb_ref[:, blk(r)] = ext_b(r + DEC_SEQ)

    cx = [z_ref[t, :, 4 * D_GROUP:5 * D_GROUP] for t in range(DEC_SEQ)]
    ext_c = lambda r: cp_ref[:, blk(r)] if r < POOL_PREV else cx[r - POOL_PREV]
    for t in range(DEC_SEQ):
        for g, win in enumerate(POOL_WINDOWS):
            ls = slice(g * D_SUB, (g + 1) * D_SUB)
            s = cx[t][:, ls]
            for i in range(1, win):
                s = s + ext_c(POOL_PREV + t - i)[:, ls]
            cnt = float(min(PAST_LEN + t + 1, win))
            pooled[t * bblk:(t + 1) * bblk, ls] = (s / cnt - cx[t][:, ls]).astype(BF16)
    for g in range(N_SUB):
        ls = slice(g * D_SUB, (g + 1) * D_SUB)
        yc = _dot(pooled[:, ls], clin_ref[g].astype(BF16)) * cs_ref[:, ls]
        for t in range(DEC_SEQ):
            y_ref[t, :, 2 * D_GROUP + g * D_SUB:2 * D_GROUP + (g + 1) * D_SUB] = (
                yc[t * bblk:(t + 1) * bblk].astype(BF16))
    for r in range(POOL_PREV):
        np_ref[:, blk(r)] = ext_c(r + DEC_SEQ)

    nsp = SCONV_WIDTH - 1
    hd = [z_ref[t, :, 7 * D_GROUP:8 * D_GROUP] * z_ref[t, :, 5 * D_GROUP:6 * D_GROUP] for t in range(DEC_SEQ)]
    ext_d = lambda r: csc_ref[:, blk(r)] if r < nsp else hd[r - nsp]
    for t in range(DEC_SEQ):
        conv = dw_ref[0:1, :] * ext_d(t) + dw_ref[1:2, :] * ext_d(t + 1) + dw_ref[2:3, :] * ext_d(t + 2)
        y_ref[t, :, 3 * D_GROUP:4 * D_GROUP] = (z_ref[t, :, 6 * D_GROUP:7 * D_GROUP] * conv).astype(BF16)
    for r in range(nsp):
        nsc_ref[:, blk(r)] = ext_d(r + DEC_SEQ)


def mixer_sample(z, cb, cp, csc, ag, aw4, ab4, bw, bb, gng, gnb, clin, cs, dw, l):
    bblk = BB_MIX
    full = lambda a: pl.BlockSpec(a.shape, lambda b: (0,) * a.ndim)
    st_in = lambda a: pl.BlockSpec((None, bblk, a.shape[2]), lambda b: (l, b, 0))
    st = lambda a: pl.BlockSpec((bblk, a.shape[2]), lambda b: (b, 0))
    nv = DEC_SEQ * D_GROUP
    return pl.pallas_call(
        _mixer_sample_kernel,
        grid=(DEC_BATCH // bblk,),
        in_specs=[pl.BlockSpec((DEC_SEQ, bblk, D_IN), lambda b: (0, b, 0)),
                  st_in(cb), st_in(cp), st_in(csc),
                  full(ag), full(aw4), full(ab4), full(bw), full(bb), full(gng), full(gnb),
                  full(clin), full(cs), full(dw)],
        out_specs=[pl.BlockSpec((DEC_SEQ, bblk, D_MODEL), lambda b: (0, b, 0)),
                   st(cb), st(cp), st(csc),
                   pl.BlockSpec((bblk, nv), lambda b: (b, 0))],
        out_shape=[jax.ShapeDtypeStruct((DEC_SEQ, DEC_BATCH, D_MODEL), BF16),
                   jax.ShapeDtypeStruct(cb.shape[1:], F32),
                   jax.ShapeDtypeStruct(cp.shape[1:], F32),
                   jax.ShapeDtypeStruct(csc.shape[1:], F32),
                   jax.ShapeDtypeStruct((DEC_BATCH, nv), F32)],
        scratch_shapes=[pltpu.VMEM((DEC_SEQ * bblk, D_GROUP), BF16)],
        compiler_params=_params("parallel"),
        name="mixer_sample",
    )(z, cb, cp, csc, ag, aw4, ab4, bw, bb, gng, gnb, clin, cs, dw)


def _stage_a(y, h, wout_ref, gpost_ref, gpre_ref, wxq_ref):
    h1 = h + _rms(_dot(y, wout_ref[...]), gpost_ref[...])
    q = _dot(_rms(h1, gpre_ref[...]).astype(BF16), wxq_ref[...])
    return h1, q


def _stage_c(o, h1, wxo_ref, gpost_ref, gffn_ref):
    h2 = h1 + _rms(_dot(o, wxo_ref[...]), gpost_ref[...])
    return h2, _rms(h2, gffn_ref[...]).astype(BF16)


def _softmax_rows(s):
    e = jnp.exp(s - jnp.max(s, axis=-1, keepdims=True))
    return e / jnp.sum(e, axis=-1, keepdims=True)


_NT = (((1,), (1,)), ((), ()))


def _mid_prompt_kernel(y_ref, h_ref, wout_ref, gmp_ref, gxp_ref, wxq_ref, mk_ref, mv_ref, wxo_ref,
                       gxo_ref, gffn_ref, h2_ref, xn_ref):
    k = mk_ref[...].astype(BF16)
    v = mv_ref[...].astype(BF16)
    for r in range(y_ref.shape[0] // SUB_MID):
        rs = slice(r * SUB_MID, (r + 1) * SUB_MID)
        h1, q = _stage_a(y_ref[rs, :], h_ref[rs, :], wout_ref, gmp_ref, gxp_ref, wxq_ref)
        q = q.astype(BF16)
        heads = []
        for hd in range(N_XHEADS):
            ls = slice(hd * D_XHEAD, (hd + 1) * D_XHEAD)
            s = lax.dot_general(q[:, ls], k[:, ls], _NT, preferred_element_type=F32) * (D_XHEAD ** -0.5)
            heads.append(_dot(_softmax_rows(s).astype(BF16), v[:, ls]))
        o = jnp.concatenate(heads, axis=-1).astype(BF16)
        h2, xn = _stage_c(o, h1, wxo_ref, gxo_ref, gffn_ref)
        h2_ref[rs, :] = h2
        xn_ref[rs, :] = xn


def _layer_weight(w, l):
    return pl.BlockSpec((None,) + w.shape[1:], lambda i: (l, 0, 0), pipeline_mode=pl.Buffered(1))


def mid_prompt(y, h, wout, gmp, gxp, wxq, mk, mv, wxo, gxo, gffn, l):
    m = y.shape[0]
    tm = TM
    per_seq = SEQ // tm
    rows = lambda c: pl.BlockSpec((tm, c), lambda i: (i, 0))
    full = lambda a: pl.BlockSpec(a.shape, lambda i: (0,) * a.ndim, pipeline_mode=pl.Buffered(1))
    mem = pl.BlockSpec((None, N_MEM, D_X), lambda i: (i // per_seq, 0, 0))
    return pl.pallas_call(
        _mid_prompt_kernel,
        grid=(m // tm,),
        in_specs=[rows(D_MODEL), rows(D_MODEL), _layer_weight(wout, l), full(gmp), full(gxp),
                  _layer_weight(wxq, l), mem, mem, _layer_weight(wxo, l), full(gxo), full(gffn)],
        out_specs=[rows(D_MODEL), rows(D_MODEL)],
        out_shape=[jax.ShapeDtypeStruct((m, D_MODEL), F32), jax.ShapeDtypeStruct((m, D_MODEL), BF16)],
        compiler_params=_params("parallel"),
        name="mid_prompt",
    )(y, h, wout, gmp, gxp, wxq, mk, mv, wxo, gxo, gffn)


def _mid_a_kernel(y_ref, h_ref, wout_ref, gmp_ref, gxp_ref, wxq_ref, h1_ref, q_ref):
    h1, q = _stage_a(y_ref[...], h_ref[...], wout_ref, gmp_ref, gxp_ref, wxq_ref)
    h1_ref[...] = h1
    q_ref[...] = q.astype(BF16)


def mid_a(y, h, wout, gmp, gxp, wxq, l):
    m = y.shape[0]
    tm = TM
    rows = lambda c: pl.BlockSpec((tm, c), lambda i: (i, 0))
    full = lambda a: pl.BlockSpec(a.shape, lambda i: (0,) * a.ndim, pipeline_mode=pl.Buffered(1))
    return pl.pallas_call(
        _mid_a_kernel,
        grid=(m // tm,),
        in_specs=[rows(D_MODEL), rows(D_MODEL), _layer_weight(wout, l), full(gmp), full(gxp),
                  _layer_weight(wxq, l)],
        out_specs=[rows(D_MODEL), rows(D_X)],
        out_shape=[jax.ShapeDtypeStruct((m, D_MODEL), F32), jax.ShapeDtypeStruct((m, D_X), BF16)],
        compiler_params=_params("parallel"),
        name="mid_a",
    )(y, h, wout, gmp, gxp, wxq)


def _mid_c_kernel(o_ref, h1_ref, wxo_ref, gxo_ref, gffn_ref, h2_ref, xn_ref):
    h2, xn = _stage_c(o_ref[...], h1_ref[...], wxo_ref, gxo_ref, gffn_ref)
    h2_ref[...] = h2
    xn_ref[...] = xn


def mid_c(o, h1, wxo, gxo, gffn, l):
    m = o.shape[0]
    tm = TM
    rows = lambda c: pl.BlockSpec((tm, c), lambda i: (i, 0))
    full = lambda a: pl.BlockSpec(a.shape, lambda i: (0,) * a.ndim, pipeline_mode=pl.Buffered(1))
    return pl.pallas_call(
        _mid_c_kernel,
        grid=(m // tm,),
        in_specs=[rows(D_X), rows(D_MODEL), _layer_weight(wxo, l), full(gxo), full(gffn)],
        out_specs=[rows(D_MODEL), rows(D_MODEL)],
        out_shape=[jax.ShapeDtypeStruct((m, D_MODEL), F32), jax.ShapeDtypeStruct((m, D_MODEL), BF16)],
        compiler_params=_params("parallel"),
        name="mid_c",
    )(o, h1, wxo, gxo, gffn)


def _attn_sample_kernel(q_ref, k_ref, v_ref, o_ref):
    bblk = q_ref.shape[0]
    lane_head = lax.broadcasted_iota(jnp.int32, (V7X_SUBLANES, D_X), 1) // D_XHEAD
    for b in range(bblk):
        q8 = q_ref[b]
        qbd = jnp.concatenate([jnp.where(lane_head == hd, q8, jnp.zeros_like(q8)) for hd in range(N_XHEADS)], axis=0)
        kb = k_ref[b].astype(BF16)
        vb = v_ref[b].astype(BF16)
        s = lax.dot_general(qbd, kb, _NT, preferred_element_type=F32) * (D_XHEAD ** -0.5)
        of = _dot(_softmax_rows(s).astype(BF16), vb)
        o8 = jnp.zeros((V7X_SUBLANES, D_X), F32)
        for hd in range(N_XHEADS):
            o8 = o8 + jnp.where(lane_head == hd, of[hd * V7X_SUBLANES:(hd + 1) * V7X_SUBLANES], 0.0)
        o_ref[b] = o8.astype(BF16)


def attn_sample(q, k, v, l):
    bblk = BB_ATT
    qs = pl.BlockSpec((bblk, V7X_SUBLANES, D_X), lambda b: (b, 0, 0))
    ms = pl.BlockSpec((None, bblk, N_MEM, D_X), lambda b: (l, b, 0, 0))
    return pl.pallas_call(
        _attn_sample_kernel,
        grid=(DEC_BATCH // bblk,),
        in_specs=[qs, ms, ms],
        out_specs=qs,
        out_shape=jax.ShapeDtypeStruct(q.shape, BF16),
        compiler_params=_params("parallel"),
        name="attn_sample",
    )(q, k, v)


def _gate(cg, cu):
    return (cg * _sigmoid(cg) * cu).astype(BF16)


def _shift_conv3(prev, h, fw_ref):
    rows = h.shape[0]
    ext = jnp.concatenate([prev, h], axis=0)
    return (fw_ref[0:1, :] * ext[HIST_F - 2:HIST_F - 2 + rows]
            + fw_ref[1:2, :] * ext[HIST_F - 1:HIST_F - 1 + rows]
            + fw_ref[2:3, :] * h)


def _up_prompt_kernel(xn_ref, wg_ref, wu_ref, fwg_ref, fwu_ref, o_ref, tg_ref, tu_ref,
                      carg, caru, *, per_seq, sub):
    i = pl.program_id(0)
    j = pl.program_id(1)
    tm = xn_ref.shape[0]

    @pl.when(i % per_seq == 0)
    def _():
        carg[j] = jnp.zeros(carg.shape[1:], F32)
        caru[j] = jnp.zeros(caru.shape[1:], F32)

    prev_g = carg[j]
    prev_u = caru[j]
    for r in range(tm // sub):
        rs = slice(r * sub, (r + 1) * sub)
        hg = _dot(xn_ref[rs, :], wg_ref[...])
        hu = _dot(xn_ref[rs, :], wu_ref[...])
        o_ref[rs, :] = _gate(_shift_conv3(prev_g, hg, fwg_ref), _shift_conv3(prev_u, hu, fwu_ref))
        prev_g = hg[sub - HIST_F:]
        prev_u = hu[sub - HIST_F:]
    carg[j] = prev_g
    caru[j] = prev_u
    tg_ref[...] = prev_g
    tu_ref[...] = prev_u


def up_prompt(xn, wg, wu, fwg, fwu, l):
    m, k = xn.shape
    tm, tn = TM_UP, TN_UP
    nj = pl.cdiv(D_FF, tn)
    col = lambda r: pl.BlockSpec((r, tn), lambda i, j: (0, j))
    wcol = pl.BlockSpec((None, k, tn), lambda i, j: (l, 0, j))
    tail = pl.BlockSpec((HIST_F, tn), lambda i, j: (i, j))
    return pl.pallas_call(
        functools.partial(_up_prompt_kernel, per_seq=SEQ // tm, sub=SUB_UP),
        grid=(m // tm, nj),
        in_specs=[pl.BlockSpec((tm, k), lambda i, j: (i, 0)), wcol, wcol, col(HIST_F), col(HIST_F)],
        out_specs=[pl.BlockSpec((tm, tn), lambda i, j: (i, j)), tail, tail],
        out_shape=[jax.ShapeDtypeStruct((m, D_FF), BF16),
                   jax.ShapeDtypeStruct((m // tm * HIST_F, D_FF), F32),
                   jax.ShapeDtypeStruct((m // tm * HIST_F, D_FF), F32)],
        scratch_shapes=[pltpu.VMEM((nj, HIST_F, tn), F32), pltpu.VMEM((nj, HIST_F, tn), F32)],
        compiler_params=_params("arbitrary", "arbitrary"),
        name="up_prompt",
    )(xn, wg, wu, fwg, fwu)


def _up_sample_kernel(xn_ref, wg_ref, wu_ref, fwg_ref, fwu_ref, p0g_ref, p1g_ref, p0u_ref, p1u_ref,
                      o_ref, tg_ref, tu_ref):
    nb = DEC_BATCH
    xn = xn_ref[...]
    hg = _dot(xn, wg_ref[...])
    hu = _dot(xn, wu_ref[...])
    tg_ref[...] = hg[(DEC_SEQ - 2) * nb:, :]
    tu_ref[...] = hu[(DEC_SEQ - 2) * nb:, :]
    ext_g = [p0g_ref[...], p1g_ref[...]] + [hg[t * nb:(t + 1) * nb] for t in range(DEC_SEQ)]
    ext_u = [p0u_ref[...], p1u_ref[...]] + [hu[t * nb:(t + 1) * nb] for t in range(DEC_SEQ)]
    for t in range(DEC_SEQ):
        cg = fwg_ref[0:1, :] * ext_g[t] + fwg_ref[1:2, :] * ext_g[t + 1] + fwg_ref[2:3, :] * ext_g[t + 2]
        cu = fwu_ref[0:1, :] * ext_u[t] + fwu_ref[1:2, :] * ext_u[t + 1] + fwu_ref[2:3, :] * ext_u[t + 2]
        o_ref[t * nb:(t + 1) * nb, :] = _gate(cg, cu)


def up_sample(xn, wg, wu, fwg, fwu, p0g, p1g, p0u, p1u, l):
    m, k = xn.shape
    tn = TN_UP
    nj = pl.cdiv(D_FF, tn)
    col = lambda r: pl.BlockSpec((r, tn), lambda j: (0, j))
    wcol = pl.BlockSpec((None, k, tn), lambda j: (l, 0, j))
    nt = (FFN_CONV_WIDTH - 1) * DEC_BATCH
    return pl.pallas_call(
        _up_sample_kernel,
        grid=(nj,),
        in_specs=[pl.BlockSpec((m, k), lambda j: (0, 0)), wcol, wcol, col(HIST_F), col(HIST_F),
                  col(DEC_BATCH), col(DEC_BATCH), col(DEC_BATCH), col(DEC_BATCH)],
        out_specs=[col(m), col(nt), col(nt)],
        out_shape=[jax.ShapeDtypeStruct((m, D_FF), BF16),
                   jax.ShapeDtypeStruct((nt, D_FF), F32),
                   jax.ShapeDtypeStruct((nt, D_FF), F32)],
        compiler_params=_params("parallel"),
        name="up_sample",
    )(xn, wg, wu, fwg, fwu, p0g, p1g, p0u, p1u)


def _down_kernel(g_ref, w_ref, h_ref, gn_ref, o_ref, acc):
    j = pl.program_id(1)
    tn = w_ref.shape[1]
    acc[:, pl.ds(pl.multiple_of(j * tn, tn), tn)] = _dot(g_ref[...], w_ref[...])

    @pl.when(j == pl.num_programs(1) - 1)
    def _():
        o_ref[...] = h_ref[...] + _rms(acc[...], gn_ref[...])


def down(g, w, h, gn, l):
    m, k = g.shape
    tm, tn = TM, TN_DOWN
    return pl.pallas_call(
        _down_kernel,
        grid=(m // tm, D_MODEL // tn),
        in_specs=[pl.BlockSpec((tm, k), lambda i, j: (i, 0)),
                  pl.BlockSpec((None, k, tn), lambda i, j: (l, 0, j)),
                  pl.BlockSpec((tm, D_MODEL), lambda i, j: (i, 0)),
                  pl.BlockSpec((1, D_MODEL), lambda i, j: (0, 0))],
        out_specs=pl.BlockSpec((tm, D_MODEL), lambda i, j: (i, 0)),
        out_shape=jax.ShapeDtypeStruct((m, D_MODEL), F32),
        scratch_shapes=[pltpu.VMEM((tm, D_MODEL), F32)],
        compiler_params=_params("parallel", "arbitrary"),
        name="down",
    )(g, w, h, gn)


def _pad_rows(a, rows):
    return jnp.pad(a, ((0, rows - a.shape[0]), (0, 0)))


def kernel(x_prompt, x_sample, cache_mem_k, cache_mem_v, state_conv_b, state_pool, state_sconv, state_ffn_conv, mem_prompt, g_mix_pre, g_mix_post, g_mem, g_x_pre, g_x_post, g_ffn_pre, g_ffn_post, w_in, w_out, a_norm_g, a_ws, a_bs, b_conv_w, b_conv_b, b_gn_g, b_gn_b, c_lin, c_scale, d_conv_w, w_xq, w_xk, w_xv, w_xo, w_up, f_conv_w, w_down):
    nb, ns = DEC_BATCH, DEC_SEQ
    w_in_b = cast_weights(w_in, 512)
    w_out_b = cast_weights(w_out, 1024)
    w_xq_b = cast_weights(w_xq, D_MODEL)
    w_kv_b = cast_weight_pair(w_xk, w_xv)
    w_xo_b = cast_weights(w_xo, D_X)
    w_g_b = cast_weights(w_up, 256, D_FF, 0)
    w_u_b = cast_weights(w_up, 256, D_FF, 1)
    w_down_b = cast_weights(w_down, D_FF // 8)

    hp = x_prompt.reshape(BATCH * SEQ, D_MODEL)
    hs = jnp.transpose(x_sample, (1, 0, 2)).reshape(ns * nb, D_MODEL)
    mem = mem_prompt.reshape(BATCH * N_MEM, D_MODEL)
    row = lambda a: a.reshape(1, -1)
    st_b = state_conv_b.reshape(DEPTH, nb, -1)
    st_p = state_pool.reshape(DEPTH, nb, -1)
    st_s = state_sconv.reshape(DEPTH, nb, -1)

    outs = [[] for _ in range(11)]
    for l in range(DEPTH):
        ag, bb, gng, gnb, cs = (row(a[l]) for a in (a_norm_g, b_conv_b, b_gn_g, b_gn_b, c_scale))
        abt = jnp.transpose(a_bs[l])
        bw = _pad_rows(b_conv_w[l], HIST_B)
        dw = _pad_rows(d_conv_w[l], HIST_D)
        fwg = _pad_rows(f_conv_w[l, :, :D_FF], HIST_F)
        fwu = _pad_rows(f_conv_w[l, :, D_FF:], HIST_F)
        aw4 = jnp.repeat(jnp.transpose(a_ws[l, :, :ns, :ns], (1, 2, 0)).reshape(ns * ns, N_SUB), D_SUB, axis=1)
        ab4 = jnp.repeat(jnp.transpose(a_bs[l, :, :ns]), D_SUB, axis=1)
        gmp, gxp, gxo, gfp, gfo = (row(a[l]) for a in (g_mix_post, g_x_pre, g_x_post, g_ffn_pre, g_ffn_post))
        gpre = row(g_mix_pre[l])

        kv = norm_matmul(mem, row(g_mem[l]), w_kv_b, l, TM, D_X)
        mk = kv[:, :D_X].reshape(BATCH, N_MEM, D_X)
        mv = kv[:, D_X:].reshape(BATCH, N_MEM, D_X)

        z = norm_matmul(hp, gpre, w_in_b, l, TM_IN, TN_IN)
        y, nbp, npp, nsp = mixer_prompt(z.reshape(BATCH, SEQ, D_IN), ag, a_ws[l], abt, bw, bb, gng, gnb,
                                        c_lin[l], cs, dw)
        h2, xn = mid_prompt(y.reshape(BATCH * SEQ, D_MODEL), hp, w_out_b, gmp, gxp, w_xq_b, mk, mv,
                            w_xo_b, gxo, gfp, l)
        gate, tg, tu = up_prompt(xn, w_g_b, w_u_b, fwg, fwu, l)
        hp = down(gate, w_down_b, h2, gfo, l)
        per_seq = SEQ // TM_UP
        tails = jnp.concatenate([tg, tu], axis=-1).reshape(BATCH, per_seq, HIST_F, 2 * D_FF)
        p_out = (mk, mv, nbp[:, HIST_B - (CONV_B_WIDTH - 1):], npp[:, HIST_C - POOL_PREV:],
                 nsp[:, HIST_D - (SCONV_WIDTH - 1):], tails[:, -1, HIST_F - (FFN_CONV_WIDTH - 1):])

        zs = norm_matmul(hs, gpre, w_in_b, l, TM, TN_IN)
        ys, nbs, nps, nss, vs = mixer_sample(zs.reshape(ns, nb, D_IN), st_b, st_p, st_s, ag, aw4, ab4, bw, bb,
                                             gng, gnb, c_lin[l], cs, dw, l)
        h1s, qs = mid_a(ys.reshape(ns * nb, D_MODEL), hs, w_out_b, gmp, gxp, w_xq_b, l)
        q8 = jnp.pad(jnp.transpose(qs.reshape(ns, nb, D_X), (1, 0, 2)), ((0, 0), (0, V7X_SUBLANES - ns), (0, 0)))
        o8 = attn_sample(q8, cache_mem_k, cache_mem_v, l)
        os_ = jnp.transpose(o8[:, :ns], (1, 0, 2)).reshape(ns * nb, D_X)
        h2s, xns = mid_c(os_, h1s, w_xo_b, gxo, gfp, l)
        sf = state_ffn_conv[l]
        gates, tgs, tus = up_sample(xns, w_g_b, w_u_b, fwg, fwu,
                                    sf[:, 0, :D_FF], sf[:, 1, :D_FF], sf[:, 0, D_FF:], sf[:, 1, D_FF:], l)
        hs = down(gates, w_down_b, h2s, gfo, l)
        nffn_s = jnp.transpose(jnp.concatenate([tgs, tus], axis=-1).reshape(FFN_CONV_WIDTH - 1, nb, 2 * D_FF),
                               (1, 0, 2))
        s_out = (nbs.reshape(nb, CONV_B_WIDTH - 1, D_GROUP), nps.reshape(nb, POOL_PREV, D_GROUP),
                 nss.reshape(nb, SCONV_WIDTH - 1, D_GROUP), nffn_s, vs.reshape(nb, ns, D_GROUP))
        for acc, val in zip(outs, p_out + s_out):
            acc.append(val)

    return (hp.reshape(BATCH, SEQ, D_MODEL),
            jnp.transpose(hs.reshape(ns, nb, D_MODEL), (1, 0, 2)),
            *(jnp.stack(o) for o in outs))
```

```python
import functools

import jax
import jax.numpy as jnp
from jax import lax
from jax.experimental import pallas as pl
from jax.experimental.pallas import tpu as pltpu

F32 = jnp.float32
BF16 = jnp.bfloat16

D_MODEL = 2048
BATCH = 4
SEQ = 2048
DEPTH = 4
DEC_BATCH = 128
DEC_SEQ = 4
PAST_LEN = 16384
D_GROUP = 512
N_SUB = 4
D_SUB = 128
D_IN = 8 * D_GROUP
CHUNK = 128
CONV_B_WIDTH = 31
POOL_WINDOWS = (2, 4, 8, 16)
POOL_PREV = 15
SCONV_WIDTH = 3
FFN_CONV_WIDTH = 3
D_FF = 5504
N_MEM = 256
N_XHEADS = 4
D_XHEAD = 128
D_X = 512
EPS = 1e-6

V7X_SUBLANES = 8
V7X_LANES = 128
V7X_VMEM_LIMIT_BYTES = 56 * 1024 * 1024

HIST_B = 32
HIST_C = 16
HIST_D = 8
HIST_F = 8

TM = 512
TM_UP = 1024
SUB_UP = 256
TM_IN = 1024
SUB_MID = 512
TT = 256
ROWS = 32
TN_IN = 1024
TN_UP = 512
FF_PAD = -(-D_FF // TN_UP) * TN_UP
BB_MIX = 32
BB_ATT = 8


def _params(*sem):
    return pltpu.CompilerParams(dimension_semantics=sem, vmem_limit_bytes=V7X_VMEM_LIMIT_BYTES)


def _rms(x, g):
    return x * lax.rsqrt(jnp.mean(x * x, axis=-1, keepdims=True) + EPS) * g


def _sigmoid(x):
    return 1.0 / (1.0 + jnp.exp(-x))


def _dot(a, b):
    return jnp.dot(a, b, preferred_element_type=F32)


def _cast_kernel(x_ref, o_ref):
    o_ref[...] = x_ref[...].astype(BF16)


def cast_weights(w, rows, cols=None, col_block=0):
    depth, r, c = w.shape
    cols = c if cols is None else cols
    return pl.pallas_call(
        _cast_kernel,
        grid=(depth, r // rows),
        in_specs=[pl.BlockSpec((None, rows, cols), lambda l, i: (l, i, col_block))],
        out_specs=pl.BlockSpec((None, rows, cols), lambda l, i: (l, i, 0)),
        out_shape=jax.ShapeDtypeStruct((depth, r, cols), BF16),
        compiler_params=_params("parallel", "parallel"),
        name="cast_weights",
    )(w)


def _cast_pad_cols_kernel(x_ref, o_ref):
    n = x_ref.shape[-1]
    o_ref[:, 0:n] = x_ref[...].astype(BF16)
    o_ref[:, n:] = jnp.zeros((o_ref.shape[0], o_ref.shape[1] - n), BF16)


def cast_weights_pad_cols(w, rows, cols, col_block, padded):
    depth, r, _ = w.shape
    return pl.pallas_call(
        _cast_pad_cols_kernel,
        grid=(depth, r // rows),
        in_specs=[pl.BlockSpec((None, rows, cols), lambda l, i: (l, i, col_block))],
        out_specs=pl.BlockSpec((None, rows, padded), lambda l, i: (l, i, 0)),
        out_shape=jax.ShapeDtypeStruct((depth, r, padded), BF16),
        compiler_params=_params("parallel", "parallel"),
        name="cast_weights_pad_cols",
    )(w)


def _cast_pad_rows_kernel(x_ref, o_ref, *, valid_rows):
    rows = o_ref.shape[0]
    row = pl.program_id(1) * rows + lax.broadcasted_iota(jnp.int32, o_ref.shape, 0)
    o_ref[...] = jnp.where(row < valid_rows, x_ref[...], 0.0).astype(BF16)


def cast_weights_pad_rows(w, rows, padded):
    depth, r, c = w.shape
    return pl.pallas_call(
        functools.partial(_cast_pad_rows_kernel, valid_rows=r),
        grid=(depth, padded // rows),
        in_specs=[pl.BlockSpec((None, rows, c), lambda l, i: (l, i, 0))],
        out_specs=pl.BlockSpec((None, rows, c), lambda l, i: (l, i, 0)),
        out_shape=jax.ShapeDtypeStruct((depth, padded, c), BF16),
        compiler_params=_params("parallel", "parallel"),
        name="cast_weights_pad_rows",
    )(w)


def _cast_pair_kernel(a_ref, b_ref, o_ref):
    n = a_ref.shape[-1]
    o_ref[:, 0:n] = a_ref[...].astype(BF16)
    o_ref[:, n:2 * n] = b_ref[...].astype(BF16)


def cast_weight_pair(a, b):
    depth, r, c = a.shape
    spec = pl.BlockSpec((None, r, c), lambda l: (l, 0, 0))
    return pl.pallas_call(
        _cast_pair_kernel,
        grid=(depth,),
        in_specs=[spec, spec],
        out_specs=pl.BlockSpec((None, r, 2 * c), lambda l: (l, 0, 0)),
        out_shape=jax.ShapeDtypeStruct((depth, r, 2 * c), BF16),
        compiler_params=_params("parallel"),
        name="cast_weight_pair",
    )(a, b)


def _norm_matmul_kernel(x_ref, g_ref, w_ref, o_ref, xn_ref):
    @pl.when(pl.program_id(1) == 0)
    def _():
        xn_ref[...] = _rms(x_ref[...], g_ref[...]).astype(BF16)

    o_ref[...] = _dot(xn_ref[...], w_ref[...])


def norm_matmul(x, g, w, l, tm, tn):
    m, k = x.shape
    n = w.shape[2]
    return pl.pallas_call(
        _norm_matmul_kernel,
        grid=(m // tm, n // tn),
        in_specs=[pl.BlockSpec((tm, k), lambda i, j: (i, 0)),
                  pl.BlockSpec((1, k), lambda i, j: (0, 0)),
                  pl.BlockSpec((None, k, tn), lambda i, j: (l, 0, j))],
        out_specs=pl.BlockSpec((tm, tn), lambda i, j: (i, j)),
        out_shape=jax.ShapeDtypeStruct((m, n), F32),
        scratch_shapes=[pltpu.VMEM((tm, k), BF16)],
        compiler_params=_params("parallel", "arbitrary"),
        name="norm_matmul",
    )(x, g, w)


def _layer_norm(v, g):
    mu = jnp.mean(v, axis=-1, keepdims=True)
    d = v - mu
    var = jnp.mean(d * d, axis=-1, keepdims=True)
    return d * lax.rsqrt(var + EPS) * g


def _group_norm_silu(y, g, b):
    mu = jnp.mean(y, axis=-1, keepdims=True)
    d = y - mu
    var = jnp.mean(d * d, axis=-1, keepdims=True)
    yn = d * lax.rsqrt(var + EPS) * g + b
    return yn * _sigmoid(yn)


def _mixer_prompt_kernel(z_ref, ag_ref, aws_ref, abt_ref, bw_ref, bb_ref, gng_ref, gnb_ref,
                         clin_ref, cs_ref, dw_ref,
                         y_ref, nb_ref, np_ref, ns_ref,
                         extb, extc, extd, pooled):
    t = pl.program_id(1)
    tt = y_ref.shape[0]

    @pl.when(t == 0)
    def _():
        for s in range(V7X_SUBLANES):
            extb[s, 0:HIST_B, :] = jnp.zeros((HIST_B, D_GROUP), F32)
            extb[s, tt + HIST_B - V7X_SUBLANES:tt + HIST_B, :] = jnp.zeros((V7X_SUBLANES, D_GROUP), F32)
        extc[0:HIST_C, :] = jnp.zeros((HIST_C, D_GROUP), F32)
        extd[0:HIST_D, :] = jnp.zeros((HIST_D, D_GROUP), F32)

    vn = _layer_norm(z_ref[:, D_GROUP:2 * D_GROUP], ag_ref[...]).astype(BF16)
    row = lax.broadcasted_iota(jnp.int32, (CHUNK, CHUNK), 0)
    col = lax.broadcasted_iota(jnp.int32, (CHUNK, CHUNK), 1)
    for h in range(N_SUB):
        wm = jnp.where(row >= col, aws_ref[h], 0.0).astype(BF16)
        bias = abt_ref[:, h:h + 1]
        for c in range(tt // CHUNK):
            rs = slice(c * CHUNK, (c + 1) * CHUNK)
            ls = slice(h * D_SUB, (h + 1) * D_SUB)
            zz = _dot(wm, vn[rs, ls]) + bias
            y_ref[rs, ls] = (z_ref[rs, ls] * zz).astype(BF16)

    hb = z_ref[:, 2 * D_GROUP:3 * D_GROUP] * _sigmoid(z_ref[:, 3 * D_GROUP:4 * D_GROUP])
    for s in range(V7X_SUBLANES):
        extb[s, HIST_B - s:HIST_B - s + tt, :] = hb
    for c in range(tt // ROWS):
        for g in range(N_SUB):
            ls = slice(g * D_SUB, (g + 1) * D_SUB)
            acc = jnp.zeros((ROWS, D_SUB), F32)
            for k in range(CONV_B_WIDTH):
                tiles, s = divmod(HIST_B - (CONV_B_WIDTH - 1) + k, V7X_SUBLANES)
                r0 = c * ROWS + tiles * V7X_SUBLANES
                acc = acc + bw_ref[k:k + 1, ls] * extb[s, r0:r0 + ROWS, ls]
            yb = _group_norm_silu(acc + bb_ref[:, ls], gng_ref[:, ls], gnb_ref[:, ls])
            y_ref[c * ROWS:(c + 1) * ROWS, D_GROUP + g * D_SUB:D_GROUP + (g + 1) * D_SUB] = yb.astype(BF16)

    extc[HIST_C:HIST_C + tt, :] = z_ref[:, 4 * D_GROUP:5 * D_GROUP]
    for c in range(tt // ROWS):
        pos = t * tt + c * ROWS + lax.broadcasted_iota(jnp.int32, (ROWS, 1), 0)
        for g, win in enumerate(POOL_WINDOWS):
            ls = slice(g * D_SUB, (g + 1) * D_SUB)
            r0 = c * ROWS + HIST_C
            x = extc[r0:r0 + ROWS, ls]
            s = x
            for i in range(1, win):
                s = s + extc[r0 - i:r0 - i + ROWS, ls]
            cnt = jnp.minimum(pos + 1, win).astype(F32)
            pooled[c * ROWS:(c + 1) * ROWS, ls] = (s / cnt - x).astype(BF16)
    for g in range(N_SUB):
        ls = slice(g * D_SUB, (g + 1) * D_SUB)
        yc = _dot(pooled[:, ls], clin_ref[g].astype(BF16)) * cs_ref[:, ls]
        y_ref[:, 2 * D_GROUP + g * D_SUB:2 * D_GROUP + (g + 1) * D_SUB] = yc.astype(BF16)

    extd[HIST_D:HIST_D + tt, :] = z_ref[:, 7 * D_GROUP:8 * D_GROUP] * z_ref[:, 5 * D_GROUP:6 * D_GROUP]
    for c in range(tt // ROWS):
        r0 = c * ROWS + HIST_D
        conv = (dw_ref[0:1, :] * extd[r0 - 2:r0 - 2 + ROWS, :]
                + dw_ref[1:2, :] * extd[r0 - 1:r0 - 1 + ROWS, :]
                + dw_ref[2:3, :] * extd[r0:r0 + ROWS, :])
        rs = slice(c * ROWS, (c + 1) * ROWS)
        y_ref[rs, 3 * D_GROUP:4 * D_GROUP] = (z_ref[rs, 6 * D_GROUP:7 * D_GROUP] * conv).astype(BF16)

    for s in range(V7X_SUBLANES):
        extb[s, 0:HIST_B, :] = extb[s, tt:tt + HIST_B, :]
    extc[0:HIST_C, :] = extc[tt:tt + HIST_C, :]
    extd[0:HIST_D, :] = extd[tt:tt + HIST_D, :]

    @pl.when(t == pl.num_programs(1) - 1)
    def _():
        nb_ref[...] = extb[0, 0:HIST_B, :]
        np_ref[...] = extc[0:HIST_C, :]
        ns_ref[...] = extd[0:HIST_D, :]


def mixer_prompt(z, ag, aws, abt, bw, bb, gng, gnb, clin, cs, dw):
    tt = TT
    full = lambda a: pl.BlockSpec(a.shape, lambda b, t: (0,) * a.ndim)
    hist = lambda r: pl.BlockSpec((None, r, D_GROUP), lambda b, t: (b, 0, 0))
    return pl.pallas_call(
        _mixer_prompt_kernel,
        grid=(BATCH, SEQ // tt),
        in_specs=[pl.BlockSpec((None, tt, D_IN), lambda b, t: (b, t, 0)),
                  full(ag), full(aws), full(abt), full(bw), full(bb), full(gng), full(gnb),
                  full(clin), full(cs), full(dw)],
        out_specs=[pl.BlockSpec((None, tt, D_MODEL), lambda b, t: (b, t, 0)),
                   hist(HIST_B), hist(HIST_C), hist(HIST_D)],
        out_shape=[jax.ShapeDtypeStruct((BATCH, SEQ, D_MODEL), BF16),
                   jax.ShapeDtypeStruct((BATCH, HIST_B, D_GROUP), F32),
                   jax.ShapeDtypeStruct((BATCH, HIST_C, D_GROUP), F32),
                   jax.ShapeDtypeStruct((BATCH, HIST_D, D_GROUP), F32)],
        scratch_shapes=[pltpu.VMEM((V7X_SUBLANES, HIST_B + tt, D_GROUP), F32),
                        pltpu.VMEM((HIST_C + tt, D_GROUP), F32),
                        pltpu.VMEM((HIST_D + tt, D_GROUP), F32),
                        pltpu.VMEM((tt, D_GROUP), BF16)],
        compiler_params=_params("parallel", "arbitrary"),
        name="mixer_prompt",
    )(z, ag, aws, abt, bw, bb, gng, gnb, clin, cs, dw)


def _mixer_sample_kernel(z_ref, cb_ref, cp_ref, csc_ref, ag_ref, aw4_ref, ab4_ref, bw_ref, bb_ref,
                         gng_ref, gnb_ref, clin_ref, cs_ref, dw_ref,
                         y_ref, nb_ref, np_ref, nsc_ref, v_ref, pooled):
    bblk = z_ref.shape[1]
    blk = lambda r: slice(r * D_GROUP, (r + 1) * D_GROUP)

    vn = [_layer_norm(z_ref[t, :, D_GROUP:2 * D_GROUP], ag_ref[...]) for t in range(DEC_SEQ)]
    for i in range(DEC_SEQ):
        v_ref[:, blk(i)] = vn[i]
        zz = ab4_ref[i:i + 1, :]
        for j in range(i + 1):
            zz = zz + aw4_ref[i * DEC_SEQ + j:i * DEC_SEQ + j + 1, :] * vn[j]
        y_ref[i, :, 0:D_GROUP] = (z_ref[i, :, 0:D_GROUP] * zz).astype(BF16)

    nprev = CONV_B_WIDTH - 1
    hb = [z_ref[t, :, 2 * D_GROUP:3 * D_GROUP] * _sigmoid(z_ref[t, :, 3 * D_GROUP:4 * D_GROUP])
          for t in range(DEC_SEQ)]
    ext_b = lambda r: cb_ref[:, blk(r)] if r < nprev else hb[r - nprev]
    for t in range(DEC_SEQ):
        acc = jnp.zeros((bblk, D_GROUP), F32)
        for k in range(CONV_B_WIDTH):
            acc = acc + bw_ref[k:k + 1, :] * ext_b(t + k)
        acc = acc + bb_ref[...]
        for g in range(N_SUB):
            ls = slice(g * D_SUB, (g + 1) * D_SUB)
            yb = _group_norm_silu(acc[:, ls], gng_ref[:, ls], gnb_ref[:, ls])
            y_ref[t, :, D_GROUP + g * D_SUB:D_GROUP + (g + 1) * D_SUB] = yb.astype(BF16)
    for r in range(nprev):
        nb_ref[:, blk(r)] = ext_b(r + DEC_SEQ)

    cx = [z_ref[t, :, 4 * D_GROUP:5 * D_GROUP] for t in range(DEC_SEQ)]
    ext_c = lambda r: cp_ref[:, blk(r)] if r < POOL_PREV else cx[r - POOL_PREV]
    for t in range(DEC_SEQ):
        for g, win in enumerate(POOL_WINDOWS):
            ls = slice(g * D_SUB, (g + 1) * D_SUB)
            s = cx[t][:, ls]
            for i in range(1, win):
                s = s + ext_c(POOL_PREV + t - i)[:, ls]
            cnt = float(min(PAST_LEN + t + 1, win))
            pooled[t * bblk:(t + 1) * bblk, ls] = (s / cnt - cx[t][:, ls]).astype(BF16)
    for g in range(N_SUB):
        ls = slice(g * D_SUB, (g + 1) * D_SUB)
        yc = _dot(pooled[:, ls], clin_ref[g].astype(BF16)) * cs_ref[:, ls]
        for t in range(DEC_SEQ):
            y_ref[t, :, 2 * D_GROUP + g * D_SUB:2 * D_GROUP + (g + 1) * D_SUB] = (
                yc[t * bblk:(t + 1) * bblk].astype(BF16))
    for r in range(POOL_PREV):
        np_ref[:, blk(r)] = ext_c(r + DEC_SEQ)

    nsp = SCONV_WIDTH - 1
    hd = [z_ref[t, :, 7 * D_GROUP:8 * D_GROUP] * z_ref[t, :, 5 * D_GROUP:6 * D_GROUP] for t in range(DEC_SEQ)]
    ext_d = lambda r: csc_ref[:, blk(r)] if r < nsp else hd[r - nsp]
    for t in range(DEC_SEQ):
        conv = dw_ref[0:1, :] * ext_d(t) + dw_ref[1:2, :] * ext_d(t + 1) + dw_ref[2:3, :] * ext_d(t + 2)
        y_ref[t, :, 3 * D_GROUP:4 * D_GROUP] = (z_ref[t, :, 6 * D_GROUP:7 * D_GROUP] * conv).astype(BF16)
    for r in range(nsp):
        nsc_ref[:, blk(r)] = ext_d(r + DEC_SEQ)


def mixer_sample(z, cb, cp, csc, ag, aw4, ab4, bw, bb, gng, gnb, clin, cs, dw, l):
    bblk = BB_MIX
    full = lambda a: pl.BlockSpec(a.shape, lambda b: (0,) * a.ndim)
    st_in = lambda a: pl.BlockSpec((None, bblk, a.shape[2]), lambda b: (l, b, 0))
    st = lambda a: pl.BlockSpec((bblk, a.shape[2]), lambda b: (b, 0))
    nv = DEC_SEQ * D_GROUP
    return pl.pallas_call(
        _mixer_sample_kernel,
        grid=(DEC_BATCH // bblk,),
        in_specs=[pl.BlockSpec((DEC_SEQ, bblk, D_IN), lambda b: (0, b, 0)),
                  st_in(cb), st_in(cp), st_in(csc),
                  full(ag), full(aw4), full(ab4), full(bw), full(bb), full(gng), full(gnb),
                  full(clin), full(cs), full(dw)],
        out_specs=[pl.BlockSpec((DEC_SEQ, bblk, D_MODEL), lambda b: (0, b, 0)),
                   st(cb), st(cp), st(csc),
                   pl.BlockSpec((bblk, nv), lambda b: (b, 0))],
        out_shape=[jax.ShapeDtypeStruct((DEC_SEQ, DEC_BATCH, D_MODEL), BF16),
                   jax.ShapeDtypeStruct(cb.shape[1:], F32),
                   jax.ShapeDtypeStruct(cp.shape[1:], F32),
                   jax.ShapeDtypeStruct(csc.shape[1:], F32),
                   jax.ShapeDtypeStruct((DEC_BATCH, nv), F32)],
        scratch_shapes=[pltpu.VMEM((DEC_SEQ * bblk, D_GROUP), BF16)],
        compiler_params=_params("parallel"),
        name="mixer_sample",
    )(z, cb, cp, csc, ag, aw4, ab4, bw, bb, gng, gnb, clin, cs, dw)


def _stage_a(y, h, wout_ref, gpost_ref, gpre_ref, wxq_ref):
    h1 = h + _rms(_dot(y, wout_ref[...]), gpost_ref[...])
    q = _dot(_rms(h1, gpre_ref[...]).astype(BF16), wxq_ref[...])
    return h1, q


def _stage_c(o, h1, wxo_ref, gpost_ref, gffn_ref):
    h2 = h1 + _rms(_dot(o, wxo_ref[...]), gpost_ref[...])
    return h2, _rms(h2, gffn_ref[...]).astype(BF16)


def _softmax_rows(s):
    e = jnp.exp(s - jnp.max(s, axis=-1, keepdims=True))
    return e / jnp.sum(e, axis=-1, keepdims=True)


_NT = (((1,), (1,)), ((), ()))


def _mid_prompt_kernel(y_ref, h_ref, wout_ref, gmp_ref, gxp_ref, wxq_ref, mk_ref, mv_ref, wxo_ref,
                       gxo_ref, gffn_ref, h2_ref, xn_ref):
    k = mk_ref[...].astype(BF16)
    v = mv_ref[...].astype(BF16)
    for r in range(y_ref.shape[0] // SUB_MID):
        rs = slice(r * SUB_MID, (r + 1) * SUB_MID)
        h1, q = _stage_a(y_ref[rs, :], h_ref[rs, :], wout_ref, gmp_ref, gxp_ref, wxq_ref)
        q = q.astype(BF16)
        heads = []
        for hd in range(N_XHEADS):
            ls = slice(hd * D_XHEAD, (hd + 1) * D_XHEAD)
            s = lax.dot_general(q[:, ls], k[:, ls], _NT, preferred_element_type=F32) * (D_XHEAD ** -0.5)
            heads.append(_dot(_softmax_rows(s).astype(BF16), v[:, ls]))
        o = jnp.concatenate(heads, axis=-1).astype(BF16)
        h2, xn = _stage_c(o, h1, wxo_ref, gxo_ref, gffn_ref)
        h2_ref[rs, :] = h2
        xn_ref[rs, :] = xn


def _layer_weight(w, l):
    return pl.BlockSpec((None,) + w.shape[1:], lambda i: (l, 0, 0), pipeline_mode=pl.Buffered(1))


def mid_prompt(y, h, wout, gmp, gxp, wxq, mk, mv, wxo, gxo, gffn, l):
    m = y.shape[0]
    tm = TM
    per_seq = SEQ // tm
    rows = lambda c: pl.BlockSpec((tm, c), lambda i: (i, 0))
    full = lambda a: pl.BlockSpec(a.shape, lambda i: (0,) * a.ndim, pipeline_mode=pl.Buffered(1))
    mem = pl.BlockSpec((None, N_MEM, D_X), lambda i: (i // per_seq, 0, 0))
    return pl.pallas_call(
        _mid_prompt_kernel,
        grid=(m // tm,),
        in_specs=[rows(D_MODEL), rows(D_MODEL), _layer_weight(wout, l), full(gmp), full(gxp),
                  _layer_weight(wxq, l), mem, mem, _layer_weight(wxo, l), full(gxo), full(gffn)],
        out_specs=[rows(D_MODEL), rows(D_MODEL)],
        out_shape=[jax.ShapeDtypeStruct((m, D_MODEL), F32), jax.ShapeDtypeStruct((m, D_MODEL), BF16)],
        compiler_params=_params("parallel"),
        name="mid_prompt",
    )(y, h, wout, gmp, gxp, wxq, mk, mv, wxo, gxo, gffn)


def _mid_a_kernel(y_ref, h_ref, wout_ref, gmp_ref, gxp_ref, wxq_ref, h1_ref, q_ref):
    h1, q = _stage_a(y_ref[...], h_ref[...], wout_ref, gmp_ref, gxp_ref, wxq_ref)
    h1_ref[...] = h1
    q_ref[...] = q.astype(BF16)


def mid_a(y, h, wout, gmp, gxp, wxq, l):
    m = y.shape[0]
    tm = TM
    rows = lambda c: pl.BlockSpec((tm, c), lambda i: (i, 0))
    full = lambda a: pl.BlockSpec(a.shape, lambda i: (0,) * a.ndim, pipeline_mode=pl.Buffered(1))
    return pl.pallas_call(
        _mid_a_kernel,
        grid=(m // tm,),
        in_specs=[rows(D_MODEL), rows(D_MODEL), _layer_weight(wout, l), full(gmp), full(gxp),
                  _layer_weight(wxq, l)],
        out_specs=[rows(D_MODEL), rows(D_X)],
        out_shape=[jax.ShapeDtypeStruct((m, D_MODEL), F32), jax.ShapeDtypeStruct((m, D_X), BF16)],
        compiler_params=_params("parallel"),
        name="mid_a",
    )(y, h, wout, gmp, gxp, wxq)


def _mid_c_kernel(o_ref, h1_ref, wxo_ref, gxo_ref, gffn_ref, h2_ref, xn_ref):
    h2, xn = _stage_c(o_ref[...], h1_ref[...], wxo_ref, gxo_ref, gffn_ref)
    h2_ref[...] = h2
    xn_ref[...] = xn


def mid_c(o, h1, wxo, gxo, gffn, l):
    m = o.shape[0]
    tm = TM
    rows = lambda c: pl.BlockSpec((tm, c), lambda i: (i, 0))
    full = lambda a: pl.BlockSpec(a.shape, lambda i: (0,) * a.ndim, pipeline_mode=pl.Buffered(1))
    return pl.pallas_call(
        _mid_c_kernel,
        grid=(m // tm,),
        in_specs=[rows(D_X), rows(D_MODEL), _layer_weight(wxo, l), full(gxo), full(gffn)],
        out_specs=[rows(D_MODEL), rows(D_MODEL)],
        out_shape=[jax.ShapeDtypeStruct((m, D_MODEL), F32), jax.ShapeDtypeStruct((m, D_MODEL), BF16)],
        compiler_params=_params("parallel"),
        name="mid_c",
    )(o, h1, wxo, gxo, gffn)


def _attn_sample_kernel(q_ref, k_ref, v_ref, o_ref):
    bblk = q_ref.shape[0]
    lane_head = lax.broadcasted_iota(jnp.int32, (V7X_SUBLANES, D_X), 1) // D_XHEAD
    for b in range(bblk):
        q8 = q_ref[b]
        qbd = jnp.concatenate([jnp.where(lane_head == hd, q8, jnp.zeros_like(q8)) for hd in range(N_XHEADS)], axis=0)
        kb = k_ref[b].astype(BF16)
        vb = v_ref[b].astype(BF16)
        s = lax.dot_general(qbd, kb, _NT, preferred_element_type=F32) * (D_XHEAD ** -0.5)
        of = _dot(_softmax_rows(s).astype(BF16), vb)
        o8 = jnp.zeros((V7X_SUBLANES, D_X), F32)
        for hd in range(N_XHEADS):
            o8 = o8 + jnp.where(lane_head == hd, of[hd * V7X_SUBLANES:(hd + 1) * V7X_SUBLANES], 0.0)
        o_ref[b] = o8.astype(BF16)


def attn_sample(q, k, v, l):
    bblk = BB_ATT
    qs = pl.BlockSpec((bblk, V7X_SUBLANES, D_X), lambda b: (b, 0, 0))
    ms = pl.BlockSpec((None, bblk, N_MEM, D_X), lambda b: (l, b, 0, 0))
    return pl.pallas_call(
        _attn_sample_kernel,
        grid=(DEC_BATCH // bblk,),
        in_specs=[qs, ms, ms],
        out_specs=qs,
        out_shape=jax.ShapeDtypeStruct(q.shape, BF16),
        compiler_params=_params("parallel"),
        name="attn_sample",
    )(q, k, v)


def _gate(cg, cu):
    return (cg * _sigmoid(cg) * cu).astype(BF16)


def _shift_conv3(prev, h, fw_ref):
    rows = h.shape[0]
    ext = jnp.concatenate([prev, h], axis=0)
    return (fw_ref[0:1, :] * ext[HIST_F - 2:HIST_F - 2 + rows]
            + fw_ref[1:2, :] * ext[HIST_F - 1:HIST_F - 1 + rows]
            + fw_ref[2:3, :] * h)


def _ffn_prompt_kernel(xn_ref, wg_ref, wu_ref, fwg_ref, fwu_ref, wd_ref, h_ref, gn_ref,
                       o_ref, tg_ref, tu_ref, carg, caru, *, per_seq, sub):
    i = pl.program_id(0)
    j = pl.program_id(1)
    tm = xn_ref.shape[0]

    @pl.when(i % per_seq == 0)
    def _():
        carg[j] = jnp.zeros(carg.shape[1:], F32)
        caru[j] = jnp.zeros(caru.shape[1:], F32)

    @pl.when(j == 0)
    def _():
        o_ref[...] = jnp.zeros(o_ref.shape, F32)

    prev_g = carg[j]
    prev_u = caru[j]
    for r in range(tm // sub):
        rs = slice(r * sub, (r + 1) * sub)
        hg = _dot(xn_ref[rs, :], wg_ref[...])
        hu = _dot(xn_ref[rs, :], wu_ref[...])
        gate = _gate(_shift_conv3(prev_g, hg, fwg_ref), _shift_conv3(prev_u, hu, fwu_ref))
        o_ref[rs, :] += _dot(gate, wd_ref[...])
        prev_g = hg[sub - HIST_F:]
        prev_u = hu[sub - HIST_F:]
    carg[j] = prev_g
    caru[j] = prev_u
    tg_ref[...] = prev_g
    tu_ref[...] = prev_u

    @pl.when(j == pl.num_programs(1) - 1)
    def _():
        o_ref[...] = h_ref[...] + _rms(o_ref[...], gn_ref[...])


def ffn_prompt(xn, wg, wu, fwg, fwu, wd, h, gn, l):
    m, k = xn.shape
    tm, tn = TM_UP, TN_UP
    nj = FF_PAD // tn
    col = lambda r: pl.BlockSpec((r, tn), lambda i, j: (0, j))
    wcol = pl.BlockSpec((None, k, tn), lambda i, j: (l, 0, j))
    rows = lambda **kw: pl.BlockSpec((tm, D_MODEL), lambda i, j: (i, 0), **kw)
    tail = pl.BlockSpec((HIST_F, tn), lambda i, j: (i, j))
    return pl.pallas_call(
        functools.partial(_ffn_prompt_kernel, per_seq=SEQ // tm, sub=SUB_UP),
        grid=(m // tm, nj),
        in_specs=[pl.BlockSpec((tm, k), lambda i, j: (i, 0)), wcol, wcol, col(HIST_F), col(HIST_F),
                  pl.BlockSpec((None, tn, D_MODEL), lambda i, j: (l, j, 0)),
                  rows(pipeline_mode=pl.Buffered(1)),
                  pl.BlockSpec((1, D_MODEL), lambda i, j: (0, 0))],
        out_specs=[rows(), tail, tail],
        out_shape=[jax.ShapeDtypeStruct((m, D_MODEL), F32),
                   jax.ShapeDtypeStruct((m // tm * HIST_F, FF_PAD), F32),
                   jax.ShapeDtypeStruct((m // tm * HIST_F, FF_PAD), F32)],
        scratch_shapes=[pltpu.VMEM((nj, HIST_F, tn), F32), pltpu.VMEM((nj, HIST_F, tn), F32)],
        compiler_params=_params("arbitrary", "arbitrary"),
        name="ffn_prompt",
    )(xn, wg, wu, fwg, fwu, wd, h, gn)


def _ffn_sample_kernel(xn_ref, wg_ref, wu_ref, fwg_ref, fwu_ref, p0g_ref, p1g_ref, p0u_ref, p1u_ref,
                       wd_ref, h_ref, gn_ref, o_ref, tg_ref, tu_ref):
    nb = DEC_BATCH
    j = pl.program_id(0)

    @pl.when(j == 0)
    def _():
        o_ref[...] = jnp.zeros(o_ref.shape, F32)

    xn = xn_ref[...]
    hg = _dot(xn, wg_ref[...])
    hu = _dot(xn, wu_ref[...])
    tg_ref[...] = hg[(DEC_SEQ - 2) * nb:, :]
    tu_ref[...] = hu[(DEC_SEQ - 2) * nb:, :]
    ext_g = [p0g_ref[...], p1g_ref[...]] + [hg[t * nb:(t + 1) * nb] for t in range(DEC_SEQ)]
    ext_u = [p0u_ref[...], p1u_ref[...]] + [hu[t * nb:(t + 1) * nb] for t in range(DEC_SEQ)]
    for t in range(DEC_SEQ):
        cg = fwg_ref[0:1, :] * ext_g[t] + fwg_ref[1:2, :] * ext_g[t + 1] + fwg_ref[2:3, :] * ext_g[t + 2]
        cu = fwu_ref[0:1, :] * ext_u[t] + fwu_ref[1:2, :] * ext_u[t + 1] + fwu_ref[2:3, :] * ext_u[t + 2]
        ts = slice(t * nb, (t + 1) * nb)
        o_ref[ts, :] += _dot(_gate(cg, cu), wd_ref[...])

    @pl.when(j == pl.num_programs(0) - 1)
    def _():
        o_ref[...] = h_ref[...] + _rms(o_ref[...], gn_ref[...])


def ffn_sample(xn, wg, wu, fwg, fwu, p0g, p1g, p0u, p1u, wd, h, gn, l):
    m, k = xn.shape
    tn = TN_UP
    nj = FF_PAD // tn
    col = lambda r: pl.BlockSpec((r, tn), lambda j: (0, j))
    wcol = pl.BlockSpec((None, k, tn), lambda j: (l, 0, j))
    whole = lambda r, c: pl.BlockSpec((r, c), lambda j: (0, 0))
    nt = (FFN_CONV_WIDTH - 1) * DEC_BATCH
    return pl.pallas_call(
        _ffn_sample_kernel,
        grid=(nj,),
        in_specs=[whole(m, k), wcol, wcol, col(HIST_F), col(HIST_F),
                  col(DEC_BATCH), col(DEC_BATCH), col(DEC_BATCH), col(DEC_BATCH),
                  pl.BlockSpec((None, tn, D_MODEL), lambda j: (l, j, 0)), whole(m, D_MODEL), whole(1, D_MODEL)],
        out_specs=[whole(m, D_MODEL), col(nt), col(nt)],
        out_shape=[jax.ShapeDtypeStruct((m, D_MODEL), F32),
                   jax.ShapeDtypeStruct((nt, FF_PAD), F32),
                   jax.ShapeDtypeStruct((nt, FF_PAD), F32)],
        compiler_params=_params("arbitrary"),
        name="ffn_sample",
    )(xn, wg, wu, fwg, fwu, p0g, p1g, p0u, p1u, wd, h, gn)


def _pad_to(a, rows, cols):
    return jnp.pad(a, ((0, rows - a.shape[0]), (0, cols - a.shape[1])))


def _pad_rows(a, rows):
    return _pad_to(a, rows, a.shape[1])


def kernel(x_prompt, x_sample, cache_mem_k, cache_mem_v, state_conv_b, state_pool, state_sconv, state_ffn_conv, mem_prompt, g_mix_pre, g_mix_post, g_mem, g_x_pre, g_x_post, g_ffn_pre, g_ffn_post, w_in, w_out, a_norm_g, a_ws, a_bs, b_conv_w, b_conv_b, b_gn_g, b_gn_b, c_lin, c_scale, d_conv_w, w_xq, w_xk, w_xv, w_xo, w_up, f_conv_w, w_down):
    nb, ns = DEC_BATCH, DEC_SEQ
    w_in_b = cast_weights(w_in, 512)
    w_out_b = cast_weights(w_out, 1024)
    w_xq_b = cast_weights(w_xq, D_MODEL)
    w_kv_b = cast_weight_pair(w_xk, w_xv)
    w_xo_b = cast_weights(w_xo, D_X)
    w_g_b = cast_weights_pad_cols(w_up, 256, D_FF, 0, FF_PAD)
    w_u_b = cast_weights_pad_cols(w_up, 256, D_FF, 1, FF_PAD)
    w_down_b = cast_weights_pad_rows(w_down, TN_UP, FF_PAD)

    hp = x_prompt.reshape(BATCH * SEQ, D_MODEL)
    hs = jnp.transpose(x_sample, (1, 0, 2)).reshape(ns * nb, D_MODEL)
    mem = mem_prompt.reshape(BATCH * N_MEM, D_MODEL)
    row = lambda a: a.reshape(1, -1)
    st_b = state_conv_b.reshape(DEPTH, nb, -1)
    st_p = state_pool.reshape(DEPTH, nb, -1)
    st_s = state_sconv.reshape(DEPTH, nb, -1)

    outs = [[] for _ in range(11)]
    for l in range(DEPTH):
        ag, bb, gng, gnb, cs = (row(a[l]) for a in (a_norm_g, b_conv_b, b_gn_g, b_gn_b, c_scale))
        abt = jnp.transpose(a_bs[l])
        bw = _pad_rows(b_conv_w[l], HIST_B)
        dw = _pad_rows(d_conv_w[l], HIST_D)
        fwg = _pad_to(f_conv_w[l, :, :D_FF], HIST_F, FF_PAD)
        fwu = _pad_to(f_conv_w[l, :, D_FF:], HIST_F, FF_PAD)
        aw4 = jnp.repeat(jnp.transpose(a_ws[l, :, :ns, :ns], (1, 2, 0)).reshape(ns * ns, N_SUB), D_SUB, axis=1)
        ab4 = jnp.repeat(jnp.transpose(a_bs[l, :, :ns]), D_SUB, axis=1)
        gmp, gxp, gxo, gfp, gfo = (row(a[l]) for a in (g_mix_post, g_x_pre, g_x_post, g_ffn_pre, g_ffn_post))
        gpre = row(g_mix_pre[l])

        kv = norm_matmul(mem, row(g_mem[l]), w_kv_b, l, TM, D_X)
        mk = kv[:, :D_X].reshape(BATCH, N_MEM, D_X)
        mv = kv[:, D_X:].reshape(BATCH, N_MEM, D_X)

        z = norm_matmul(hp, gpre, w_in_b, l, TM_IN, TN_IN)
        y, nbp, npp, nsp = mixer_prompt(z.reshape(BATCH, SEQ, D_IN), ag, a_ws[l], abt, bw, bb, gng, gnb,
                                        c_lin[l], cs, dw)
        h2, xn = mid_prompt(y.reshape(BATCH * SEQ, D_MODEL), hp, w_out_b, gmp, gxp, w_xq_b, mk, mv,
                            w_xo_b, gxo, gfp, l)
        hp, tg, tu = ffn_prompt(xn, w_g_b, w_u_b, fwg, fwu, w_down_b, h2, gfo, l)
        per_seq = SEQ // TM_UP
        tails = jnp.concatenate([tg[:, :D_FF], tu[:, :D_FF]], axis=-1).reshape(BATCH, per_seq, HIST_F, 2 * D_FF)
        p_out = (mk, mv, nbp[:, HIST_B - (CONV_B_WIDTH - 1):], npp[:, HIST_C - POOL_PREV:],
                 nsp[:, HIST_D - (SCONV_WIDTH - 1):], tails[:, -1, HIST_F - (FFN_CONV_WIDTH - 1):])

        zs = norm_matmul(hs, gpre, w_in_b, l, TM, TN_IN)
        ys, nbs, nps, nss, vs = mixer_sample(zs.reshape(ns, nb, D_IN), st_b, st_p, st_s, ag, aw4, ab4, bw, bb,
                                             gng, gnb, c_lin[l], cs, dw, l)
        h1s, qs = mid_a(ys.reshape(ns * nb, D_MODEL), hs, w_out_b, gmp, gxp, w_xq_b, l)
        q8 = jnp.pad(jnp.transpose(qs.reshape(ns, nb, D_X), (1, 0, 2)), ((0, 0), (0, V7X_SUBLANES - ns), (0, 0)))
        o8 = attn_sample(q8, cache_mem_k, cache_mem_v, l)
        os_ = jnp.transpose(o8[:, :ns], (1, 0, 2)).reshape(ns * nb, D_X)
        h2s, xns = mid_c(os_, h1s, w_xo_b, gxo, gfp, l)
        sf = state_ffn_conv[l]
        prev = [_pad_to(p, nb, FF_PAD) for p in (sf[:, 0, :D_FF], sf[:, 1, :D_FF], sf[:, 0, D_FF:], sf[:, 1, D_FF:])]
        hs, tgs, tus = ffn_sample(xns, w_g_b, w_u_b, fwg, fwu, *prev, w_down_b, h2s, gfo, l)
        nffn_s = jnp.transpose(jnp.concatenate([tgs[:, :D_FF], tus[:, :D_FF]],
                                               axis=-1).reshape(FFN_CONV_WIDTH - 1, nb, 2 * D_FF),
                               (1, 0, 2))
        s_out = (nbs.reshape(nb, CONV_B_WIDTH - 1, D_GROUP), nps.reshape(nb, POOL_PREV, D_GROUP),
                 nss.reshape(nb, SCONV_WIDTH - 1, D_GROUP), nffn_s, vs.reshape(nb, ns, D_GROUP))
        for acc, val in zip(outs, p_out + s_out):
            acc.append(val)

    return (hp.reshape(BATCH, SEQ, D_MODEL),
            jnp.transpose(hs.reshape(ns, nb, D_MODEL), (1, 0, 2)),
            *(jnp.stack(o) for o in outs))
```

```python
import functools

import jax
import jax.numpy as jnp
from jax import lax
from jax.experimental import pallas as pl
from jax.experimental.pallas import tpu as pltpu

F32 = jnp.float32
BF16 = jnp.bfloat16

D_MODEL = 2048
BATCH = 4
SEQ = 2048
DEPTH = 4
DEC_BATCH = 128
DEC_SEQ = 4
PAST_LEN = 16384
D_GROUP = 512
N_SUB = 4
D_SUB = 128
D_IN = 8 * D_GROUP
CHUNK = 128
CONV_B_WIDTH = 31
POOL_WINDOWS = (2, 4, 8, 16)
POOL_PREV = 15
SCONV_WIDTH = 3
FFN_CONV_WIDTH = 3
D_FF = 5504
N_MEM = 256
N_XHEADS = 4
D_XHEAD = 128
D_X = 512
EPS = 1e-6

V7X_SUBLANES = 8
V7X_LANES = 128
V7X_VMEM_LIMIT_BYTES = 56 * 1024 * 1024

HIST_B = 32
HIST_C = 16
HIST_D = 8
HIST_F = 8

TM = 512
TM_UP = 1024
SUB_UP = 512
TM_IN = 1024
SUB_MID = 512
TT = 256
ROWS = 32
TN_IN = 1024
TN_UP = 512
FF_PAD = -(-D_FF // TN_UP) * TN_UP
TN_DOWN = 512
BB_MIX = 32
BB_ATT = 8


def _params(*sem):
    return pltpu.CompilerParams(dimension_semantics=sem, vmem_limit_bytes=V7X_VMEM_LIMIT_BYTES)


def _rms(x, g):
    return x * lax.rsqrt(jnp.mean(x * x, axis=-1, keepdims=True) + EPS) * g


def _sigmoid(x):
    return 1.0 / (1.0 + jnp.exp(-x))


def _dot(a, b):
    return jnp.dot(a, b, preferred_element_type=F32)


def _cast_kernel(x_ref, o_ref):
    o_ref[...] = x_ref[...].astype(BF16)


def cast_weights(w, rows, cols=None, col_block=0):
    depth, r, c = w.shape
    cols = c if cols is None else cols
    return pl.pallas_call(
        _cast_kernel,
        grid=(depth, r // rows),
        in_specs=[pl.BlockSpec((None, rows, cols), lambda l, i: (l, i, col_block))],
        out_specs=pl.BlockSpec((None, rows, cols), lambda l, i: (l, i, 0)),
        out_shape=jax.ShapeDtypeStruct((depth, r, cols), BF16),
        compiler_params=_params("parallel", "parallel"),
        name="cast_weights",
    )(w)


def _cast_pad_cols_kernel(x_ref, o_ref):
    n = x_ref.shape[-1]
    o_ref[:, 0:n] = x_ref[...].astype(BF16)
    o_ref[:, n:] = jnp.zeros((o_ref.shape[0], o_ref.shape[1] - n), BF16)


def cast_weights_pad_cols(w, rows, cols, col_block, padded):
    depth, r, _ = w.shape
    return pl.pallas_call(
        _cast_pad_cols_kernel,
        grid=(depth, r // rows),
        in_specs=[pl.BlockSpec((None, rows, cols), lambda l, i: (l, i, col_block))],
        out_specs=pl.BlockSpec((None, rows, padded), lambda l, i: (l, i, 0)),
        out_shape=jax.ShapeDtypeStruct((depth, r, padded), BF16),
        compiler_params=_params("parallel", "parallel"),
        name="cast_weights_pad_cols",
    )(w)


def _cast_pad_rows_kernel(x_ref, o_ref, *, valid_rows):
    rows = o_ref.shape[0]
    row = pl.program_id(1) * rows + lax.broadcasted_iota(jnp.int32, o_ref.shape, 0)
    o_ref[...] = jnp.where(row < valid_rows, x_ref[...], 0.0).astype(BF16)


def cast_weights_pad_rows(w, rows, padded):
    depth, r, c = w.shape
    return pl.pallas_call(
        functools.partial(_cast_pad_rows_kernel, valid_rows=r),
        grid=(depth, padded // rows),
        in_specs=[pl.BlockSpec((None, rows, c), lambda l, i: (l, i, 0))],
        out_specs=pl.BlockSpec((None, rows, c), lambda l, i: (l, i, 0)),
        out_shape=jax.ShapeDtypeStruct((depth, padded, c), BF16),
        compiler_params=_params("parallel", "parallel"),
        name="cast_weights_pad_rows",
    )(w)


def _cast_pair_kernel(a_ref, b_ref, o_ref):
    n = a_ref.shape[-1]
    o_ref[:, 0:n] = a_ref[...].astype(BF16)
    o_ref[:, n:2 * n] = b_ref[...].astype(BF16)


def cast_weight_pair(a, b):
    depth, r, c = a.shape
    spec = pl.BlockSpec((None, r, c), lambda l: (l, 0, 0))
    return pl.pallas_call(
        _cast_pair_kernel,
        grid=(depth,),
        in_specs=[spec, spec],
        out_specs=pl.BlockSpec((None, r, 2 * c), lambda l: (l, 0, 0)),
        out_shape=jax.ShapeDtypeStruct((depth, r, 2 * c), BF16),
        compiler_params=_params("parallel"),
        name="cast_weight_pair",
    )(a, b)


def _norm_matmul_kernel(x_ref, g_ref, w_ref, o_ref, xn_ref):
    @pl.when(pl.program_id(1) == 0)
    def _():
        xn_ref[...] = _rms(x_ref[...], g_ref[...]).astype(BF16)

    o_ref[...] = _dot(xn_ref[...], w_ref[...])


def norm_matmul(x, g, w, l, tm, tn):
    m, k = x.shape
    n = w.shape[2]
    return pl.pallas_call(
        _norm_matmul_kernel,
        grid=(m // tm, n // tn),
        in_specs=[pl.BlockSpec((tm, k), lambda i, j: (i, 0)),
                  pl.BlockSpec((1, k), lambda i, j: (0, 0)),
                  pl.BlockSpec((None, k, tn), lambda i, j: (l, 0, j))],
        out_specs=pl.BlockSpec((tm, tn), lambda i, j: (i, j)),
        out_shape=jax.ShapeDtypeStruct((m, n), F32),
        scratch_shapes=[pltpu.VMEM((tm, k), BF16)],
        compiler_params=_params("parallel", "arbitrary"),
        name="norm_matmul",
    )(x, g, w)


def _layer_norm(v, g):
    mu = jnp.mean(v, axis=-1, keepdims=True)
    d = v - mu
    var = jnp.mean(d * d, axis=-1, keepdims=True)
    return d * lax.rsqrt(var + EPS) * g


def _group_norm_silu(y, g, b):
    mu = jnp.mean(y, axis=-1, keepdims=True)
    d = y - mu
    var = jnp.mean(d * d, axis=-1, keepdims=True)
    yn = d * lax.rsqrt(var + EPS) * g + b
    return yn * _sigmoid(yn)


def _inmix_prompt_kernel(x_ref, gpre_ref, win_ref, ag_ref, aws_ref, abt_ref, bw_ref, bb_ref, gng_ref, gnb_ref,
                         clin_ref, cs_ref, dw_ref,
                         y_ref, nb_ref, np_ref, ns_ref,
                         extb, extc, extd, pooled):
    t = pl.program_id(1)
    tt = y_ref.shape[0]
    quarter = 2 * D_GROUP

    @pl.when(t == 0)
    def _():
        for s in range(V7X_SUBLANES):
            extb[s, 0:HIST_B, :] = jnp.zeros((HIST_B, D_GROUP), F32)
            extb[s, tt + HIST_B - V7X_SUBLANES:tt + HIST_B, :] = jnp.zeros((V7X_SUBLANES, D_GROUP), F32)
        extc[0:HIST_C, :] = jnp.zeros((HIST_C, D_GROUP), F32)
        extd[0:HIST_D, :] = jnp.zeros((HIST_D, D_GROUP), F32)

    xn = _rms(x_ref[...], gpre_ref[...]).astype(BF16)
    z = {}
    proj = lambda q: _dot(xn, win_ref[:, q * quarter:(q + 1) * quarter])

    def zcol(k, rs=slice(None)):
        return z[k // 2][rs, (k % 2) * D_GROUP:(k % 2 + 1) * D_GROUP]

    z[1] = proj(1)
    hb = zcol(2) * _sigmoid(zcol(3))
    for s in range(V7X_SUBLANES):
        extb[s, HIST_B - s:HIST_B - s + tt, :] = hb
    n_chunks = tt // ROWS
    later = {0: 0, n_chunks // 3: 2, 2 * n_chunks // 3: 3}
    for c in range(n_chunks):
        if c in later:
            z[later[c]] = proj(later[c])
        for g in range(N_SUB):
            ls = slice(g * D_SUB, (g + 1) * D_SUB)
            acc = jnp.zeros((ROWS, D_SUB), F32)
            for k in range(CONV_B_WIDTH):
                tiles, s = divmod(HIST_B - (CONV_B_WIDTH - 1) + k, V7X_SUBLANES)
                r0 = c * ROWS + tiles * V7X_SUBLANES
                acc = acc + bw_ref[k:k + 1, ls] * extb[s, r0:r0 + ROWS, ls]
            yb = _group_norm_silu(acc + bb_ref[:, ls], gng_ref[:, ls], gnb_ref[:, ls])
            y_ref[c * ROWS:(c + 1) * ROWS, D_GROUP + g * D_SUB:D_GROUP + (g + 1) * D_SUB] = yb.astype(BF16)

    vn = _layer_norm(zcol(1), ag_ref[...]).astype(BF16)
    row = lax.broadcasted_iota(jnp.int32, (CHUNK, CHUNK), 0)
    col = lax.broadcasted_iota(jnp.int32, (CHUNK, CHUNK), 1)
    for h in range(N_SUB):
        wm = jnp.where(row >= col, aws_ref[h], 0.0).astype(BF16)
        bias = abt_ref[:, h:h + 1]
        for c in range(tt // CHUNK):
            rs = slice(c * CHUNK, (c + 1) * CHUNK)
            ls = slice(h * D_SUB, (h + 1) * D_SUB)
            zz = _dot(wm, vn[rs, ls]) + bias
            y_ref[rs, ls] = (z[0][rs, ls] * zz).astype(BF16)

    extc[HIST_C:HIST_C + tt, :] = zcol(4)
    for c in range(tt // ROWS):
        pos = t * tt + c * ROWS + lax.broadcasted_iota(jnp.int32, (ROWS, 1), 0)
        for g, win in enumerate(POOL_WINDOWS):
            ls = slice(g * D_SUB, (g + 1) * D_SUB)
            r0 = c * ROWS + HIST_C
            x = extc[r0:r0 + ROWS, ls]
            s = x
            for i in range(1, win):
                s = s + extc[r0 - i:r0 - i + ROWS, ls]
            cnt = jnp.minimum(pos + 1, win).astype(F32)
            pooled[c * ROWS:(c + 1) * ROWS, ls] = (s / cnt - x).astype(BF16)
    for g in range(N_SUB):
        ls = slice(g * D_SUB, (g + 1) * D_SUB)
        yc = _dot(pooled[:, ls], clin_ref[g].astype(BF16)) * cs_ref[:, ls]
        y_ref[:, 2 * D_GROUP + g * D_SUB:2 * D_GROUP + (g + 1) * D_SUB] = yc.astype(BF16)

    extd[HIST_D:HIST_D + tt, :] = zcol(7) * zcol(5)
    for c in range(tt // ROWS):
        r0 = c * ROWS + HIST_D
        conv = (dw_ref[0:1, :] * extd[r0 - 2:r0 - 2 + ROWS, :]
                + dw_ref[1:2, :] * extd[r0 - 1:r0 - 1 + ROWS, :]
                + dw_ref[2:3, :] * extd[r0:r0 + ROWS, :])
        rs = slice(c * ROWS, (c + 1) * ROWS)
        y_ref[rs, 3 * D_GROUP:4 * D_GROUP] = (zcol(6, rs) * conv).astype(BF16)

    for s in range(V7X_SUBLANES):
        extb[s, 0:HIST_B, :] = extb[s, tt:tt + HIST_B, :]
    extc[0:HIST_C, :] = extc[tt:tt + HIST_C, :]
    extd[0:HIST_D, :] = extd[tt:tt + HIST_D, :]

    @pl.when(t == pl.num_programs(1) - 1)
    def _():
        nb_ref[...] = extb[0, 0:HIST_B, :]
        np_ref[...] = extc[0:HIST_C, :]
        ns_ref[...] = extd[0:HIST_D, :]


def inmix_prompt(x, gpre, win, ag, aws, abt, bw, bb, gng, gnb, clin, cs, dw, l):
    tt = TT
    full = lambda a: pl.BlockSpec(a.shape, lambda b, t: (0,) * a.ndim)
    hist = lambda r: pl.BlockSpec((None, r, D_GROUP), lambda b, t: (b, 0, 0))
    return pl.pallas_call(
        _inmix_prompt_kernel,
        grid=(BATCH, SEQ // tt),
        in_specs=[pl.BlockSpec((None, tt, D_MODEL), lambda b, t: (b, t, 0)),
                  full(gpre),
                  pl.BlockSpec((None, D_MODEL, D_IN), lambda b, t: (l, 0, 0), pipeline_mode=pl.Buffered(1)),
                  full(ag), full(aws), full(abt), full(bw), full(bb), full(gng), full(gnb),
                  full(clin), full(cs), full(dw)],
        out_specs=[pl.BlockSpec((None, tt, D_MODEL), lambda b, t: (b, t, 0)),
                   hist(HIST_B), hist(HIST_C), hist(HIST_D)],
        out_shape=[jax.ShapeDtypeStruct((BATCH, SEQ, D_MODEL), BF16),
                   jax.ShapeDtypeStruct((BATCH, HIST_B, D_GROUP), F32),
                   jax.ShapeDtypeStruct((BATCH, HIST_C, D_GROUP), F32),
                   jax.ShapeDtypeStruct((BATCH, HIST_D, D_GROUP), F32)],
        scratch_shapes=[pltpu.VMEM((V7X_SUBLANES, HIST_B + tt, D_GROUP), F32),
                        pltpu.VMEM((HIST_C + tt, D_GROUP), F32),
                        pltpu.VMEM((HIST_D + tt, D_GROUP), F32),
                        pltpu.VMEM((tt, D_GROUP), BF16)],
        compiler_params=_params("parallel", "arbitrary"),
        name="inmix_prompt",
    )(x, gpre, win, ag, aws, abt, bw, bb, gng, gnb, clin, cs, dw)


def _mixer_sample_kernel(z_ref, cb_ref, cp_ref, csc_ref, ag_ref, aw4_ref, ab4_ref, bw_ref, bb_ref,
                         gng_ref, gnb_ref, clin_ref, cs_ref, dw_ref,
                         y_ref, nb_ref, np_ref, nsc_ref, v_ref, pooled):
    bblk = z_ref.shape[1]
    blk = lambda r: slice(r * D_GROUP, (r + 1) * D_GROUP)

    vn = [_layer_norm(z_ref[t, :, D_GROUP:2 * D_GROUP], ag_ref[...]) for t in range(DEC_SEQ)]
    for i in range(DEC_SEQ):
        v_ref[:, blk(i)] = vn[i]
        zz = ab4_ref[i:i + 1, :]
        for j in range(i + 1):
            zz = zz + aw4_ref[i * DEC_SEQ + j:i * DEC_SEQ + j + 1, :] * vn[j]
        y_ref[i, :, 0:D_GROUP] = (z_ref[i, :, 0:D_GROUP] * zz).astype(BF16)

    nprev = CONV_B_WIDTH - 1
    hb = [z_ref[t, :, 2 * D_GROUP:3 * D_GROUP] * _sigmoid(z_ref[t, :, 3 * D_GROUP:4 * D_GROUP])
          for t in range(DEC_SEQ)]
    ext_b = lambda r: cb_ref[:, blk(r)] if r < nprev else hb[r - nprev]
    for t in range(DEC_SEQ):
        acc = jnp.zeros((bblk, D_GROUP), F32)
        for k in range(CONV_B_WIDTH):
            acc = acc + bw_ref[k:k + 1, :] * ext_b(t + k)
        acc = acc + bb_ref[...]
        for g in range(N_SUB):
            ls = slice(g * D_SUB, (g + 1) * D_SUB)
            yb = _group_norm_silu(acc[:, ls], gng_ref[:, ls], gnb_ref[:, ls])
            y_ref[t, :, D_GROUP + g * D_SUB:D_GROUP + (g + 1) * D_SUB] = yb.astype(BF16)
    for r in range(nprev):
        nb_ref[:, blk(r)] = ext_b(r + DEC_SEQ)

    cx = [z_ref[t, :, 4 * D_GROUP:5 * D_GROUP] for t in range(DEC_SEQ)]
    ext_c = lambda r: cp_ref[:, blk(r)] if r < POOL_PREV else cx[r - POOL_PREV]
    for t in range(DEC_SEQ):
        for g, win in enumerate(POOL_WINDOWS):
            ls = slice(g * D_SUB, (g + 1) * D_SUB)
            s = cx[t][:, ls]
            for i in range(1, win):
                s = s + ext_c(POOL_PREV + t - i)[:, ls]
            cnt = float(min(PAST_LEN + t + 1, win))
            pooled[t * bblk:(t + 1) * bblk, ls] = (s / cnt - cx[t][:, ls]).astype(BF16)
    for g in range(N_SUB):
        ls = slice(g * D_SUB, (g + 1) * D_SUB)
        yc = _dot(pooled[:, ls], clin_ref[g].astype(BF16)) * cs_ref[:, ls]
        for t in range(DEC_SEQ):
            y_ref[t, :, 2 * D_GROUP + g * D_SUB:2 * D_GROUP + (g + 1) * D_SUB] = (
                yc[t * bblk:(t + 1) * bblk].astype(BF16))
    for r in range(POOL_PREV):
        np_ref[:, blk(r)] = ext_c(r + DEC_SEQ)

    nsp = SCONV_WIDTH - 1
    hd = [z_ref[t, :, 7 * D_GROUP:8 * D_GROUP] * z_ref[t, :, 5 * D_GROUP:6 * D_GROUP] for t in range(DEC_SEQ)]
    ext_d = lambda r: csc_ref[:, blk(r)] if r < nsp else hd[r - nsp]
    for t in range(DEC_SEQ):
        conv = dw_ref[0:1, :] * ext_d(t) + dw_ref[1:2, :] * ext_d(t + 1) + dw_ref[2:3, :] * ext_d(t + 2)
        y_ref[t, :, 3 * D_GROUP:4 * D_GROUP] = (z_ref[t, :, 6 * D_GROUP:7 * D_GROUP] * conv).astype(BF16)
    for r in range(nsp):
        nsc_ref[:, blk(r)] = ext_d(r + DEC_SEQ)


def mixer_sample(z, cb, cp, csc, ag, aw4, ab4, bw, bb, gng, gnb, clin, cs, dw, l):
    bblk = BB_MIX
    full = lambda a: pl.BlockSpec(a.shape, lambda b: (0,) * a.ndim)
    st_in = lambda a: pl.BlockSpec((None, bblk, a.shape[2]), lambda b: (l, b, 0))
    st = lambda a: pl.BlockSpec((bblk, a.shape[2]), lambda b: (b, 0))
    nv = DEC_SEQ * D_GROUP
    return pl.pallas_call(
        _mixer_sample_kernel,
        grid=(DEC_BATCH // bblk,),
        in_specs=[pl.BlockSpec((DEC_SEQ, bblk, D_IN), lambda b: (0, b, 0)),
                  st_in(cb), st_in(cp), st_in(csc),
                  full(ag), full(aw4), full(ab4), full(bw), full(bb), full(gng), full(gnb),
                  full(clin), full(cs), full(dw)],
        out_specs=[pl.BlockSpec((DEC_SEQ, bblk, D_MODEL), lambda b: (0, b, 0)),
                   st(cb), st(cp), st(csc),
                   pl.BlockSpec((bblk, nv), lambda b: (b, 0))],
        out_shape=[jax.ShapeDtypeStruct((DEC_SEQ, DEC_BATCH, D_MODEL), BF16),
                   jax.ShapeDtypeStruct(cb.shape[1:], F32),
                   jax.ShapeDtypeStruct(cp.shape[1:], F32),
                   jax.ShapeDtypeStruct(csc.shape[1:], F32),
                   jax.ShapeDtypeStruct((DEC_BATCH, nv), F32)],
        scratch_shapes=[pltpu.VMEM((DEC_SEQ * bblk, D_GROUP), BF16)],
        compiler_params=_params("parallel"),
        name="mixer_sample",
    )(z, cb, cp, csc, ag, aw4, ab4, bw, bb, gng, gnb, clin, cs, dw)


def _stage_a(y, h, wout_ref, gpost_ref, gpre_ref, wxq_ref):
    h1 = h + _rms(_dot(y, wout_ref[...]), gpost_ref[...])
    q = _dot(_rms(h1, gpre_ref[...]).astype(BF16), wxq_ref[...])
    return h1, q


def _stage_c(o, h1, wxo_ref, gpost_ref, gffn_ref):
    h2 = h1 + _rms(_dot(o, wxo_ref[...]), gpost_ref[...])
    return h2, _rms(h2, gffn_ref[...]).astype(BF16)


def _softmax_rows(s):
    e = jnp.exp(s - jnp.max(s, axis=-1, keepdims=True))
    return e / jnp.sum(e, axis=-1, keepdims=True)


_NT = (((1,), (1,)), ((), ()))


def _mid_prompt_kernel(y_ref, h_ref, wout_ref, gmp_ref, gxp_ref, wxq_ref, mk_ref, mv_ref, wxo_ref,
                       gxo_ref, gffn_ref, h2_ref, xn_ref):
    k = mk_ref[...].astype(BF16)
    v = mv_ref[...].astype(BF16)
    for r in range(y_ref.shape[0] // SUB_MID):
        rs = slice(r * SUB_MID, (r + 1) * SUB_MID)
        h1, q = _stage_a(y_ref[rs, :], h_ref[rs, :], wout_ref, gmp_ref, gxp_ref, wxq_ref)
        q = q.astype(BF16)
        heads = []
        for hd in range(N_XHEADS):
            ls = slice(hd * D_XHEAD, (hd + 1) * D_XHEAD)
            s = lax.dot_general(q[:, ls], k[:, ls], _NT, preferred_element_type=F32) * (D_XHEAD ** -0.5)
            heads.append(_dot(_softmax_rows(s).astype(BF16), v[:, ls]))
        o = jnp.concatenate(heads, axis=-1).astype(BF16)
        h2, xn = _stage_c(o, h1, wxo_ref, gxo_ref, gffn_ref)
        h2_ref[rs, :] = h2
        xn_ref[rs, :] = xn


def _layer_weight(w, l):
    return pl.BlockSpec((None,) + w.shape[1:], lambda i: (l, 0, 0), pipeline_mode=pl.Buffered(1))


def mid_prompt(y, h, wout, gmp, gxp, wxq, mk, mv, wxo, gxo, gffn, l):
    m = y.shape[0]
    tm = TM
    per_seq = SEQ // tm
    rows = lambda c: pl.BlockSpec((tm, c), lambda i: (i, 0))
    full = lambda a: pl.BlockSpec(a.shape, lambda i: (0,) * a.ndim, pipeline_mode=pl.Buffered(1))
    mem = pl.BlockSpec((None, N_MEM, D_X), lambda i: (i // per_seq, 0, 0))
    return pl.pallas_call(
        _mid_prompt_kernel,
        grid=(m // tm,),
        in_specs=[rows(D_MODEL), rows(D_MODEL), _layer_weight(wout, l), full(gmp), full(gxp),
                  _layer_weight(wxq, l), mem, mem, _layer_weight(wxo, l), full(gxo), full(gffn)],
        out_specs=[rows(D_MODEL), rows(D_MODEL)],
        out_shape=[jax.ShapeDtypeStruct((m, D_MODEL), F32), jax.ShapeDtypeStruct((m, D_MODEL), BF16)],
        compiler_params=_params("parallel"),
        name="mid_prompt",
    )(y, h, wout, gmp, gxp, wxq, mk, mv, wxo, gxo, gffn)


def _mid_a_kernel(y_ref, h_ref, wout_ref, gmp_ref, gxp_ref, wxq_ref, h1_ref, q_ref):
    h1, q = _stage_a(y_ref[...], h_ref[...], wout_ref, gmp_ref, gxp_ref, wxq_ref)
    h1_ref[...] = h1
    q_ref[...] = q.astype(BF16)


def mid_a(y, h, wout, gmp, gxp, wxq, l):
    m = y.shape[0]
    tm = TM
    rows = lambda c: pl.BlockSpec((tm, c), lambda i: (i, 0))
    full = lambda a: pl.BlockSpec(a.shape, lambda i: (0,) * a.ndim, pipeline_mode=pl.Buffered(1))
    return pl.pallas_call(
        _mid_a_kernel,
        grid=(m // tm,),
        in_specs=[rows(D_MODEL), rows(D_MODEL), _layer_weight(wout, l), full(gmp), full(gxp),
                  _layer_weight(wxq, l)],
        out_specs=[rows(D_MODEL), rows(D_X)],
        out_shape=[jax.ShapeDtypeStruct((m, D_MODEL), F32), jax.ShapeDtypeStruct((m, D_X), BF16)],
        compiler_params=_params("parallel"),
        name="mid_a",
    )(y, h, wout, gmp, gxp, wxq)


def _mid_c_kernel(o_ref, h1_ref, wxo_ref, gxo_ref, gffn_ref, h2_ref, xn_ref):
    h2, xn = _stage_c(o_ref[...], h1_ref[...], wxo_ref, gxo_ref, gffn_ref)
    h2_ref[...] = h2
    xn_ref[...] = xn


def mid_c(o, h1, wxo, gxo, gffn, l):
    m = o.shape[0]
    tm = TM
    rows = lambda c: pl.BlockSpec((tm, c), lambda i: (i, 0))
    full = lambda a: pl.BlockSpec(a.shape, lambda i: (0,) * a.ndim, pipeline_mode=pl.Buffered(1))
    return pl.pallas_call(
        _mid_c_kernel,
        grid=(m // tm,),
        in_specs=[rows(D_X), rows(D_MODEL), _layer_weight(wxo, l), full(gxo), full(gffn)],
        out_specs=[rows(D_MODEL), rows(D_MODEL)],
        out_shape=[jax.ShapeDtypeStruct((m, D_MODEL), F32), jax.ShapeDtypeStruct((m, D_MODEL), BF16)],
        compiler_params=_params("parallel"),
        name="mid_c",
    )(o, h1, wxo, gxo, gffn)


def _attn_sample_kernel(q_ref, k_ref, v_ref, o_ref):
    bblk = q_ref.shape[0]
    lane_head = lax.broadcasted_iota(jnp.int32, (V7X_SUBLANES, D_X), 1) // D_XHEAD
    for b in range(bblk):
        q8 = q_ref[b]
        qbd = jnp.concatenate([jnp.where(lane_head == hd, q8, jnp.zeros_like(q8)) for hd in range(N_XHEADS)], axis=0)
        kb = k_ref[b].astype(BF16)
        vb = v_ref[b].astype(BF16)
        s = lax.dot_general(qbd, kb, _NT, preferred_element_type=F32) * (D_XHEAD ** -0.5)
        of = _dot(_softmax_rows(s).astype(BF16), vb)
        o8 = jnp.zeros((V7X_SUBLANES, D_X), F32)
        for hd in range(N_XHEADS):
            o8 = o8 + jnp.where(lane_head == hd, of[hd * V7X_SUBLANES:(hd + 1) * V7X_SUBLANES], 0.0)
        o_ref[b] = o8.astype(BF16)


def attn_sample(q, k, v, l):
    bblk = BB_ATT
    qs = pl.BlockSpec((bblk, V7X_SUBLANES, D_X), lambda b: (b, 0, 0))
    ms = pl.BlockSpec((None, bblk, N_MEM, D_X), lambda b: (l, b, 0, 0))
    return pl.pallas_call(
        _attn_sample_kernel,
        grid=(DEC_BATCH // bblk,),
        in_specs=[qs, ms, ms],
        out_specs=qs,
        out_shape=jax.ShapeDtypeStruct(q.shape, BF16),
        compiler_params=_params("parallel"),
        name="attn_sample",
    )(q, k, v)


def _gate(cg, cu):
    return (cg * _sigmoid(cg) * cu).astype(BF16)


def _shift_conv3(prev, h, fw_ref):
    rows = h.shape[0]
    ext = jnp.concatenate([prev, h], axis=0)
    return (fw_ref[0:1, :] * ext[HIST_F - 2:HIST_F - 2 + rows]
            + fw_ref[1:2, :] * ext[HIST_F - 1:HIST_F - 1 + rows]
            + fw_ref[2:3, :] * h)


def _up_prompt_kernel(xn_ref, wg_ref, wu_ref, fwg_ref, fwu_ref, o_ref, tg_ref, tu_ref,
                      carg, caru, *, per_seq, sub):
    i = pl.program_id(0)
    j = pl.program_id(1)
    tm = xn_ref.shape[0]

    @pl.when(i % per_seq == 0)
    def _():
        carg[j] = jnp.zeros(carg.shape[1:], F32)
        caru[j] = jnp.zeros(caru.shape[1:], F32)

    prev_g = carg[j]
    prev_u = caru[j]
    for r in range(tm // sub):
        rs = slice(r * sub, (r + 1) * sub)
        hg = _dot(xn_ref[rs, :], wg_ref[...])
        hu = _dot(xn_ref[rs, :], wu_ref[...])
        o_ref[rs, :] = _gate(_shift_conv3(prev_g, hg, fwg_ref), _shift_conv3(prev_u, hu, fwu_ref))
        prev_g = hg[sub - HIST_F:]
        prev_u = hu[sub - HIST_F:]
    carg[j] = prev_g
    caru[j] = prev_u
    tg_ref[...] = prev_g
    tu_ref[...] = prev_u


def up_prompt(xn, wg, wu, fwg, fwu, l):
    m, k = xn.shape
    tm, tn = TM_UP, TN_UP
    nj = FF_PAD // tn
    col = lambda r: pl.BlockSpec((r, tn), lambda i, j: (0, j))
    wcol = pl.BlockSpec((None, k, tn), lambda i, j: (l, 0, j))
    tail = pl.BlockSpec((HIST_F, tn), lambda i, j: (i, j))
    return pl.pallas_call(
        functools.partial(_up_prompt_kernel, per_seq=SEQ // tm, sub=SUB_UP),
        grid=(m // tm, nj),
        in_specs=[pl.BlockSpec((tm, k), lambda i, j: (i, 0)), wcol, wcol, col(HIST_F), col(HIST_F)],
        out_specs=[pl.BlockSpec((tm, tn), lambda i, j: (i, j)), tail, tail],
        out_shape=[jax.ShapeDtypeStruct((m, FF_PAD), BF16),
                   jax.ShapeDtypeStruct((m // tm * HIST_F, FF_PAD), F32),
                   jax.ShapeDtypeStruct((m // tm * HIST_F, FF_PAD), F32)],
        scratch_shapes=[pltpu.VMEM((nj, HIST_F, tn), F32), pltpu.VMEM((nj, HIST_F, tn), F32)],
        compiler_params=_params("arbitrary", "arbitrary"),
        name="up_prompt",
    )(xn, wg, wu, fwg, fwu)


def _down_kernel(g_ref, w_ref, h_ref, gn_ref, o_ref, acc):
    j = pl.program_id(1)
    tn = w_ref.shape[1]
    acc[:, pl.ds(pl.multiple_of(j * tn, tn), tn)] = _dot(g_ref[...], w_ref[...])

    @pl.when(j == pl.num_programs(1) - 1)
    def _():
        o_ref[...] = h_ref[...] + _rms(acc[...], gn_ref[...])


def down(g, w, h, gn, l):
    m, k = g.shape
    tm, tn = TM, TN_DOWN
    return pl.pallas_call(
        _down_kernel,
        grid=(m // tm, D_MODEL // tn),
        in_specs=[pl.BlockSpec((tm, k), lambda i, j: (i, 0)),
                  pl.BlockSpec((None, k, tn), lambda i, j: (l, 0, j)),
                  pl.BlockSpec((tm, D_MODEL), lambda i, j: (i, 0)),
                  pl.BlockSpec((1, D_MODEL), lambda i, j: (0, 0))],
        out_specs=pl.BlockSpec((tm, D_MODEL), lambda i, j: (i, 0)),
        out_shape=jax.ShapeDtypeStruct((m, D_MODEL), F32),
        scratch_shapes=[pltpu.VMEM((tm, D_MODEL), F32)],
        compiler_params=_params("parallel", "arbitrary"),
        name="down",
    )(g, w, h, gn)


def _ffn_sample_kernel(xn_ref, wg_ref, wu_ref, fwg_ref, fwu_ref, p0g_ref, p1g_ref, p0u_ref, p1u_ref,
                       wd_ref, h_ref, gn_ref, o_ref, tg_ref, tu_ref):
    nb = DEC_BATCH
    j = pl.program_id(0)

    @pl.when(j == 0)
    def _():
        o_ref[...] = jnp.zeros(o_ref.shape, F32)

    xn = xn_ref[...]
    hg = _dot(xn, wg_ref[...])
    hu = _dot(xn, wu_ref[...])
    tg_ref[...] = hg[(DEC_SEQ - 2) * nb:, :]
    tu_ref[...] = hu[(DEC_SEQ - 2) * nb:, :]
    ext_g = [p0g_ref[...], p1g_ref[...]] + [hg[t * nb:(t + 1) * nb] for t in range(DEC_SEQ)]
    ext_u = [p0u_ref[...], p1u_ref[...]] + [hu[t * nb:(t + 1) * nb] for t in range(DEC_SEQ)]
    for t in range(DEC_SEQ):
        cg = fwg_ref[0:1, :] * ext_g[t] + fwg_ref[1:2, :] * ext_g[t + 1] + fwg_ref[2:3, :] * ext_g[t + 2]
        cu = fwu_ref[0:1, :] * ext_u[t] + fwu_ref[1:2, :] * ext_u[t + 1] + fwu_ref[2:3, :] * ext_u[t + 2]
        ts = slice(t * nb, (t + 1) * nb)
        o_ref[ts, :] += _dot(_gate(cg, cu), wd_ref[...])

    @pl.when(j == pl.num_programs(0) - 1)
    def _():
        o_ref[...] = h_ref[...] + _rms(o_ref[...], gn_ref[...])


def ffn_sample(xn, wg, wu, fwg, fwu, p0g, p1g, p0u, p1u, wd, h, gn, l):
    m, k = xn.shape
    tn = TN_UP
    nj = FF_PAD // tn
    col = lambda r: pl.BlockSpec((r, tn), lambda j: (0, j))
    wcol = pl.BlockSpec((None, k, tn), lambda j: (l, 0, j))
    whole = lambda r, c: pl.BlockSpec((r, c), lambda j: (0, 0))
    nt = (FFN_CONV_WIDTH - 1) * DEC_BATCH
    return pl.pallas_call(
        _ffn_sample_kernel,
        grid=(nj,),
        in_specs=[whole(m, k), wcol, wcol, col(HIST_F), col(HIST_F),
                  col(DEC_BATCH), col(DEC_BATCH), col(DEC_BATCH), col(DEC_BATCH),
                  pl.BlockSpec((None, tn, D_MODEL), lambda j: (l, j, 0)), whole(m, D_MODEL), whole(1, D_MODEL)],
        out_specs=[whole(m, D_MODEL), col(nt), col(nt)],
        out_shape=[jax.ShapeDtypeStruct((m, D_MODEL), F32),
                   jax.ShapeDtypeStruct((nt, FF_PAD), F32),
                   jax.ShapeDtypeStruct((nt, FF_PAD), F32)],
        compiler_params=_params("arbitrary"),
        name="ffn_sample",
    )(xn, wg, wu, fwg, fwu, p0g, p1g, p0u, p1u, wd, h, gn)


def _pad_to(a, rows, cols):
    return jnp.pad(a, ((0, rows - a.shape[0]), (0, cols - a.shape[1])))


def _pad_rows(a, rows):
    return _pad_to(a, rows, a.shape[1])


def kernel(x_prompt, x_sample, cache_mem_k, cache_mem_v, state_conv_b, state_pool, state_sconv, state_ffn_conv, mem_prompt, g_mix_pre, g_mix_post, g_mem, g_x_pre, g_x_post, g_ffn_pre, g_ffn_post, w_in, w_out, a_norm_g, a_ws, a_bs, b_conv_w, b_conv_b, b_gn_g, b_gn_b, c_lin, c_scale, d_conv_w, w_xq, w_xk, w_xv, w_xo, w_up, f_conv_w, w_down):
    nb, ns = DEC_BATCH, DEC_SEQ
    w_in_b = cast_weights(w_in, 512)
    w_out_b = cast_weights(w_out, 1024)
    w_xq_b = cast_weights(w_xq, D_MODEL)
    w_kv_b = cast_weight_pair(w_xk, w_xv)
    w_xo_b = cast_weights(w_xo, D_X)
    w_g_b = cast_weights_pad_cols(w_up, 256, D_FF, 0, FF_PAD)
    w_u_b = cast_weights_pad_cols(w_up, 256, D_FF, 1, FF_PAD)
    w_down_b = cast_weights_pad_rows(w_down, TN_UP, FF_PAD)

    hp = x_prompt.reshape(BATCH * SEQ, D_MODEL)
    hs = jnp.transpose(x_sample, (1, 0, 2)).reshape(ns * nb, D_MODEL)
    mem = mem_prompt.reshape(BATCH * N_MEM, D_MODEL)
    row = lambda a: a.reshape(1, -1)
    st_b = state_conv_b.reshape(DEPTH, nb, -1)
    st_p = state_pool.reshape(DEPTH, nb, -1)
    st_s = state_sconv.reshape(DEPTH, nb, -1)

    outs = [[] for _ in range(11)]
    for l in range(DEPTH):
        ag, bb, gng, gnb, cs = (row(a[l]) for a in (a_norm_g, b_conv_b, b_gn_g, b_gn_b, c_scale))
        abt = jnp.transpose(a_bs[l])
        bw = _pad_rows(b_conv_w[l], HIST_B)
        dw = _pad_rows(d_conv_w[l], HIST_D)
        fwg = _pad_to(f_conv_w[l, :, :D_FF], HIST_F, FF_PAD)
        fwu = _pad_to(f_conv_w[l, :, D_FF:], HIST_F, FF_PAD)
        aw4 = jnp.repeat(jnp.transpose(a_ws[l, :, :ns, :ns], (1, 2, 0)).reshape(ns * ns, N_SUB), D_SUB, axis=1)
        ab4 = jnp.repeat(jnp.transpose(a_bs[l, :, :ns]), D_SUB, axis=1)
        gmp, gxp, gxo, gfp, gfo = (row(a[l]) for a in (g_mix_post, g_x_pre, g_x_post, g_ffn_pre, g_ffn_post))
        gpre = row(g_mix_pre[l])

        kv = norm_matmul(mem, row(g_mem[l]), w_kv_b, l, TM, D_X)
        mk = kv[:, :D_X].reshape(BATCH, N_MEM, D_X)
        mv = kv[:, D_X:].reshape(BATCH, N_MEM, D_X)

        y, nbp, npp, nsp = inmix_prompt(hp.reshape(BATCH, SEQ, D_MODEL), gpre, w_in_b, ag, a_ws[l], abt, bw, bb,
                                        gng, gnb, c_lin[l], cs, dw, l)
        h2, xn = mid_prompt(y.reshape(BATCH * SEQ, D_MODEL), hp, w_out_b, gmp, gxp, w_xq_b, mk, mv,
                            w_xo_b, gxo, gfp, l)
        gate, tg, tu = up_prompt(xn, w_g_b, w_u_b, fwg, fwu, l)
        hp = down(gate, w_down_b, h2, gfo, l)
        per_seq = SEQ // TM_UP
        tails = jnp.concatenate([tg[:, :D_FF], tu[:, :D_FF]], axis=-1).reshape(BATCH, per_seq, HIST_F, 2 * D_FF)
        p_out = (mk, mv, nbp[:, HIST_B - (CONV_B_WIDTH - 1):], npp[:, HIST_C - POOL_PREV:],
                 nsp[:, HIST_D - (SCONV_WIDTH - 1):], tails[:, -1, HIST_F - (FFN_CONV_WIDTH - 1):])

        zs = norm_matmul(hs, gpre, w_in_b, l, TM, TN_IN)
        ys, nbs, nps, nss, vs = mixer_sample(zs.reshape(ns, nb, D_IN), st_b, st_p, st_s, ag, aw4, ab4, bw, bb,
                                             gng, gnb, c_lin[l], cs, dw, l)
        h1s, qs = mid_a(ys.reshape(ns * nb, D_MODEL), hs, w_out_b, gmp, gxp, w_xq_b, l)
        q8 = jnp.pad(jnp.transpose(qs.reshape(ns, nb, D_X), (1, 0, 2)), ((0, 0), (0, V7X_SUBLANES - ns), (0, 0)))
        o8 = attn_sample(q8, cache_mem_k, cache_mem_v, l)
        os_ = jnp.transpose(o8[:, :ns], (1, 0, 2)).reshape(ns * nb, D_X)
        h2s, xns = mid_c(os_, h1s, w_xo_b, gxo, gfp, l)
        sf = state_ffn_conv[l]
        prev = [_pad_to(p, nb, FF_PAD) for p in (sf[:, 0, :D_FF], sf[:, 1, :D_FF], sf[:, 0, D_FF:], sf[:, 1, D_FF:])]
        hs, tgs, tus = ffn_sample(xns, w_g_b, w_u_b, fwg, fwu, *prev, w_down_b, h2s, gfo, l)
        nffn_s = jnp.transpose(jnp.concatenate([tgs[:, :D_FF], tus[:, :D_FF]],
                                               axis=-1).reshape(FFN_CONV_WIDTH - 1, nb, 2 * D_FF),
                               (1, 0, 2))
        s_out = (nbs.reshape(nb, CONV_B_WIDTH - 1, D_GROUP), nps.reshape(nb, POOL_PREV, D_GROUP),
                 nss.reshape(nb, SCONV_WIDTH - 1, D_GROUP), nffn_s, vs.reshape(nb, ns, D_GROUP))
        for acc, val in zip(outs, p_out + s_out):
            acc.append(val)

    return (hp.reshape(BATCH, SEQ, D_MODEL),
            jnp.transpose(hs.reshape(ns, nb, D_MODEL), (1, 0, 2)),
            *(jnp.stack(o) for o in outs))
```

```python
import functools

import jax
import jax.numpy as jnp
from jax import lax
from jax.experimental import pallas as pl
from jax.experimental.pallas import tpu as pltpu

F32 = jnp.float32
BF16 = jnp.bfloat16

D_MODEL = 2048
BATCH = 4
SEQ = 2048
DEPTH = 4
DEC_BATCH = 128
DEC_SEQ = 4
PAST_LEN = 16384
D_GROUP = 512
N_SUB = 4
D_SUB = 128
D_IN = 8 * D_GROUP
CHUNK = 128
CONV_B_WIDTH = 31
POOL_WINDOWS = (2, 4, 8, 16)
POOL_PREV = 15
SCONV_WIDTH = 3
FFN_CONV_WIDTH = 3
D_FF = 5504
N_MEM = 256
N_XHEADS = 4
D_XHEAD = 128
D_X = 512
EPS = 1e-6

V7X_SUBLANES = 8
V7X_LANES = 128
V7X_VMEM_LIMIT_BYTES = 56 * 1024 * 1024

HIST_B = 32
HIST_C = 16
HIST_D = 8
HIST_F = 8

TM = 512
TM_UP = 1024
SUB_UP = 512
TM_IN = 1024
SUB_MID = 512
TT = 256
ROWS = 32
TN_IN = 1024
TN_UP = 512
FF_PAD = -(-D_FF // TN_UP) * TN_UP
TN_DOWN = 512
BB_MIX = 32
BB_ATT = 8


def _layer_param(a, l):
    tail = a.shape[1:]
    return pl.BlockSpec((None,) + tail, lambda *_: (l,) + (0,) * len(tail))


def _params(*sem):
    return pltpu.CompilerParams(dimension_semantics=sem, vmem_limit_bytes=V7X_VMEM_LIMIT_BYTES)


def _rms(x, g):
    return x * lax.rsqrt(jnp.mean(x * x, axis=-1, keepdims=True) + EPS) * g


def _sigmoid(x):
    return 1.0 / (1.0 + jnp.exp(-x))


def _dot(a, b):
    return jnp.dot(a, b, preferred_element_type=F32)


def _cast_kernel(x_ref, o_ref):
    o_ref[...] = x_ref[...].astype(BF16)


def cast_weights(w, rows, cols=None, col_block=0):
    depth, r, c = w.shape
    cols = c if cols is None else cols
    return pl.pallas_call(
        _cast_kernel,
        grid=(depth, r // rows),
        in_specs=[pl.BlockSpec((None, rows, cols), lambda l, i: (l, i, col_block))],
        out_specs=pl.BlockSpec((None, rows, cols), lambda l, i: (l, i, 0)),
        out_shape=jax.ShapeDtypeStruct((depth, r, cols), BF16),
        compiler_params=_params("parallel", "parallel"),
        name="cast_weights",
    )(w)


def _cast_pad_cols_kernel(x_ref, o_ref):
    n = x_ref.shape[-1]
    o_ref[:, 0:n] = x_ref[...].astype(BF16)
    o_ref[:, n:] = jnp.zeros((o_ref.shape[0], o_ref.shape[1] - n), BF16)


def cast_weights_pad_cols(w, rows, cols, col_block, padded):
    depth, r, _ = w.shape
    return pl.pallas_call(
        _cast_pad_cols_kernel,
        grid=(depth, r // rows),
        in_specs=[pl.BlockSpec((None, rows, cols), lambda l, i: (l, i, col_block))],
        out_specs=pl.BlockSpec((None, rows, padded), lambda l, i: (l, i, 0)),
        out_shape=jax.ShapeDtypeStruct((depth, r, padded), BF16),
        compiler_params=_params("parallel", "parallel"),
        name="cast_weights_pad_cols",
    )(w)


def _cast_pad_rows_kernel(x_ref, o_ref, *, valid_rows):
    rows = o_ref.shape[0]
    row = pl.program_id(1) * rows + lax.broadcasted_iota(jnp.int32, o_ref.shape, 0)
    o_ref[...] = jnp.where(row < valid_rows, x_ref[...], 0.0).astype(BF16)


def cast_weights_pad_rows(w, rows, padded):
    depth, r, c = w.shape
    return pl.pallas_call(
        functools.partial(_cast_pad_rows_kernel, valid_rows=r),
        grid=(depth, padded // rows),
        in_specs=[pl.BlockSpec((None, rows, c), lambda l, i: (l, i, 0))],
        out_specs=pl.BlockSpec((None, rows, c), lambda l, i: (l, i, 0)),
        out_shape=jax.ShapeDtypeStruct((depth, padded, c), BF16),
        compiler_params=_params("parallel", "parallel"),
        name="cast_weights_pad_rows",
    )(w)


def _cast_pair_kernel(a_ref, b_ref, o_ref):
    n = a_ref.shape[-1]
    o_ref[:, 0:n] = a_ref[...].astype(BF16)
    o_ref[:, n:2 * n] = b_ref[...].astype(BF16)


def cast_weight_pair(a, b):
    depth, r, c = a.shape
    spec = pl.BlockSpec((None, r, c), lambda l: (l, 0, 0))
    return pl.pallas_call(
        _cast_pair_kernel,
        grid=(depth,),
        in_specs=[spec, spec],
        out_specs=pl.BlockSpec((None, r, 2 * c), lambda l: (l, 0, 0)),
        out_shape=jax.ShapeDtypeStruct((depth, r, 2 * c), BF16),
        compiler_params=_params("parallel"),
        name="cast_weight_pair",
    )(a, b)


def _norm_matmul_kernel(x_ref, g_ref, w_ref, o_ref, xn_ref, *, col_axis):
    @pl.when(pl.program_id(col_axis) == 0)
    def _():
        xn_ref[...] = _rms(x_ref[...], g_ref[...]).astype(BF16)

    o_ref[...] = _dot(xn_ref[...], w_ref[...])


def norm_matmul(x, g, w, l, tm, tn):
    m, k = x.shape
    n = w.shape[2]
    return pl.pallas_call(
        functools.partial(_norm_matmul_kernel, col_axis=1),
        grid=(m // tm, n // tn),
        in_specs=[pl.BlockSpec((tm, k), lambda i, j: (i, 0)),
                  _layer_param(g, l),
                  pl.BlockSpec((None, k, tn), lambda i, j: (l, 0, j))],
        out_specs=pl.BlockSpec((tm, tn), lambda i, j: (i, j)),
        out_shape=jax.ShapeDtypeStruct((m, n), F32),
        scratch_shapes=[pltpu.VMEM((tm, k), BF16)],
        compiler_params=_params("parallel", "arbitrary"),
        name="norm_matmul",
    )(x, g, w)


def norm_matmul_layers(x, g, w, tm, tn):
    m, k = x.shape
    depth, _, n = w.shape
    return pl.pallas_call(
        functools.partial(_norm_matmul_kernel, col_axis=2),
        grid=(depth, m // tm, n // tn),
        in_specs=[pl.BlockSpec((tm, k), lambda l, i, j: (i, 0)),
                  pl.BlockSpec((None, 1, k), lambda l, i, j: (l, 0, 0)),
                  pl.BlockSpec((None, k, tn), lambda l, i, j: (l, 0, j))],
        out_specs=pl.BlockSpec((None, tm, tn), lambda l, i, j: (l, i, j)),
        out_shape=jax.ShapeDtypeStruct((depth, m, n), F32),
        scratch_shapes=[pltpu.VMEM((tm, k), BF16)],
        compiler_params=_params("parallel", "parallel", "arbitrary"),
        name="norm_matmul_layers",
    )(x, g, w)


def _layer_norm(v, g):
    mu = jnp.mean(v, axis=-1, keepdims=True)
    d = v - mu
    var = jnp.mean(d * d, axis=-1, keepdims=True)
    return d * lax.rsqrt(var + EPS) * g


def _group_norm_silu(y, g, b):
    mu = jnp.mean(y, axis=-1, keepdims=True)
    d = y - mu
    var = jnp.mean(d * d, axis=-1, keepdims=True)
    yn = d * lax.rsqrt(var + EPS) * g + b
    return yn * _sigmoid(yn)


def _inmix_prompt_kernel(x_ref, gpre_ref, win_ref, ag_ref, aws_ref, abt_ref, bw_ref, bb_ref, gng_ref, gnb_ref,
                         clin_ref, cs_ref, dw_ref,
                         y_ref, nb_ref, np_ref, ns_ref,
                         extb, extc, extd, pooled):
    t = pl.program_id(1)
    tt = y_ref.shape[0]
    quarter = 2 * D_GROUP

    @pl.when(t == 0)
    def _():
        for s in range(V7X_SUBLANES):
            extb[s, 0:HIST_B, :] = jnp.zeros((HIST_B, D_GROUP), F32)
            extb[s, tt + HIST_B - V7X_SUBLANES:tt + HIST_B, :] = jnp.zeros((V7X_SUBLANES, D_GROUP), F32)
        extc[0:HIST_C, :] = jnp.zeros((HIST_C, D_GROUP), F32)
        extd[0:HIST_D, :] = jnp.zeros((HIST_D, D_GROUP), F32)

    xn = _rms(x_ref[...], gpre_ref[...]).astype(BF16)
    z = {}
    proj = lambda q: _dot(xn, win_ref[:, q * quarter:(q + 1) * quarter])

    def zcol(k, rs=slice(None)):
        return z[k // 2][rs, (k % 2) * D_GROUP:(k % 2 + 1) * D_GROUP]

    z[1] = proj(1)
    hb = zcol(2) * _sigmoid(zcol(3))
    for s in range(V7X_SUBLANES):
        extb[s, HIST_B - s:HIST_B - s + tt, :] = hb
    n_chunks = tt // ROWS
    later = {0: 0, n_chunks // 3: 2, 2 * n_chunks // 3: 3}
    for c in range(n_chunks):
        if c in later:
            z[later[c]] = proj(later[c])
        for g in range(N_SUB):
            ls = slice(g * D_SUB, (g + 1) * D_SUB)
            acc = jnp.zeros((ROWS, D_SUB), F32)
            for k in range(CONV_B_WIDTH):
                tiles, s = divmod(HIST_B - (CONV_B_WIDTH - 1) + k, V7X_SUBLANES)
                r0 = c * ROWS + tiles * V7X_SUBLANES
                acc = acc + bw_ref[k:k + 1, ls] * extb[s, r0:r0 + ROWS, ls]
            yb = _group_norm_silu(acc + bb_ref[:, ls], gng_ref[:, ls], gnb_ref[:, ls])
            y_ref[c * ROWS:(c + 1) * ROWS, D_GROUP + g * D_SUB:D_GROUP + (g + 1) * D_SUB] = yb.astype(BF16)

    vn = _layer_norm(zcol(1), ag_ref[...]).astype(BF16)
    row = lax.broadcasted_iota(jnp.int32, (CHUNK, CHUNK), 0)
    col = lax.broadcasted_iota(jnp.int32, (CHUNK, CHUNK), 1)
    for h in range(N_SUB):
        wm = jnp.where(row >= col, aws_ref[h], 0.0).astype(BF16)
        bias = abt_ref[:, h:h + 1]
        for c in range(tt // CHUNK):
            rs = slice(c * CHUNK, (c + 1) * CHUNK)
            ls = slice(h * D_SUB, (h + 1) * D_SUB)
            zz = _dot(wm, vn[rs, ls]) + bias
            y_ref[rs, ls] = (z[0][rs, ls] * zz).astype(BF16)

    extc[HIST_C:HIST_C + tt, :] = zcol(4)
    for c in range(tt // ROWS):
        pos = t * tt + c * ROWS + lax.broadcasted_iota(jnp.int32, (ROWS, 1), 0)
        for g, win in enumerate(POOL_WINDOWS):
            ls = slice(g * D_SUB, (g + 1) * D_SUB)
            r0 = c * ROWS + HIST_C
            x = extc[r0:r0 + ROWS, ls]
            s = x
            for i in range(1, win):
                s = s + extc[r0 - i:r0 - i + ROWS, ls]
            cnt = jnp.minimum(pos + 1, win).astype(F32)
            pooled[c * ROWS:(c + 1) * ROWS, ls] = (s / cnt - x).astype(BF16)
    for g in range(N_SUB):
        ls = slice(g * D_SUB, (g + 1) * D_SUB)
        yc = _dot(pooled[:, ls], clin_ref[g].astype(BF16)) * cs_ref[:, ls]
        y_ref[:, 2 * D_GROUP + g * D_SUB:2 * D_GROUP + (g + 1) * D_SUB] = yc.astype(BF16)

    extd[HIST_D:HIST_D + tt, :] = zcol(7) * zcol(5)
    for c in range(tt // ROWS):
        r0 = c * ROWS + HIST_D
        conv = (dw_ref[0:1, :] * extd[r0 - 2:r0 - 2 + ROWS, :]
                + dw_ref[1:2, :] * extd[r0 - 1:r0 - 1 + ROWS, :]
                + dw_ref[2:3, :] * extd[r0:r0 + ROWS, :])
        rs = slice(c * ROWS, (c + 1) * ROWS)
        y_ref[rs, 3 * D_GROUP:4 * D_GROUP] = (zcol(6, rs) * conv).astype(BF16)

    for s in range(V7X_SUBLANES):
        extb[s, 0:HIST_B, :] = extb[s, tt:tt + HIST_B, :]
    extc[0:HIST_C, :] = extc[tt:tt + HIST_C, :]
    extd[0:HIST_D, :] = extd[tt:tt + HIST_D, :]

    @pl.when(t == pl.num_programs(1) - 1)
    def _():
        nb_ref[...] = extb[0, 0:HIST_B, :]
        np_ref[...] = extc[0:HIST_C, :]
        ns_ref[...] = extd[0:HIST_D, :]


def inmix_prompt(x, gpre, win, ag, aws, abt, bw, bb, gng, gnb, clin, cs, dw, l):
    tt = TT
    full = lambda a: _layer_param(a, l)
    hist = lambda r: pl.BlockSpec((None, r, D_GROUP), lambda b, t: (b, 0, 0))
    return pl.pallas_call(
        _inmix_prompt_kernel,
        grid=(BATCH, SEQ // tt),
        in_specs=[pl.BlockSpec((None, tt, D_MODEL), lambda b, t: (b, t, 0)),
                  full(gpre),
                  pl.BlockSpec((None, D_MODEL, D_IN), lambda b, t: (l, 0, 0), pipeline_mode=pl.Buffered(1)),
                  full(ag), full(aws), full(abt), full(bw), full(bb), full(gng), full(gnb),
                  full(clin), full(cs), full(dw)],
        out_specs=[pl.BlockSpec((None, tt, D_MODEL), lambda b, t: (b, t, 0)),
                   hist(HIST_B), hist(HIST_C), hist(HIST_D)],
        out_shape=[jax.ShapeDtypeStruct((BATCH, SEQ, D_MODEL), BF16),
                   jax.ShapeDtypeStruct((BATCH, HIST_B, D_GROUP), F32),
                   jax.ShapeDtypeStruct((BATCH, HIST_C, D_GROUP), F32),
                   jax.ShapeDtypeStruct((BATCH, HIST_D, D_GROUP), F32)],
        scratch_shapes=[pltpu.VMEM((V7X_SUBLANES, HIST_B + tt, D_GROUP), F32),
                        pltpu.VMEM((HIST_C + tt, D_GROUP), F32),
                        pltpu.VMEM((HIST_D + tt, D_GROUP), F32),
                        pltpu.VMEM((tt, D_GROUP), BF16)],
        compiler_params=_params("parallel", "arbitrary"),
        name="inmix_prompt",
    )(x, gpre, win, ag, aws, abt, bw, bb, gng, gnb, clin, cs, dw)


def _mixer_sample_kernel(z_ref, cb_ref, cp_ref, csc_ref, ag_ref, aw4_ref, ab4_ref, bw_ref, bb_ref,
                         gng_ref, gnb_ref, clin_ref, cs_ref, dw_ref,
                         y_ref, nb_ref, np_ref, nsc_ref, v_ref, pooled):
    bblk = z_ref.shape[1]
    blk = lambda r: slice(r * D_GROUP, (r + 1) * D_GROUP)

    vn = [_layer_norm(z_ref[t, :, D_GROUP:2 * D_GROUP], ag_ref[...]) for t in range(DEC_SEQ)]
    for i in range(DEC_SEQ):
        v_ref[:, blk(i)] = vn[i]
        zz = ab4_ref[i:i + 1, :]
        for j in range(i + 1):
            zz = zz + aw4_ref[i * DEC_SEQ + j:i * DEC_SEQ + j + 1, :] * vn[j]
        y_ref[i, :, 0:D_GROUP] = (z_ref[i, :, 0:D_GROUP] * zz).astype(BF16)

    nprev = CONV_B_WIDTH - 1
    hb = [z_ref[t, :, 2 * D_GROUP:3 * D_GROUP] * _sigmoid(z_ref[t, :, 3 * D_GROUP:4 * D_GROUP])
          for t in range(DEC_SEQ)]
    ext_b = lambda r: cb_ref[:, blk(r)] if r < nprev else hb[r - nprev]
    for t in range(DEC_SEQ):
        acc = jnp.zeros((bblk, D_GROUP), F32)
        for k in range(CONV_B_WIDTH):
            acc = acc + bw_ref[k:k + 1, :] * ext_b(t + k)
        acc = acc + bb_ref[...]
        for g in range(N_SUB):
            ls = slice(g * D_SUB, (g + 1) * D_SUB)
            yb = _group_norm_silu(acc[:, ls], gng_ref[:, ls], gnb_ref[:, ls])
            y_ref[t, :, D_GROUP + g * D_SUB:D_GROUP + (g + 1) * D_SUB] = yb.astype(BF16)
    for r in range(nprev):
        nb_ref[:, blk(r)] = ext_b(r + DEC_SEQ)

    cx = [z_ref[t, :, 4 * D_GROUP:5 * D_GROUP] for t in range(DEC_SEQ)]
    ext_c = lambda r: cp_ref[:, blk(r)] if r < POOL_PREV else cx[r - POOL_PREV]
    for t in range(DEC_SEQ):
        for g, win in enumerate(POOL_WINDOWS):
            ls = slice(g * D_SUB, (g + 1) * D_SUB)
            s = cx[t][:, ls]
            for i in range(1, win):
                s = s + ext_c(POOL_PREV + t - i)[:, ls]
            cnt = float(min(PAST_LEN + t + 1, win))
            pooled[t * bblk:(t + 1) * bblk, ls] = (s / cnt - cx[t][:, ls]).astype(BF16)
    for g in range(N_SUB):
        ls = slice(g * D_SUB, (g + 1) * D_SUB)
        yc = _dot(pooled[:, ls], clin_ref[g].astype(BF16)) * cs_ref[:, ls]
        for t in range(DEC_SEQ):
            y_ref[t, :, 2 * D_GROUP + g * D_SUB:2 * D_GROUP + (g + 1) * D_SUB] = (
                yc[t * bblk:(t + 1) * bblk].astype(BF16))
    for r in range(POOL_PREV):
        np_ref[:, blk(r)] = ext_c(r + DEC_SEQ)

    nsp = SCONV_WIDTH - 1
    hd = [z_ref[t, :, 7 * D_GROUP:8 * D_GROUP] * z_ref[t, :, 5 * D_GROUP:6 * D_GROUP] for t in range(DEC_SEQ)]
    ext_d = lambda r: csc_ref[:, blk(r)] if r < nsp else hd[r - nsp]
    for t in range(DEC_SEQ):
        conv = dw_ref[0:1, :] * ext_d(t) + dw_ref[1:2, :] * ext_d(t + 1) + dw_ref[2:3, :] * ext_d(t + 2)
        y_ref[t, :, 3 * D_GROUP:4 * D_GROUP] = (z_ref[t, :, 6 * D_GROUP:7 * D_GROUP] * conv).astype(BF16)
    for r in range(nsp):
        nsc_ref[:, blk(r)] = ext_d(r + DEC_SEQ)


def mixer_sample(z, cb, cp, csc, ag, aw4, ab4, bw, bb, gng, gnb, clin, cs, dw, l):
    bblk = BB_MIX
    full = lambda a: _layer_param(a, l)
    st_in = lambda a: pl.BlockSpec((None, bblk, a.shape[2]), lambda b: (l, b, 0))
    st = lambda a: pl.BlockSpec((bblk, a.shape[2]), lambda b: (b, 0))
    nv = DEC_SEQ * D_GROUP
    return pl.pallas_call(
        _mixer_sample_kernel,
        grid=(DEC_BATCH // bblk,),
        in_specs=[pl.BlockSpec((DEC_SEQ, bblk, D_IN), lambda b: (0, b, 0)),
                  st_in(cb), st_in(cp), st_in(csc),
                  full(ag), full(aw4), full(ab4), full(bw), full(bb), full(gng), full(gnb),
                  full(clin), full(cs), full(dw)],
        out_specs=[pl.BlockSpec((DEC_SEQ, bblk, D_MODEL), lambda b: (0, b, 0)),
                   st(cb), st(cp), st(csc),
                   pl.BlockSpec((bblk, nv), lambda b: (b, 0))],
        out_shape=[jax.ShapeDtypeStruct((DEC_SEQ, DEC_BATCH, D_MODEL), BF16),
                   jax.ShapeDtypeStruct(cb.shape[1:], F32),
                   jax.ShapeDtypeStruct(cp.shape[1:], F32),
                   jax.ShapeDtypeStruct(csc.shape[1:], F32),
                   jax.ShapeDtypeStruct((DEC_BATCH, nv), F32)],
        scratch_shapes=[pltpu.VMEM((DEC_SEQ * bblk, D_GROUP), BF16)],
        compiler_params=_params("parallel"),
        name="mixer_sample",
    )(z, cb, cp, csc, ag, aw4, ab4, bw, bb, gng, gnb, clin, cs, dw)


def _stage_a(y, h, wout_ref, gpost_ref, gpre_ref, wxq_ref):
    h1 = h + _rms(_dot(y, wout_ref[...]), gpost_ref[...])
    q = _dot(_rms(h1, gpre_ref[...]).astype(BF16), wxq_ref[...])
    return h1, q


def _stage_c(o, h1, wxo_ref, gpost_ref, gffn_ref):
    h2 = h1 + _rms(_dot(o, wxo_ref[...]), gpost_ref[...])
    return h2, _rms(h2, gffn_ref[...]).astype(BF16)


def _softmax_rows(s):
    e = jnp.exp(s - jnp.max(s, axis=-1, keepdims=True))
    return e / jnp.sum(e, axis=-1, keepdims=True)


_NT = (((1,), (1,)), ((), ()))


def _mid_prompt_kernel(y_ref, h_ref, wout_ref, gmp_ref, gxp_ref, wxq_ref, mk_ref, mv_ref, wxo_ref,
                       gxo_ref, gffn_ref, h2_ref, xn_ref):
    k = mk_ref[...].astype(BF16)
    v = mv_ref[...].astype(BF16)
    for r in range(y_ref.shape[0] // SUB_MID):
        rs = slice(r * SUB_MID, (r + 1) * SUB_MID)
        h1, q = _stage_a(y_ref[rs, :], h_ref[rs, :], wout_ref, gmp_ref, gxp_ref, wxq_ref)
        q = q.astype(BF16)
        heads = []
        for hd in range(N_XHEADS):
            ls = slice(hd * D_XHEAD, (hd + 1) * D_XHEAD)
            s = lax.dot_general(q[:, ls], k[:, ls], _NT, preferred_element_type=F32) * (D_XHEAD ** -0.5)
            heads.append(_dot(_softmax_rows(s).astype(BF16), v[:, ls]))
        o = jnp.concatenate(heads, axis=-1).astype(BF16)
        h2, xn = _stage_c(o, h1, wxo_ref, gxo_ref, gffn_ref)
        h2_ref[rs, :] = h2
        xn_ref[rs, :] = xn


def _layer_weight(w, l):
    return pl.BlockSpec((None,) + w.shape[1:], lambda i: (l, 0, 0), pipeline_mode=pl.Buffered(1))


def mid_prompt(y, h, wout, gmp, gxp, wxq, kv, wxo, gxo, gffn, l):
    m = y.shape[0]
    tm = TM
    per_seq = SEQ // tm
    rows = lambda c: pl.BlockSpec((tm, c), lambda i: (i, 0))
    full = lambda a: _layer_param(a, l)
    mem = lambda half: pl.BlockSpec((None, N_MEM, D_X), lambda i: (l, i // per_seq, half))
    return pl.pallas_call(
        _mid_prompt_kernel,
        grid=(m // tm,),
        in_specs=[rows(D_MODEL), rows(D_MODEL), _layer_weight(wout, l), full(gmp), full(gxp),
                  _layer_weight(wxq, l), mem(0), mem(1), _layer_weight(wxo, l), full(gxo), full(gffn)],
        out_specs=[rows(D_MODEL), rows(D_MODEL)],
        out_shape=[jax.ShapeDtypeStruct((m, D_MODEL), F32), jax.ShapeDtypeStruct((m, D_MODEL), BF16)],
        compiler_params=_params("parallel"),
        name="mid_prompt",
    )(y, h, wout, gmp, gxp, wxq, kv, kv, wxo, gxo, gffn)


def _mid_a_kernel(y_ref, h_ref, wout_ref, gmp_ref, gxp_ref, wxq_ref, h1_ref, q_ref):
    h1, q = _stage_a(y_ref[...], h_ref[...], wout_ref, gmp_ref, gxp_ref, wxq_ref)
    h1_ref[...] = h1
    q_ref[...] = q.astype(BF16)


def mid_a(y, h, wout, gmp, gxp, wxq, l):
    m = y.shape[0]
    tm = TM
    rows = lambda c: pl.BlockSpec((tm, c), lambda i: (i, 0))
    full = lambda a: _layer_param(a, l)
    return pl.pallas_call(
        _mid_a_kernel,
        grid=(m // tm,),
        in_specs=[rows(D_MODEL), rows(D_MODEL), _layer_weight(wout, l), full(gmp), full(gxp),
                  _layer_weight(wxq, l)],
        out_specs=[rows(D_MODEL), rows(D_X)],
        out_shape=[jax.ShapeDtypeStruct((m, D_MODEL), F32), jax.ShapeDtypeStruct((m, D_X), BF16)],
        compiler_params=_params("parallel"),
        name="mid_a",
    )(y, h, wout, gmp, gxp, wxq)


def _mid_c_kernel(o_ref, h1_ref, wxo_ref, gxo_ref, gffn_ref, h2_ref, xn_ref):
    h2, xn = _stage_c(o_ref[...], h1_ref[...], wxo_ref, gxo_ref, gffn_ref)
    h2_ref[...] = h2
    xn_ref[...] = xn


def mid_c(o, h1, wxo, gxo, gffn, l):
    m = o.shape[0]
    tm = TM
    rows = lambda c: pl.BlockSpec((tm, c), lambda i: (i, 0))
    full = lambda a: _layer_param(a, l)
    return pl.pallas_call(
        _mid_c_kernel,
        grid=(m // tm,),
        in_specs=[rows(D_X), rows(D_MODEL), _layer_weight(wxo, l), full(gxo), full(gffn)],
        out_specs=[rows(D_MODEL), rows(D_MODEL)],
        out_shape=[jax.ShapeDtypeStruct((m, D_MODEL), F32), jax.ShapeDtypeStruct((m, D_MODEL), BF16)],
        compiler_params=_params("parallel"),
        name="mid_c",
    )(o, h1, wxo, gxo, gffn)


def _attn_sample_kernel(q_ref, k_ref, v_ref, o_ref):
    bblk = q_ref.shape[0]
    lane_head = lax.broadcasted_iota(jnp.int32, (V7X_SUBLANES, D_X), 1) // D_XHEAD
    for b in range(bblk):
        q8 = q_ref[b]
        qbd = jnp.concatenate([jnp.where(lane_head == hd, q8, jnp.zeros_like(q8)) for hd in range(N_XHEADS)], axis=0)
        kb = k_ref[b].astype(BF16)
        vb = v_ref[b].astype(BF16)
        s = lax.dot_general(qbd, kb, _NT, preferred_element_type=F32) * (D_XHEAD ** -0.5)
        of = _dot(_softmax_rows(s).astype(BF16), vb)
        o8 = jnp.zeros((V7X_SUBLANES, D_X), F32)
        for hd in range(N_XHEADS):
            o8 = o8 + jnp.where(lane_head == hd, of[hd * V7X_SUBLANES:(hd + 1) * V7X_SUBLANES], 0.0)
        o_ref[b] = o8.astype(BF16)


def attn_sample(q, k, v, l):
    bblk = BB_ATT
    qs = pl.BlockSpec((bblk, V7X_SUBLANES, D_X), lambda b: (b, 0, 0))
    ms = pl.BlockSpec((None, bblk, N_MEM, D_X), lambda b: (l, b, 0, 0))
    return pl.pallas_call(
        _attn_sample_kernel,
        grid=(DEC_BATCH // bblk,),
        in_specs=[qs, ms, ms],
        out_specs=qs,
        out_shape=jax.ShapeDtypeStruct(q.shape, BF16),
        compiler_params=_params("parallel"),
        name="attn_sample",
    )(q, k, v)


def _gate(cg, cu):
    return (cg * _sigmoid(cg) * cu).astype(BF16)


def _shift_conv3(prev, h, fw_ref):
    rows = h.shape[0]
    ext = jnp.concatenate([prev, h], axis=0)
    return (fw_ref[0:1, :] * ext[HIST_F - 2:HIST_F - 2 + rows]
            + fw_ref[1:2, :] * ext[HIST_F - 1:HIST_F - 1 + rows]
            + fw_ref[2:3, :] * h)


def _up_prompt_kernel(xn_ref, wg_ref, wu_ref, fwg_ref, fwu_ref, o_ref, tg_ref, tu_ref,
                      carg, caru, *, per_seq, sub):
    i = pl.program_id(0)
    j = pl.program_id(1)
    tm = xn_ref.shape[0]

    @pl.when(i % per_seq == 0)
    def _():
        carg[j] = jnp.zeros(carg.shape[1:], F32)
        caru[j] = jnp.zeros(caru.shape[1:], F32)

    prev_g = carg[j]
    prev_u = caru[j]
    for r in range(tm // sub):
        rs = slice(r * sub, (r + 1) * sub)
        hg = _dot(xn_ref[rs, :], wg_ref[...])
        hu = _dot(xn_ref[rs, :], wu_ref[...])
        o_ref[rs, :] = _gate(_shift_conv3(prev_g, hg, fwg_ref), _shift_conv3(prev_u, hu, fwu_ref))
        prev_g = hg[sub - HIST_F:]
        prev_u = hu[sub - HIST_F:]
    carg[j] = prev_g
    caru[j] = prev_u
    tg_ref[...] = prev_g
    tu_ref[...] = prev_u


def up_prompt(xn, wg, wu, fwg, fwu, l):
    m, k = xn.shape
    tm, tn = TM_UP, TN_UP
    nj = FF_PAD // tn
    wcol = pl.BlockSpec((None, k, tn), lambda i, j: (l, 0, j))
    fcol = pl.BlockSpec((None, HIST_F, tn), lambda i, j: (l, 0, j))
    tail = pl.BlockSpec((HIST_F, tn), lambda i, j: (i, j))
    return pl.pallas_call(
        functools.partial(_up_prompt_kernel, per_seq=SEQ // tm, sub=SUB_UP),
        grid=(m // tm, nj),
        in_specs=[pl.BlockSpec((tm, k), lambda i, j: (i, 0)), wcol, wcol, fcol, fcol],
        out_specs=[pl.BlockSpec((tm, tn), lambda i, j: (i, j)), tail, tail],
        out_shape=[jax.ShapeDtypeStruct((m, FF_PAD), BF16),
                   jax.ShapeDtypeStruct((m // tm * HIST_F, FF_PAD), F32),
                   jax.ShapeDtypeStruct((m // tm * HIST_F, FF_PAD), F32)],
        scratch_shapes=[pltpu.VMEM((nj, HIST_F, tn), F32), pltpu.VMEM((nj, HIST_F, tn), F32)],
        compiler_params=_params("arbitrary", "arbitrary"),
        name="up_prompt",
    )(xn, wg, wu, fwg, fwu)


def _down_kernel(g_ref, w_ref, h_ref, gn_ref, o_ref, acc):
    j = pl.program_id(1)
    tn = w_ref.shape[1]
    acc[:, pl.ds(pl.multiple_of(j * tn, tn), tn)] = _dot(g_ref[...], w_ref[...])

    @pl.when(j == pl.num_programs(1) - 1)
    def _():
        o_ref[...] = h_ref[...] + _rms(acc[...], gn_ref[...])


def down(g, w, h, gn, l):
    m, k = g.shape
    tm, tn = TM, TN_DOWN
    return pl.pallas_call(
        _down_kernel,
        grid=(m // tm, D_MODEL // tn),
        in_specs=[pl.BlockSpec((tm, k), lambda i, j: (i, 0)),
                  pl.BlockSpec((None, k, tn), lambda i, j: (l, 0, j)),
                  pl.BlockSpec((tm, D_MODEL), lambda i, j: (i, 0)),
                  _layer_param(gn, l)],
        out_specs=pl.BlockSpec((tm, D_MODEL), lambda i, j: (i, 0)),
        out_shape=jax.ShapeDtypeStruct((m, D_MODEL), F32),
        scratch_shapes=[pltpu.VMEM((tm, D_MODEL), F32)],
        compiler_params=_params("parallel", "arbitrary"),
        name="down",
    )(g, w, h, gn)


def _ffn_sample_kernel(xn_ref, wg_ref, wu_ref, fwg_ref, fwu_ref, p0g_ref, p1g_ref, p0u_ref, p1u_ref,
                       wd_ref, h_ref, gn_ref, o_ref, tg_ref, tu_ref):
    nb = DEC_BATCH
    j = pl.program_id(0)

    @pl.when(j == 0)
    def _():
        o_ref[...] = jnp.zeros(o_ref.shape, F32)

    xn = xn_ref[...]
    hg = _dot(xn, wg_ref[...])
    hu = _dot(xn, wu_ref[...])
    tg_ref[...] = hg[(DEC_SEQ - 2) * nb:, :]
    tu_ref[...] = hu[(DEC_SEQ - 2) * nb:, :]
    ext_g = [p0g_ref[...], p1g_ref[...]] + [hg[t * nb:(t + 1) * nb] for t in range(DEC_SEQ)]
    ext_u = [p0u_ref[...], p1u_ref[...]] + [hu[t * nb:(t + 1) * nb] for t in range(DEC_SEQ)]
    for t in range(DEC_SEQ):
        cg = fwg_ref[0:1, :] * ext_g[t] + fwg_ref[1:2, :] * ext_g[t + 1] + fwg_ref[2:3, :] * ext_g[t + 2]
        cu = fwu_ref[0:1, :] * ext_u[t] + fwu_ref[1:2, :] * ext_u[t + 1] + fwu_ref[2:3, :] * ext_u[t + 2]
        ts = slice(t * nb, (t + 1) * nb)
        o_ref[ts, :] += _dot(_gate(cg, cu), wd_ref[...])

    @pl.when(j == pl.num_programs(0) - 1)
    def _():
        o_ref[...] = h_ref[...] + _rms(o_ref[...], gn_ref[...])


def ffn_sample(xn, wg, wu, fwg, fwu, prev, wd, h, gn, l):
    m, k = xn.shape
    tn = TN_UP
    nj = FF_PAD // tn
    col = lambda r: pl.BlockSpec((r, tn), lambda j: (0, j))
    wcol = pl.BlockSpec((None, k, tn), lambda j: (l, 0, j))
    fcol = pl.BlockSpec((None, HIST_F, tn), lambda j: (l, 0, j))
    prev_row = lambda r, half: pl.BlockSpec((None, None, None, DEC_BATCH, tn), lambda j: (l, r, half, 0, j))
    whole = lambda r, c: pl.BlockSpec((r, c), lambda j: (0, 0))
    nt = (FFN_CONV_WIDTH - 1) * DEC_BATCH
    return pl.pallas_call(
        _ffn_sample_kernel,
        grid=(nj,),
        in_specs=[whole(m, k), wcol, wcol, fcol, fcol, prev_row(0, 0), prev_row(1, 0), prev_row(0, 1), prev_row(1, 1),
                  pl.BlockSpec((None, tn, D_MODEL), lambda j: (l, j, 0)), whole(m, D_MODEL), _layer_param(gn, l)],
        out_specs=[whole(m, D_MODEL), col(nt), col(nt)],
        out_shape=[jax.ShapeDtypeStruct((m, D_MODEL), F32),
                   jax.ShapeDtypeStruct((nt, FF_PAD), F32),
                   jax.ShapeDtypeStruct((nt, FF_PAD), F32)],
        compiler_params=_params("arbitrary"),
        name="ffn_sample",
    )(xn, wg, wu, fwg, fwu, prev, prev, prev, prev, wd, h, gn)


def kernel(x_prompt, x_sample, cache_mem_k, cache_mem_v, state_conv_b, state_pool, state_sconv, state_ffn_conv, mem_prompt, g_mix_pre, g_mix_post, g_mem, g_x_pre, g_x_post, g_ffn_pre, g_ffn_post, w_in, w_out, a_norm_g, a_ws, a_bs, b_conv_w, b_conv_b, b_gn_g, b_gn_b, c_lin, c_scale, d_conv_w, w_xq, w_xk, w_xv, w_xo, w_up, f_conv_w, w_down):
    nb, ns = DEC_BATCH, DEC_SEQ
    w_in_b = cast_weights(w_in, 512)
    w_out_b = cast_weights(w_out, 1024)
    w_xq_b = cast_weights(w_xq, D_MODEL)
    w_kv_b = cast_weight_pair(w_xk, w_xv)
    w_xo_b = cast_weights(w_xo, D_X)
    w_g_b = cast_weights_pad_cols(w_up, 256, D_FF, 0, FF_PAD)
    w_u_b = cast_weights_pad_cols(w_up, 256, D_FF, 1, FF_PAD)
    w_down_b = cast_weights_pad_rows(w_down, TN_UP, FF_PAD)

    hp = x_prompt.reshape(BATCH * SEQ, D_MODEL)
    hs = jnp.transpose(x_sample, (1, 0, 2)).reshape(ns * nb, D_MODEL)
    mem = mem_prompt.reshape(BATCH * N_MEM, D_MODEL)
    st_b = state_conv_b.reshape(DEPTH, nb, -1)
    st_p = state_pool.reshape(DEPTH, nb, -1)
    st_s = state_sconv.reshape(DEPTH, nb, -1)

    row = lambda a: a.reshape(DEPTH, 1, -1)
    ag, bb, gng, gnb, cs = (row(a) for a in (a_norm_g, b_conv_b, b_gn_g, b_gn_b, c_scale))
    gpre, gmp, gxp, gxo, gfp, gfo = (row(a) for a in (g_mix_pre, g_mix_post, g_x_pre, g_x_post, g_ffn_pre,
                                                      g_ffn_post))
    abt = jnp.transpose(a_bs, (0, 2, 1))
    pad_rows = lambda a, rows: jnp.pad(a, ((0, 0), (0, rows - a.shape[1]), (0, 0)))
    bw = pad_rows(b_conv_w, HIST_B)
    dw = pad_rows(d_conv_w, HIST_D)
    fw = jnp.pad(f_conv_w.reshape(DEPTH, FFN_CONV_WIDTH, 2, D_FF),
                 ((0, 0), (0, HIST_F - FFN_CONV_WIDTH), (0, 0), (0, FF_PAD - D_FF)))
    fwg, fwu = fw[:, :, 0], fw[:, :, 1]
    aw4 = jnp.repeat(jnp.transpose(a_ws[:, :, :ns, :ns], (0, 2, 3, 1)).reshape(DEPTH, ns * ns, N_SUB), D_SUB, axis=2)
    ab4 = jnp.repeat(jnp.transpose(a_bs[:, :, :ns], (0, 2, 1)), D_SUB, axis=2)
    keep = FFN_CONV_WIDTH - 1
    prev_f = jnp.pad(jnp.transpose(state_ffn_conv.reshape(DEPTH, nb, keep, 2, D_FF), (0, 2, 3, 1, 4)),
                     ((0, 0),) * 4 + ((0, FF_PAD - D_FF),))

    kv = norm_matmul_layers(mem, row(g_mem), w_kv_b, TM, D_X)

    p_states, s_states = [], []
    for l in range(DEPTH):
        y, nbp, npp, nsp = inmix_prompt(hp.reshape(BATCH, SEQ, D_MODEL), gpre, w_in_b, ag, a_ws, abt, bw, bb,
                                        gng, gnb, c_lin, cs, dw, l)
        h2, xn = mid_prompt(y.reshape(BATCH * SEQ, D_MODEL), hp, w_out_b, gmp, gxp, w_xq_b, kv, w_xo_b, gxo, gfp, l)
        gate, tg, tu = up_prompt(xn, w_g_b, w_u_b, fwg, fwu, l)
        hp = down(gate, w_down_b, h2, gfo, l)
        p_states.append((nbp, npp, nsp, tg, tu))

        zs = norm_matmul(hs, gpre, w_in_b, l, TM, TN_IN)
        ys, nbs, nps, nss, vs = mixer_sample(zs.reshape(ns, nb, D_IN), st_b, st_p, st_s, ag, aw4, ab4, bw, bb,
                                             gng, gnb, c_lin, cs, dw, l)
        h1s, qs = mid_a(ys.reshape(ns * nb, D_MODEL), hs, w_out_b, gmp, gxp, w_xq_b, l)
        q8 = jnp.pad(jnp.transpose(qs.reshape(ns, nb, D_X), (1, 0, 2)), ((0, 0), (0, V7X_SUBLANES - ns), (0, 0)))
        o8 = attn_sample(q8, cache_mem_k, cache_mem_v, l)
        os_ = jnp.transpose(o8[:, :ns], (1, 0, 2)).reshape(ns * nb, D_X)
        h2s, xns = mid_c(os_, h1s, w_xo_b, gxo, gfp, l)
        hs, tgs, tus = ffn_sample(xns, w_g_b, w_u_b, fwg, fwu, prev_f, w_down_b, h2s, gfo, l)
        s_states.append((nbs, nps, nss, tgs, tus, vs))

    nbp, npp, nsp, tg, tu = (jnp.stack(a) for a in zip(*p_states))
    nbs, nps, nss, tgs, tus, vs = (jnp.stack(a) for a in zip(*s_states))
    last = lambda a: a.reshape(DEPTH, BATCH, -1, HIST_F, FF_PAD)[:, :, -1, HIST_F - keep:, :D_FF]
    bmajor = lambda a: jnp.transpose(a.reshape(DEPTH, keep, nb, FF_PAD)[..., :D_FF], (0, 2, 1, 3))
    return (hp.reshape(BATCH, SEQ, D_MODEL),
            jnp.transpose(hs.reshape(ns, nb, D_MODEL), (1, 0, 2)),
            kv[:, :, :D_X].reshape(DEPTH, BATCH, N_MEM, D_X),
            kv[:, :, D_X:].reshape(DEPTH, BATCH, N_MEM, D_X),
            nbp[:, :, HIST_B - (CONV_B_WIDTH - 1):],
            npp[:, :, HIST_C - POOL_PREV:],
            nsp[:, :, HIST_D - (SCONV_WIDTH - 1):],
            jnp.concatenate([last(tg), last(tu)], axis=-1),
            nbs.reshape(DEPTH, nb, CONV_B_WIDTH - 1, D_GROUP),
            nps.reshape(DEPTH, nb, POOL_PREV, D_GROUP),
            nss.reshape(DEPTH, nb, SCONV_WIDTH - 1, D_GROUP),
            jnp.concatenate([bmajor(tgs), bmajor(tus)], axis=-1),
            vs.reshape(DEPTH, nb, ns, D_GROUP))
```

```python
import functools

import jax
import jax.numpy as jnp
from jax import lax
from jax.experimental import pallas as pl
from jax.experimental.pallas import tpu as pltpu

F32 = jnp.float32
BF16 = jnp.bfloat16

D_MODEL = 2048
BATCH = 4
SEQ = 2048
DEPTH = 4
DEC_BATCH = 128
DEC_SEQ = 4
PAST_LEN = 16384
D_GROUP = 512
N_SUB = 4
D_SUB = 128
D_IN = 8 * D_GROUP
CHUNK = 128
CONV_B_WIDTH = 31
POOL_WINDOWS = (2, 4, 8, 16)
POOL_PREV = 15
SCONV_WIDTH = 3
FFN_CONV_WIDTH = 3
D_FF = 5504
N_MEM = 256
N_XHEADS = 4
D_XHEAD = 128
D_X = 512
EPS = 1e-6

V7X_SUBLANES = 8
V7X_LANES = 128
V7X_VMEM_LIMIT_BYTES = 56 * 1024 * 1024

HIST_B = 32
HIST_C = 16
HIST_D = 8
HIST_F = 8

TM = 512
TM_UP = 1024
SUBS_UP = (512, 512)
TM_IN = 1024
SUB_MID = 256
TT = 256
ROWS = 32
TN_IN = 1024
TN_UP = 512
FF_PAD = -(-D_FF // TN_UP) * TN_UP
TN_DOWN = 512
BB_MIX = 32
BB_ATT = 8


def _layer_param(a, l):
    tail = a.shape[1:]
    return pl.BlockSpec((None,) + tail, lambda *_: (l,) + (0,) * len(tail))


def _params(*sem):
    return pltpu.CompilerParams(dimension_semantics=sem, vmem_limit_bytes=V7X_VMEM_LIMIT_BYTES)


def _rms(x, g):
    return x * lax.rsqrt(jnp.mean(x * x, axis=-1, keepdims=True) + EPS) * g


def _sigmoid(x):
    return 1.0 / (1.0 + jnp.exp(-x))


def _dot(a, b):
    return jnp.dot(a, b, preferred_element_type=F32)


def _cast_kernel(x_ref, o_ref):
    o_ref[...] = x_ref[...].astype(BF16)


def cast_weights(w, rows, cols=None, col_block=0):
    depth, r, c = w.shape
    cols = c if cols is None else cols
    return pl.pallas_call(
        _cast_kernel,
        grid=(depth, r // rows),
        in_specs=[pl.BlockSpec((None, rows, cols), lambda l, i: (l, i, col_block))],
        out_specs=pl.BlockSpec((None, rows, cols), lambda l, i: (l, i, 0)),
        out_shape=jax.ShapeDtypeStruct((depth, r, cols), BF16),
        compiler_params=_params("parallel", "parallel"),
        name="cast_weights",
    )(w)


def _cast_pad_cols_kernel(x_ref, o_ref):
    n = x_ref.shape[-1]
    o_ref[:, 0:n] = x_ref[...].astype(BF16)
    o_ref[:, n:] = jnp.zeros((o_ref.shape[0], o_ref.shape[1] - n), BF16)


def cast_weights_pad_cols(w, rows, cols, col_block, padded):
    depth, r, _ = w.shape
    return pl.pallas_call(
        _cast_pad_cols_kernel,
        grid=(depth, r // rows),
        in_specs=[pl.BlockSpec((None, rows, cols), lambda l, i: (l, i, col_block))],
        out_specs=pl.BlockSpec((None, rows, padded), lambda l, i: (l, i, 0)),
        out_shape=jax.ShapeDtypeStruct((depth, r, padded), BF16),
        compiler_params=_params("parallel", "parallel"),
        name="cast_weights_pad_cols",
    )(w)


def _cast_pad_rows_kernel(x_ref, o_ref, *, valid_rows):
    rows = o_ref.shape[0]
    row = pl.program_id(1) * rows + lax.broadcasted_iota(jnp.int32, o_ref.shape, 0)
    o_ref[...] = jnp.where(row < valid_rows, x_ref[...], 0.0).astype(BF16)


def cast_weights_pad_rows(w, rows, padded):
    depth, r, c = w.shape
    return pl.pallas_call(
        functools.partial(_cast_pad_rows_kernel, valid_rows=r),
        grid=(depth, padded // rows),
        in_specs=[pl.BlockSpec((None, rows, c), lambda l, i: (l, i, 0))],
        out_specs=pl.BlockSpec((None, rows, c), lambda l, i: (l, i, 0)),
        out_shape=jax.ShapeDtypeStruct((depth, padded, c), BF16),
        compiler_params=_params("parallel", "parallel"),
        name="cast_weights_pad_rows",
    )(w)


def _cast_pair_kernel(a_ref, b_ref, o_ref):
    n = a_ref.shape[-1]
    o_ref[:, 0:n] = a_ref[...].astype(BF16)
    o_ref[:, n:2 * n] = b_ref[...].astype(BF16)


def cast_weight_pair(a, b):
    depth, r, c = a.shape
    spec = pl.BlockSpec((None, r, c), lambda l: (l, 0, 0))
    return pl.pallas_call(
        _cast_pair_kernel,
        grid=(depth,),
        in_specs=[spec, spec],
        out_specs=pl.BlockSpec((None, r, 2 * c), lambda l: (l, 0, 0)),
        out_shape=jax.ShapeDtypeStruct((depth, r, 2 * c), BF16),
        compiler_params=_params("parallel"),
        name="cast_weight_pair",
    )(a, b)


def _norm_matmul_kernel(x_ref, g_ref, w_ref, o_ref, xn_ref, *, col_axis):
    @pl.when(pl.program_id(col_axis) == 0)
    def _():
        xn_ref[...] = _rms(x_ref[...], g_ref[...]).astype(BF16)

    o_ref[...] = _dot(xn_ref[...], w_ref[...])


def norm_matmul(x, g, w, l, tm, tn):
    m, k = x.shape
    n = w.shape[2]
    return pl.pallas_call(
        functools.partial(_norm_matmul_kernel, col_axis=1),
        grid=(m // tm, n // tn),
        in_specs=[pl.BlockSpec((tm, k), lambda i, j: (i, 0)),
                  _layer_param(g, l),
                  pl.BlockSpec((None, k, tn), lambda i, j: (l, 0, j))],
        out_specs=pl.BlockSpec((tm, tn), lambda i, j: (i, j)),
        out_shape=jax.ShapeDtypeStruct((m, n), F32),
        scratch_shapes=[pltpu.VMEM((tm, k), BF16)],
        compiler_params=_params("parallel", "arbitrary"),
        name="norm_matmul",
    )(x, g, w)


def norm_matmul_layers(x, g, w, tm, tn):
    m, k = x.shape
    depth, _, n = w.shape
    return pl.pallas_call(
        functools.partial(_norm_matmul_kernel, col_axis=2),
        grid=(depth, m // tm, n // tn),
        in_specs=[pl.BlockSpec((tm, k), lambda l, i, j: (i, 0)),
                  pl.BlockSpec((None, 1, k), lambda l, i, j: (l, 0, 0)),
                  pl.BlockSpec((None, k, tn), lambda l, i, j: (l, 0, j))],
        out_specs=pl.BlockSpec((None, tm, tn), lambda l, i, j: (l, i, j)),
        out_shape=jax.ShapeDtypeStruct((depth, m, n), F32),
        scratch_shapes=[pltpu.VMEM((tm, k), BF16)],
        compiler_params=_params("parallel", "parallel", "arbitrary"),
        name="norm_matmul_layers",
    )(x, g, w)


def _layer_norm(v, g):
    mu = jnp.mean(v, axis=-1, keepdims=True)
    d = v - mu
    var = jnp.mean(d * d, axis=-1, keepdims=True)
    return d * lax.rsqrt(var + EPS) * g


def _group_norm_silu(y, g, b):
    mu = jnp.mean(y, axis=-1, keepdims=True)
    d = y - mu
    var = jnp.mean(d * d, axis=-1, keepdims=True)
    yn = d * lax.rsqrt(var + EPS) * g + b
    return yn * _sigmoid(yn)


def _inmix_prompt_kernel(x_ref, gpre_ref, win_ref, ag_ref, aws_ref, abt_ref, bw_ref, bb_ref, gng_ref, gnb_ref,
                         clin_ref, cs_ref, dw_ref,
                         y_ref, nb_ref, np_ref, ns_ref,
                         extb, extc, extd, pooled):
    t = pl.program_id(1)
    tt = y_ref.shape[0]
    quarter = 2 * D_GROUP

    @pl.when(t == 0)
    def _():
        for s in range(V7X_SUBLANES):
            extb[s, 0:HIST_B, :] = jnp.zeros((HIST_B, D_GROUP), F32)
            extb[s, tt + HIST_B - V7X_SUBLANES:tt + HIST_B, :] = jnp.zeros((V7X_SUBLANES, D_GROUP), F32)
        extc[0:HIST_C, :] = jnp.zeros((HIST_C, D_GROUP), F32)
        extd[0:HIST_D, :] = jnp.zeros((HIST_D, D_GROUP), F32)

    xn = _rms(x_ref[...], gpre_ref[...]).astype(BF16)
    z = {}
    proj = lambda q: _dot(xn, win_ref[:, q * quarter:(q + 1) * quarter])

    def zcol(k, rs=slice(None)):
        return z[k // 2][rs, (k % 2) * D_GROUP:(k % 2 + 1) * D_GROUP]

    z[1] = proj(1)
    hb = zcol(2) * _sigmoid(zcol(3))
    for s in range(V7X_SUBLANES):
        extb[s, HIST_B - s:HIST_B - s + tt, :] = hb
    n_chunks = tt // ROWS
    later = {0: 0, n_chunks // 3: 2, 2 * n_chunks // 3: 3}
    for c in range(n_chunks):
        if c in later:
            z[later[c]] = proj(later[c])
        for g in range(N_SUB):
            ls = slice(g * D_SUB, (g + 1) * D_SUB)
            acc = jnp.zeros((ROWS, D_SUB), F32)
            for k in range(CONV_B_WIDTH):
                tiles, s = divmod(HIST_B - (CONV_B_WIDTH - 1) + k, V7X_SUBLANES)
                r0 = c * ROWS + tiles * V7X_SUBLANES
                acc = acc + bw_ref[k:k + 1, ls] * extb[s, r0:r0 + ROWS, ls]
            yb = _group_norm_silu(acc + bb_ref[:, ls], gng_ref[:, ls], gnb_ref[:, ls])
            y_ref[c * ROWS:(c + 1) * ROWS, D_GROUP + g * D_SUB:D_GROUP + (g + 1) * D_SUB] = yb.astype(BF16)

    vn = _layer_norm(zcol(1), ag_ref[...]).astype(BF16)
    row = lax.broadcasted_iota(jnp.int32, (CHUNK, CHUNK), 0)
    col = lax.broadcasted_iota(jnp.int32, (CHUNK, CHUNK), 1)
    for h in range(N_SUB):
        wm = jnp.where(row >= col, aws_ref[h], 0.0).astype(BF16)
        bias = abt_ref[:, h:h + 1]
        for c in range(tt // CHUNK):
            rs = slice(c * CHUNK, (c + 1) * CHUNK)
            ls = slice(h * D_SUB, (h + 1) * D_SUB)
            zz = _dot(wm, vn[rs, ls]) + bias
            y_ref[rs, ls] = (z[0][rs, ls] * zz).astype(BF16)

    extc[HIST_C:HIST_C + tt, :] = zcol(4)
    for c in range(tt // ROWS):
        pos = t * tt + c * ROWS + lax.broadcasted_iota(jnp.int32, (ROWS, 1), 0)
        for g, win in enumerate(POOL_WINDOWS):
            ls = slice(g * D_SUB, (g + 1) * D_SUB)
            r0 = c * ROWS + HIST_C
            x = extc[r0:r0 + ROWS, ls]
            s = x
            for i in range(1, win):
                s = s + extc[r0 - i:r0 - i + ROWS, ls]
            cnt = jnp.minimum(pos + 1, win).astype(F32)
            pooled[c * ROWS:(c + 1) * ROWS, ls] = (s / cnt - x).astype(BF16)
    for g in range(N_SUB):
        ls = slice(g * D_SUB, (g + 1) * D_SUB)
        yc = _dot(pooled[:, ls], clin_ref[g].astype(BF16)) * cs_ref[:, ls]
        y_ref[:, 2 * D_GROUP + g * D_SUB:2 * D_GROUP + (g + 1) * D_SUB] = yc.astype(BF16)

    extd[HIST_D:HIST_D + tt, :] = zcol(7) * zcol(5)
    for c in range(tt // ROWS):
        r0 = c * ROWS + HIST_D
        conv = (dw_ref[0:1, :] * extd[r0 - 2:r0 - 2 + ROWS, :]
                + dw_ref[1:2, :] * extd[r0 - 1:r0 - 1 + ROWS, :]
                + dw_ref[2:3, :] * extd[r0:r0 + ROWS, :])
        rs = slice(c * ROWS, (c + 1) * ROWS)
        y_ref[rs, 3 * D_GROUP:4 * D_GROUP] = (zcol(6, rs) * conv).astype(BF16)

    for s in range(V7X_SUBLANES):
        extb[s, 0:HIST_B, :] = extb[s, tt:tt + HIST_B, :]
    extc[0:HIST_C, :] = extc[tt:tt + HIST_C, :]
    extd[0:HIST_D, :] = extd[tt:tt + HIST_D, :]

    @pl.when(t == pl.num_programs(1) - 1)
    def _():
        nb_ref[...] = extb[0, 0:HIST_B, :]
        np_ref[...] = extc[0:HIST_C, :]
        ns_ref[...] = extd[0:HIST_D, :]


def inmix_prompt(x, gpre, win, ag, aws, abt, bw, bb, gng, gnb, clin, cs, dw, l):
    tt = TT
    full = lambda a: _layer_param(a, l)
    hist = lambda r: pl.BlockSpec((None, r, D_GROUP), lambda b, t: (b, 0, 0))
    return pl.pallas_call(
        _inmix_prompt_kernel,
        grid=(BATCH, SEQ // tt),
        in_specs=[pl.BlockSpec((None, tt, D_MODEL), lambda b, t: (b, t, 0)),
                  full(gpre),
                  pl.BlockSpec((None, D_MODEL, D_IN), lambda b, t: (l, 0, 0), pipeline_mode=pl.Buffered(1)),
                  full(ag), full(aws), full(abt), full(bw), full(bb), full(gng), full(gnb),
                  full(clin), full(cs), full(dw)],
        out_specs=[pl.BlockSpec((None, tt, D_MODEL), lambda b, t: (b, t, 0)),
                   hist(HIST_B), hist(HIST_C), hist(HIST_D)],
        out_shape=[jax.ShapeDtypeStruct((BATCH, SEQ, D_MODEL), BF16),
                   jax.ShapeDtypeStruct((BATCH, HIST_B, D_GROUP), F32),
                   jax.ShapeDtypeStruct((BATCH, HIST_C, D_GROUP), F32),
                   jax.ShapeDtypeStruct((BATCH, HIST_D, D_GROUP), F32)],
        scratch_shapes=[pltpu.VMEM((V7X_SUBLANES, HIST_B + tt, D_GROUP), F32),
                        pltpu.VMEM((HIST_C + tt, D_GROUP), F32),
                        pltpu.VMEM((HIST_D + tt, D_GROUP), F32),
                        pltpu.VMEM((tt, D_GROUP), BF16)],
        compiler_params=_params("parallel", "arbitrary"),
        name="inmix_prompt",
    )(x, gpre, win, ag, aws, abt, bw, bb, gng, gnb, clin, cs, dw)


def _mixer_sample_kernel(z_ref, cb_ref, cp_ref, csc_ref, ag_ref, aw4_ref, ab4_ref, bw_ref, bb_ref,
                         gng_ref, gnb_ref, clin_ref, cs_ref, dw_ref,
                         y_ref, nb_ref, np_ref, nsc_ref, v_ref, pooled):
    bblk = z_ref.shape[1]
    blk = lambda r: slice(r * D_GROUP, (r + 1) * D_GROUP)

    vn = [_layer_norm(z_ref[t, :, D_GROUP:2 * D_GROUP], ag_ref[...]) for t in range(DEC_SEQ)]
    for i in range(DEC_SEQ):
        v_ref[:, blk(i)] = vn[i]
        zz = ab4_ref[i:i + 1, :]
        for j in range(i + 1):
            zz = zz + aw4_ref[i * DEC_SEQ + j:i * DEC_SEQ + j + 1, :] * vn[j]
        y_ref[i, :, 0:D_GROUP] = (z_ref[i, :, 0:D_GROUP] * zz).astype(BF16)

    nprev = CONV_B_WIDTH - 1
    hb = [z_ref[t, :, 2 * D_GROUP:3 * D_GROUP] * _sigmoid(z_ref[t, :, 3 * D_GROUP:4 * D_GROUP])
          for t in range(DEC_SEQ)]
    ext_b = lambda r: cb_ref[:, blk(r)] if r < nprev else hb[r - nprev]
    for t in range(DEC_SEQ):
        acc = jnp.zeros((bblk, D_GROUP), F32)
        for k in range(CONV_B_WIDTH):
            acc = acc + bw_ref[k:k + 1, :] * ext_b(t + k)
        acc = acc + bb_ref[...]
        for g in range(N_SUB):
            ls = slice(g * D_SUB, (g + 1) * D_SUB)
            yb = _group_norm_silu(acc[:, ls], gng_ref[:, ls], gnb_ref[:, ls])
            y_ref[t, :, D_GROUP + g * D_SUB:D_GROUP + (g + 1) * D_SUB] = yb.astype(BF16)
    for r in range(nprev):
        nb_ref[:, blk(r)] = ext_b(r + DEC_SEQ)

    cx = [z_ref[t, :, 4 * D_GROUP:5 * D_GROUP] for t in range(DEC_SEQ)]
    ext_c = lambda r: cp_ref[:, blk(r)] if r < POOL_PREV else cx[r - POOL_PREV]
    for t in range(DEC_SEQ):
        for g, win in enumerate(POOL_WINDOWS):
            ls = slice(g * D_SUB, (g + 1) * D_SUB)
            s = cx[t][:, ls]
            for i in range(1, win):
                s = s + ext_c(POOL_PREV + t - i)[:, ls]
            cnt = float(min(PAST_LEN + t + 1, win))
            pooled[t * bblk:(t + 1) * bblk, ls] = (s / cnt - cx[t][:, ls]).astype(BF16)
    for g in range(N_SUB):
        ls = slice(g * D_SUB, (g + 1) * D_SUB)
        yc = _dot(pooled[:, ls], clin_ref[g].astype(BF16)) * cs_ref[:, ls]
        for t in range(DEC_SEQ):
            y_ref[t, :, 2 * D_GROUP + g * D_SUB:2 * D_GROUP + (g + 1) * D_SUB] = (
                yc[t * bblk:(t + 1) * bblk].astype(BF16))
    for r in range(POOL_PREV):
        np_ref[:, blk(r)] = ext_c(r + DEC_SEQ)

    nsp = SCONV_WIDTH - 1
    hd = [z_ref[t, :, 7 * D_GROUP:8 * D_GROUP] * z_ref[t, :, 5 * D_GROUP:6 * D_GROUP] for t in range(DEC_SEQ)]
    ext_d = lambda r: csc_ref[:, blk(r)] if r < nsp else hd[r - nsp]
    for t in range(DEC_SEQ):
        conv = dw_ref[0:1, :] * ext_d(t) + dw_ref[1:2, :] * ext_d(t + 1) + dw_ref[2:3, :] * ext_d(t + 2)
        y_ref[t, :, 3 * D_GROUP:4 * D_GROUP] = (z_ref[t, :, 6 * D_GROUP:7 * D_GROUP] * conv).astype(BF16)
    for r in range(nsp):
        nsc_ref[:, blk(r)] = ext_d(r + DEC_SEQ)


def mixer_sample(z, cb, cp, csc, ag, aw4, ab4, bw, bb, gng, gnb, clin, cs, dw, l):
    bblk = BB_MIX
    full = lambda a: _layer_param(a, l)
    st_in = lambda a: pl.BlockSpec((None, bblk, a.shape[2]), lambda b: (l, b, 0))
    st = lambda a: pl.BlockSpec((bblk, a.shape[2]), lambda b: (b, 0))
    nv = DEC_SEQ * D_GROUP
    return pl.pallas_call(
        _mixer_sample_kernel,
        grid=(DEC_BATCH // bblk,),
        in_specs=[pl.BlockSpec((DEC_SEQ, bblk, D_IN), lambda b: (0, b, 0)),
                  st_in(cb), st_in(cp), st_in(csc),
                  full(ag), full(aw4), full(ab4), full(bw), full(bb), full(gng), full(gnb),
                  full(clin), full(cs), full(dw)],
        out_specs=[pl.BlockSpec((DEC_SEQ, bblk, D_MODEL), lambda b: (0, b, 0)),
                   st(cb), st(cp), st(csc),
                   pl.BlockSpec((bblk, nv), lambda b: (b, 0))],
        out_shape=[jax.ShapeDtypeStruct((DEC_SEQ, DEC_BATCH, D_MODEL), BF16),
                   jax.ShapeDtypeStruct(cb.shape[1:], F32),
                   jax.ShapeDtypeStruct(cp.shape[1:], F32),
                   jax.ShapeDtypeStruct(csc.shape[1:], F32),
                   jax.ShapeDtypeStruct((DEC_BATCH, nv), F32)],
        scratch_shapes=[pltpu.VMEM((DEC_SEQ * bblk, D_GROUP), BF16)],
        compiler_params=_params("parallel"),
        name="mixer_sample",
    )(z, cb, cp, csc, ag, aw4, ab4, bw, bb, gng, gnb, clin, cs, dw)


def _stage_a(y, h, wout_ref, gpost_ref, gpre_ref, wxq_ref):
    h1 = h + _rms(_dot(y, wout_ref[...]), gpost_ref[...])
    q = _dot(_rms(h1, gpre_ref[...]).astype(BF16), wxq_ref[...])
    return h1, q


def _stage_c(o, h1, wxo_ref, gpost_ref, gffn_ref):
    h2 = h1 + _rms(_dot(o, wxo_ref[...]), gpost_ref[...])
    return h2, _rms(h2, gffn_ref[...]).astype(BF16)


def _softmax_rows(s):
    e = jnp.exp(s - jnp.max(s, axis=-1, keepdims=True))
    return e / jnp.sum(e, axis=-1, keepdims=True)


_NT = (((1,), (1,)), ((), ()))


def _mid_prompt_kernel(y_ref, h_ref, wout_ref, gmp_ref, gxp_ref, wxq_ref, mk_ref, mv_ref, wxo_ref,
                       gxo_ref, gffn_ref, h2_ref, xn_ref):
    k = mk_ref[...].astype(BF16)
    v = mv_ref[...].astype(BF16)
    halves = [slice(r * SUB_MID, (r + 1) * SUB_MID) for r in range(y_ref.shape[0] // SUB_MID)]

    def out_proj(rs):
        return _dot(y_ref[rs, :], wout_ref[...])

    def residual_and_query_in(rs, mix):
        h1 = h_ref[rs, :] + _rms(mix, gmp_ref[...])
        return h1, _rms(h1, gxp_ref[...]).astype(BF16)

    def attention(xq):
        q = _dot(xq, wxq_ref[...]).astype(BF16)
        heads = []
        for hd in range(N_XHEADS):
            ls = slice(hd * D_XHEAD, (hd + 1) * D_XHEAD)
            s = lax.dot_general(q[:, ls], k[:, ls], _NT, preferred_element_type=F32) * (D_XHEAD ** -0.5)
            heads.append(_dot(_softmax_rows(s).astype(BF16), v[:, ls]))
        return _dot(jnp.concatenate(heads, axis=-1).astype(BF16), wxo_ref[...])

    def finish(rs, h1, xa):
        h2 = h1 + _rms(xa, gxo_ref[...])
        h2_ref[rs, :] = h2
        xn_ref[rs, :] = _rms(h2, gffn_ref[...]).astype(BF16)

    mix = [out_proj(rs) for rs in halves]
    state = [residual_and_query_in(halves[0], mix[0])]
    xa = []
    for r, rs in enumerate(halves):
        xa.append(attention(state[r][1]))
        if r + 1 < len(halves):
            state.append(residual_and_query_in(halves[r + 1], mix[r + 1]))
        if r > 0:
            finish(halves[r - 1], state[r - 1][0], xa[r - 1])
    finish(halves[-1], state[-1][0], xa[-1])


def _layer_weight(w, l):
    return pl.BlockSpec((None,) + w.shape[1:], lambda i: (l, 0, 0), pipeline_mode=pl.Buffered(1))


def mid_prompt(y, h, wout, gmp, gxp, wxq, kv, wxo, gxo, gffn, l):
    m = y.shape[0]
    tm = TM
    per_seq = SEQ // tm
    rows = lambda c: pl.BlockSpec((tm, c), lambda i: (i, 0))
    full = lambda a: _layer_param(a, l)
    mem = lambda half: pl.BlockSpec((None, N_MEM, D_X), lambda i: (l, i // per_seq, half))
    return pl.pallas_call(
        _mid_prompt_kernel,
        grid=(m // tm,),
        in_specs=[rows(D_MODEL), rows(D_MODEL), _layer_weight(wout, l), full(gmp), full(gxp),
                  _layer_weight(wxq, l), mem(0), mem(1), _layer_weight(wxo, l), full(gxo), full(gffn)],
        out_specs=[rows(D_MODEL), rows(D_MODEL)],
        out_shape=[jax.ShapeDtypeStruct((m, D_MODEL), F32), jax.ShapeDtypeStruct((m, D_MODEL), BF16)],
        compiler_params=_params("parallel"),
        name="mid_prompt",
    )(y, h, wout, gmp, gxp, wxq, kv, kv, wxo, gxo, gffn)


def _mid_a_kernel(y_ref, h_ref, wout_ref, gmp_ref, gxp_ref, wxq_ref, h1_ref, q_ref):
    h1, q = _stage_a(y_ref[...], h_ref[...], wout_ref, gmp_ref, gxp_ref, wxq_ref)
    h1_ref[...] = h1
    q_ref[...] = q.astype(BF16)


def mid_a(y, h, wout, gmp, gxp, wxq, l):
    m = y.shape[0]
    tm = TM
    rows = lambda c: pl.BlockSpec((tm, c), lambda i: (i, 0))
    full = lambda a: _layer_param(a, l)
    return pl.pallas_call(
        _mid_a_kernel,
        grid=(m // tm,),
        in_specs=[rows(D_MODEL), rows(D_MODEL), _layer_weight(wout, l), full(gmp), full(gxp),
                  _layer_weight(wxq, l)],
        out_specs=[rows(D_MODEL), rows(D_X)],
        out_shape=[jax.ShapeDtypeStruct((m, D_MODEL), F32), jax.ShapeDtypeStruct((m, D_X), BF16)],
        compiler_params=_params("parallel"),
        name="mid_a",
    )(y, h, wout, gmp, gxp, wxq)


def _mid_c_kernel(o_ref, h1_ref, wxo_ref, gxo_ref, gffn_ref, h2_ref, xn_ref):
    h2, xn = _stage_c(o_ref[...], h1_ref[...], wxo_ref, gxo_ref, gffn_ref)
    h2_ref[...] = h2
    xn_ref[...] = xn


def mid_c(o, h1, wxo, gxo, gffn, l):
    m = o.shape[0]
    tm = TM
    rows = lambda c: pl.BlockSpec((tm, c), lambda i: (i, 0))
    full = lambda a: _layer_param(a, l)
    return pl.pallas_call(
        _mid_c_kernel,
        grid=(m // tm,),
        in_specs=[rows(D_X), rows(D_MODEL), _layer_weight(wxo, l), full(gxo), full(gffn)],
        out_specs=[rows(D_MODEL), rows(D_MODEL)],
        out_shape=[jax.ShapeDtypeStruct((m, D_MODEL), F32), jax.ShapeDtypeStruct((m, D_MODEL), BF16)],
        compiler_params=_params("parallel"),
        name="mid_c",
    )(o, h1, wxo, gxo, gffn)


def _attn_sample_kernel(q_ref, k_ref, v_ref, o_ref):
    bblk = q_ref.shape[0]
    lane_head = lax.broadcasted_iota(jnp.int32, (V7X_SUBLANES, D_X), 1) // D_XHEAD

    def scores(b):
        q8 = q_ref[b]
        qbd = jnp.concatenate([jnp.where(lane_head == hd, q8, jnp.zeros_like(q8)) for hd in range(N_XHEADS)], axis=0)
        return lax.dot_general(qbd, k_ref[b].astype(BF16), _NT, preferred_element_type=F32) * (D_XHEAD ** -0.5)

    def attend(b, s):
        of = _dot(_softmax_rows(s).astype(BF16), v_ref[b].astype(BF16))
        o8 = jnp.zeros((V7X_SUBLANES, D_X), F32)
        for hd in range(N_XHEADS):
            o8 = o8 + jnp.where(lane_head == hd, of[hd * V7X_SUBLANES:(hd + 1) * V7X_SUBLANES], 0.0)
        o_ref[b] = o8.astype(BF16)

    s = scores(0)
    for b in range(1, bblk):
        s_next = scores(b)
        attend(b - 1, s)
        s = s_next
    attend(bblk - 1, s)


def attn_sample(q, k, v, l):
    bblk = BB_ATT
    qs = pl.BlockSpec((bblk, V7X_SUBLANES, D_X), lambda b: (b, 0, 0))
    ms = pl.BlockSpec((None, bblk, N_MEM, D_X), lambda b: (l, b, 0, 0))
    return pl.pallas_call(
        _attn_sample_kernel,
        grid=(DEC_BATCH // bblk,),
        in_specs=[qs, ms, ms],
        out_specs=qs,
        out_shape=jax.ShapeDtypeStruct(q.shape, BF16),
        compiler_params=_params("parallel"),
        name="attn_sample",
    )(q, k, v)


def _gate(cg, cu):
    return (cg * _sigmoid(cg) * cu).astype(BF16)


def _shift_conv3(prev, h, fw_ref):
    rows = h.shape[0]
    ext = jnp.concatenate([prev, h], axis=0)
    return (fw_ref[0:1, :] * ext[HIST_F - 2:HIST_F - 2 + rows]
            + fw_ref[1:2, :] * ext[HIST_F - 1:HIST_F - 1 + rows]
            + fw_ref[2:3, :] * h)


def _up_prompt_kernel(xn_ref, wg_ref, wu_ref, fwg_ref, fwu_ref, o_ref, tg_ref, tu_ref,
                      carg, caru, *, per_seq, subs):
    i = pl.program_id(0)
    j = pl.program_id(1)

    @pl.when(i % per_seq == 0)
    def _():
        carg[j] = jnp.zeros(carg.shape[1:], F32)
        caru[j] = jnp.zeros(caru.shape[1:], F32)

    prev_g = carg[j]
    prev_u = caru[j]
    r0 = 0
    for sub in subs:
        rs = slice(r0, r0 + sub)
        r0 += sub
        hg = _dot(xn_ref[rs, :], wg_ref[...])
        hu = _dot(xn_ref[rs, :], wu_ref[...])
        o_ref[rs, :] = _gate(_shift_conv3(prev_g, hg, fwg_ref), _shift_conv3(prev_u, hu, fwu_ref))
        prev_g = hg[sub - HIST_F:]
        prev_u = hu[sub - HIST_F:]
    carg[j] = prev_g
    caru[j] = prev_u
    tg_ref[...] = prev_g
    tu_ref[...] = prev_u


def up_prompt(xn, wg, wu, fwg, fwu, l):
    m, k = xn.shape
    tm, tn = TM_UP, TN_UP
    assert sum(SUBS_UP) == tm
    nj = FF_PAD // tn
    wcol = pl.BlockSpec((None, k, tn), lambda i, j: (l, 0, j))
    fcol = pl.BlockSpec((None, HIST_F, tn), lambda i, j: (l, 0, j))
    tail = pl.BlockSpec((HIST_F, tn), lambda i, j: (i, j))
    return pl.pallas_call(
        functools.partial(_up_prompt_kernel, per_seq=SEQ // tm, subs=SUBS_UP),
        grid=(m // tm, nj),
        in_specs=[pl.BlockSpec((tm, k), lambda i, j: (i, 0)), wcol, wcol, fcol, fcol],
        out_specs=[pl.BlockSpec((tm, tn), lambda i, j: (i, j)), tail, tail],
        out_shape=[jax.ShapeDtypeStruct((m, FF_PAD), BF16),
                   jax.ShapeDtypeStruct((m // tm * HIST_F, FF_PAD), F32),
                   jax.ShapeDtypeStruct((m // tm * HIST_F, FF_PAD), F32)],
        scratch_shapes=[pltpu.VMEM((nj, HIST_F, tn), F32), pltpu.VMEM((nj, HIST_F, tn), F32)],
        compiler_params=_params("arbitrary", "arbitrary"),
        name="up_prompt",
    )(xn, wg, wu, fwg, fwu)


def _down_kernel(g_ref, w_ref, h_ref, gn_ref, o_ref, acc):
    j = pl.program_id(1)
    tn = w_ref.shape[1]
    acc[:, pl.ds(pl.multiple_of(j * tn, tn), tn)] = _dot(g_ref[...], w_ref[...])

    @pl.when(j == pl.num_programs(1) - 1)
    def _():
        o_ref[...] = h_ref[...] + _rms(acc[...], gn_ref[...])


def down(g, w, h, gn, l):
    m, k = g.shape
    tm, tn = TM, TN_DOWN
    return pl.pallas_call(
        _down_kernel,
        grid=(m // tm, D_MODEL // tn),
        in_specs=[pl.BlockSpec((tm, k), lambda i, j: (i, 0)),
                  pl.BlockSpec((None, k, tn), lambda i, j: (l, 0, j)),
                  pl.BlockSpec((tm, D_MODEL), lambda i, j: (i, 0)),
                  _layer_param(gn, l)],
        out_specs=pl.BlockSpec((tm, D_MODEL), lambda i, j: (i, 0)),
        out_shape=jax.ShapeDtypeStruct((m, D_MODEL), F32),
        scratch_shapes=[pltpu.VMEM((tm, D_MODEL), F32)],
        compiler_params=_params("parallel", "arbitrary"),
        name="down",
    )(g, w, h, gn)


def _ffn_sample_kernel(xn_ref, wg_ref, wu_ref, fwg_ref, fwu_ref, p0g_ref, p1g_ref, p0u_ref, p1u_ref,
                       wd_ref, h_ref, gn_ref, o_ref, tg_ref, tu_ref):
    nb = DEC_BATCH
    j = pl.program_id(0)

    @pl.when(j == 0)
    def _():
        o_ref[...] = jnp.zeros(o_ref.shape, F32)

    xn = xn_ref[...]
    hg = _dot(xn, wg_ref[...])
    hu = _dot(xn, wu_ref[...])
    tg_ref[...] = hg[(DEC_SEQ - 2) * nb:, :]
    tu_ref[...] = hu[(DEC_SEQ - 2) * nb:, :]
    ext_g = [p0g_ref[...], p1g_ref[...]] + [hg[t * nb:(t + 1) * nb] for t in range(DEC_SEQ)]
    ext_u = [p0u_ref[...], p1u_ref[...]] + [hu[t * nb:(t + 1) * nb] for t in range(DEC_SEQ)]
    for t in range(DEC_SEQ):
        cg = fwg_ref[0:1, :] * ext_g[t] + fwg_ref[1:2, :] * ext_g[t + 1] + fwg_ref[2:3, :] * ext_g[t + 2]
        cu = fwu_ref[0:1, :] * ext_u[t] + fwu_ref[1:2, :] * ext_u[t + 1] + fwu_ref[2:3, :] * ext_u[t + 2]
        ts = slice(t * nb, (t + 1) * nb)
        o_ref[ts, :] += _dot(_gate(cg, cu), wd_ref[...])

    @pl.when(j == pl.num_programs(0) - 1)
    def _():
        o_ref[...] = h_ref[...] + _rms(o_ref[...], gn_ref[...])


def ffn_sample(xn, wg, wu, fwg, fwu, prev, wd, h, gn, l):
    m, k = xn.shape
    tn = TN_UP
    nj = FF_PAD // tn
    col = lambda r: pl.BlockSpec((r, tn), lambda j: (0, j))
    wcol = pl.BlockSpec((None, k, tn), lambda j: (l, 0, j))
    fcol = pl.BlockSpec((None, HIST_F, tn), lambda j: (l, 0, j))
    prev_row = lambda r, half: pl.BlockSpec((None, DEC_BATCH, tn), lambda j: (l, 0, (2 * r + half) * nj + j))
    whole = lambda r, c: pl.BlockSpec((r, c), lambda j: (0, 0))
    nt = (FFN_CONV_WIDTH - 1) * DEC_BATCH
    return pl.pallas_call(
        _ffn_sample_kernel,
        grid=(nj,),
        in_specs=[whole(m, k), wcol, wcol, fcol, fcol, prev_row(0, 0), prev_row(1, 0), prev_row(0, 1), prev_row(1, 1),
                  pl.BlockSpec((None, tn, D_MODEL), lambda j: (l, j, 0)), whole(m, D_MODEL), _layer_param(gn, l)],
        out_specs=[whole(m, D_MODEL), col(nt), col(nt)],
        out_shape=[jax.ShapeDtypeStruct((m, D_MODEL), F32),
                   jax.ShapeDtypeStruct((nt, FF_PAD), F32),
                   jax.ShapeDtypeStruct((nt, FF_PAD), F32)],
        compiler_params=_params("arbitrary"),
        name="ffn_sample",
    )(xn, wg, wu, fwg, fwu, prev, prev, prev, prev, wd, h, gn)


def kernel(x_prompt, x_sample, cache_mem_k, cache_mem_v, state_conv_b, state_pool, state_sconv, state_ffn_conv, mem_prompt, g_mix_pre, g_mix_post, g_mem, g_x_pre, g_x_post, g_ffn_pre, g_ffn_post, w_in, w_out, a_norm_g, a_ws, a_bs, b_conv_w, b_conv_b, b_gn_g, b_gn_b, c_lin, c_scale, d_conv_w, w_xq, w_xk, w_xv, w_xo, w_up, f_conv_w, w_down):
    nb, ns = DEC_BATCH, DEC_SEQ
    w_in_b = cast_weights(w_in, 512)
    w_out_b = cast_weights(w_out, 1024)
    w_xq_b = cast_weights(w_xq, D_MODEL)
    w_kv_b = cast_weight_pair(w_xk, w_xv)
    w_xo_b = cast_weights(w_xo, D_X)
    w_g_b = cast_weights_pad_cols(w_up, 256, D_FF, 0, FF_PAD)
    w_u_b = cast_weights_pad_cols(w_up, 256, D_FF, 1, FF_PAD)
    w_down_b = cast_weights_pad_rows(w_down, TN_UP, FF_PAD)

    hp = x_prompt.reshape(BATCH * SEQ, D_MODEL)
    hs = jnp.transpose(x_sample, (1, 0, 2)).reshape(ns * nb, D_MODEL)
    mem = mem_prompt.reshape(BATCH * N_MEM, D_MODEL)
    st_b = state_conv_b.reshape(DEPTH, nb, -1)
    st_p = state_pool.reshape(DEPTH, nb, -1)
    st_s = state_sconv.reshape(DEPTH, nb, -1)

    row = lambda a: a.reshape(DEPTH, 1, -1)
    ag, bb, gng, gnb, cs = (row(a) for a in (a_norm_g, b_conv_b, b_gn_g, b_gn_b, c_scale))
    gpre, gmp, gxp, gxo, gfp, gfo = (row(a) for a in (g_mix_pre, g_mix_post, g_x_pre, g_x_post, g_ffn_pre,
                                                      g_ffn_post))
    abt = jnp.transpose(a_bs, (0, 2, 1))
    pad_rows = lambda a, rows: jnp.pad(a, ((0, 0), (0, rows - a.shape[1]), (0, 0)))
    bw = pad_rows(b_conv_w, HIST_B)
    dw = pad_rows(d_conv_w, HIST_D)
    fw = jnp.pad(f_conv_w.reshape(DEPTH, FFN_CONV_WIDTH, 2, D_FF),
                 ((0, 0), (0, HIST_F - FFN_CONV_WIDTH), (0, 0), (0, FF_PAD - D_FF)))
    fwg, fwu = fw[:, :, 0], fw[:, :, 1]
    aw4 = jnp.repeat(jnp.transpose(a_ws[:, :, :ns, :ns], (0, 2, 3, 1)).reshape(DEPTH, ns * ns, N_SUB), D_SUB, axis=2)
    ab4 = jnp.repeat(jnp.transpose(a_bs[:, :, :ns], (0, 2, 1)), D_SUB, axis=2)
    keep = FFN_CONV_WIDTH - 1
    prev_f = jnp.pad(state_ffn_conv.reshape(DEPTH, nb, keep * 2, D_FF),
                     ((0, 0), (0, 0), (0, 0), (0, FF_PAD - D_FF))).reshape(DEPTH, nb, keep * 2 * FF_PAD)

    kv = norm_matmul_layers(mem, row(g_mem), w_kv_b, TM, D_X)

    p_states, s_states = [], []
    for l in range(DEPTH):
        y, nbp, npp, nsp = inmix_prompt(hp.reshape(BATCH, SEQ, D_MODEL), gpre, w_in_b, ag, a_ws, abt, bw, bb,
                                        gng, gnb, c_lin, cs, dw, l)
        h2, xn = mid_prompt(y.reshape(BATCH * SEQ, D_MODEL), hp, w_out_b, gmp, gxp, w_xq_b, kv, w_xo_b, gxo, gfp, l)
        gate, tg, tu = up_prompt(xn, w_g_b, w_u_b, fwg, fwu, l)
        hp = down(gate, w_down_b, h2, gfo, l)
        p_states.append((nbp, npp, nsp, tg, tu))

        zs = norm_matmul(hs, gpre, w_in_b, l, TM, TN_IN)
        ys, nbs, nps, nss, vs = mixer_sample(zs.reshape(ns, nb, D_IN), st_b, st_p, st_s, ag, aw4, ab4, bw, bb,
                                             gng, gnb, c_lin, cs, dw, l)
        h1s, qs = mid_a(ys.reshape(ns * nb, D_MODEL), hs, w_out_b, gmp, gxp, w_xq_b, l)
        q8 = jnp.pad(jnp.transpose(qs.reshape(ns, nb, D_X), (1, 0, 2)), ((0, 0), (0, V7X_SUBLANES - ns), (0, 0)))
        o8 = attn_sample(q8, cache_mem_k, cache_mem_v, l)
        os_ = jnp.transpose(o8[:, :ns], (1, 0, 2)).reshape(ns * nb, D_X)
        h2s, xns = mid_c(os_, h1s, w_xo_b, gxo, gfp, l)
        hs, tgs, tus = ffn_sample(xns, w_g_b, w_u_b, fwg, fwu, prev_f, w_down_b, h2s, gfo, l)
        s_states.append((nbs, nps, nss, tgs, tus, vs))

    nbp, npp, nsp, tg, tu = (jnp.stack(a) for a in zip(*p_states))
    nbs, nps, nss, tgs, tus, vs = (jnp.stack(a) for a in zip(*s_states))
    last = lambda a: a.reshape(DEPTH, BATCH, -1, HIST_F, FF_PAD)[:, :, -1, HIST_F - keep:, :D_FF]
    bmajor = lambda a: jnp.transpose(a.reshape(DEPTH, keep, nb, FF_PAD)[..., :D_FF], (0, 2, 1, 3))
    return (hp.reshape(BATCH, SEQ, D_MODEL),
            jnp.transpose(hs.reshape(ns, nb, D_MODEL), (1, 0, 2)),
            kv[:, :, :D_X].reshape(DEPTH, BATCH, N_MEM, D_X),
            kv[:, :, D_X:].reshape(DEPTH, BATCH, N_MEM, D_X),
            nbp[:, :, HIST_B - (CONV_B_WIDTH - 1):],
            npp[:, :, HIST_C - POOL_PREV:],
            nsp[:, :, HIST_D - (SCONV_WIDTH - 1):],
            jnp.concatenate([last(tg), last(tu)], axis=-1),
            nbs.reshape(DEPTH, nb, CONV_B_WIDTH - 1, D_GROUP),
            nps.reshape(DEPTH, nb, POOL_PREV, D_GROUP),
            nss.reshape(DEPTH, nb, SCONV_WIDTH - 1, D_GROUP),
            jnp.concatenate([bmajor(tgs), bmajor(tus)], axis=-1),
            vs.reshape(DEPTH, nb, ns, D_GROUP))
```

```python
import functools

import jax
import jax.numpy as jnp
from jax import lax
from jax.experimental import pallas as pl
from jax.experimental.pallas import tpu as pltpu

F32 = jnp.float32
BF16 = jnp.bfloat16

D_MODEL = 2048
BATCH = 4
SEQ = 2048
DEPTH = 4
DEC_BATCH = 128
DEC_SEQ = 4
PAST_LEN = 16384
D_GROUP = 512
N_SUB = 4
D_SUB = 128
D_IN = 8 * D_GROUP
CHUNK = 128
CONV_B_WIDTH = 31
POOL_WINDOWS = (2, 4, 8, 16)
POOL_PREV = 15
SCONV_WIDTH = 3
FFN_CONV_WIDTH = 3
D_FF = 5504
N_MEM = 256
N_XHEADS = 4
D_XHEAD = 128
D_X = 512
EPS = 1e-6

V7X_SUBLANES = 8
V7X_LANES = 128
V7X_VMEM_LIMIT_BYTES = 56 * 1024 * 1024

HIST_B = 32
HIST_C = 16
HIST_D = 8
HIST_F = 8

TM = 512
TM_UP = 2048
SUBS_UP = (512, 512, 512, 512)
TM_IN = 1024
SUB_MID = 256
TT = 256
ROWS = 32
TN_IN = 1024
TN_UP = 512
FF_PAD = -(-D_FF // TN_UP) * TN_UP
TN_DOWN = 512
CAST_CHUNKS = 32
BB_MIX = 32
BB_ATT = 8


def _layer_param(a, l):
    tail = a.shape[1:]
    return pl.BlockSpec((None,) + tail, lambda *_: (l,) + (0,) * len(tail))


def _params(*sem):
    return pltpu.CompilerParams(dimension_semantics=sem, vmem_limit_bytes=V7X_VMEM_LIMIT_BYTES)


def _rms(x, g):
    return x * lax.rsqrt(jnp.mean(x * x, axis=-1, keepdims=True) + EPS) * g


def _sigmoid(x):
    return 1.0 / (1.0 + jnp.exp(-x))


def _dot(a, b):
    return jnp.dot(a, b, preferred_element_type=F32)


N_CAST = 7


def _cast_specs(layer, chunk):
    def pair(rows_total, cols, col_block=0, out_rows=None, out_cols=None):
        rows = (out_rows or rows_total) // CAST_CHUNKS
        src = pl.BlockSpec((None, rows, cols), lambda *g: (layer, chunk(*g), col_block))
        dst = pl.BlockSpec((rows, out_cols or cols), lambda *g: (chunk(*g), 0))
        return src, dst, jax.ShapeDtypeStruct((out_rows or rows_total, out_cols or cols), BF16)
    table = [pair(D_MODEL, D_IN), pair(D_MODEL, D_MODEL), pair(D_MODEL, D_X), pair(D_X, D_MODEL),
             pair(D_MODEL, D_FF, 0, out_cols=FF_PAD), pair(D_MODEL, D_FF, 1, out_cols=FF_PAD),
             pair(D_FF, D_MODEL, out_rows=FF_PAD)]
    return tuple(list(t) for t in zip(*table))


def _cast_chunk(c, src, dst):
    for s, d in zip(src[:4], dst[:4]):
        d[...] = s[...].astype(BF16)
    for s, d in zip(src[4:6], dst[4:6]):
        d[:, 0:D_FF] = s[...].astype(BF16)
        d[:, D_FF:] = jnp.zeros((d.shape[0], d.shape[1] - D_FF), BF16)
    rows = dst[6].shape[0]
    row = c * rows + lax.broadcasted_iota(jnp.int32, dst[6].shape, 0)
    dst[6][...] = jnp.where(row < D_FF, src[6][...], 0.0).astype(BF16)


def _cast_layer_kernel(*refs):
    _cast_chunk(pl.program_id(0), refs[:N_CAST], refs[N_CAST:])


def cast_layer(sources, layer):
    in_specs, out_specs, out_shape = _cast_specs(layer, lambda c: c)
    return pl.pallas_call(
        _cast_layer_kernel,
        grid=(CAST_CHUNKS,),
        in_specs=in_specs,
        out_specs=out_specs,
        out_shape=out_shape,
        compiler_params=_params("parallel"),
        name="cast_layer",
    )(*sources)


def _cast_pair_kernel(a_ref, b_ref, o_ref):
    n = a_ref.shape[-1]
    o_ref[:, 0:n] = a_ref[...].astype(BF16)
    o_ref[:, n:2 * n] = b_ref[...].astype(BF16)


def cast_weight_pair(a, b):
    depth, r, c = a.shape
    spec = pl.BlockSpec((None, r, c), lambda l: (l, 0, 0))
    return pl.pallas_call(
        _cast_pair_kernel,
        grid=(depth,),
        in_specs=[spec, spec],
        out_specs=pl.BlockSpec((None, r, 2 * c), lambda l: (l, 0, 0)),
        out_shape=jax.ShapeDtypeStruct((depth, r, 2 * c), BF16),
        compiler_params=_params("parallel"),
        name="cast_weight_pair",
    )(a, b)


def _norm_matmul_kernel(x_ref, g_ref, w_ref, o_ref, xn_ref, *, col_axis):
    @pl.when(pl.program_id(col_axis) == 0)
    def _():
        xn_ref[...] = _rms(x_ref[...], g_ref[...]).astype(BF16)

    o_ref[...] = _dot(xn_ref[...], w_ref[...])


def norm_matmul(x, g, w, l, tm, tn):
    m, k = x.shape
    n = w.shape[1]
    return pl.pallas_call(
        functools.partial(_norm_matmul_kernel, col_axis=1),
        grid=(m // tm, n // tn),
        in_specs=[pl.BlockSpec((tm, k), lambda i, j: (i, 0)),
                  _layer_param(g, l),
                  pl.BlockSpec((k, tn), lambda i, j: (0, j))],
        out_specs=pl.BlockSpec((tm, tn), lambda i, j: (i, j)),
        out_shape=jax.ShapeDtypeStruct((m, n), F32),
        scratch_shapes=[pltpu.VMEM((tm, k), BF16)],
        compiler_params=_params("parallel", "arbitrary"),
        name="norm_matmul",
    )(x, g, w)


def norm_matmul_layers(x, g, w, tm, tn):
    m, k = x.shape
    depth, _, n = w.shape
    return pl.pallas_call(
        functools.partial(_norm_matmul_kernel, col_axis=2),
        grid=(depth, m // tm, n // tn),
        in_specs=[pl.BlockSpec((tm, k), lambda l, i, j: (i, 0)),
                  pl.BlockSpec((None, 1, k), lambda l, i, j: (l, 0, 0)),
                  pl.BlockSpec((None, k, tn), lambda l, i, j: (l, 0, j))],
        out_specs=pl.BlockSpec((None, tm, tn), lambda l, i, j: (l, i, j)),
        out_shape=jax.ShapeDtypeStruct((depth, m, n), F32),
        scratch_shapes=[pltpu.VMEM((tm, k), BF16)],
        compiler_params=_params("parallel", "parallel", "arbitrary"),
        name="norm_matmul_layers",
    )(x, g, w)


def _layer_norm(v, g):
    mu = jnp.mean(v, axis=-1, keepdims=True)
    d = v - mu
    var = jnp.mean(d * d, axis=-1, keepdims=True)
    return d * lax.rsqrt(var + EPS) * g


def _group_norm_silu(y, g, b):
    mu = jnp.mean(y, axis=-1, keepdims=True)
    d = y - mu
    var = jnp.mean(d * d, axis=-1, keepdims=True)
    yn = d * lax.rsqrt(var + EPS) * g + b
    return yn * _sigmoid(yn)


def _inmix_prompt_kernel(x_ref, gpre_ref, win_ref, ag_ref, aws_ref, abt_ref, bw_ref, bb_ref, gng_ref, gnb_ref,
                         clin_ref, cs_ref, dw_ref,
                         y_ref, nb_ref, np_ref, ns_ref,
                         extb, extc, extd, pooled):
    t = pl.program_id(1)
    tt = y_ref.shape[0]
    quarter = 2 * D_GROUP

    @pl.when(t == 0)
    def _():
        for s in range(V7X_SUBLANES):
            extb[s, 0:HIST_B, :] = jnp.zeros((HIST_B, D_GROUP), F32)
            extb[s, tt + HIST_B - V7X_SUBLANES:tt + HIST_B, :] = jnp.zeros((V7X_SUBLANES, D_GROUP), F32)
        extc[0:HIST_C, :] = jnp.zeros((HIST_C, D_GROUP), F32)
        extd[0:HIST_D, :] = jnp.zeros((HIST_D, D_GROUP), F32)

    xn = _rms(x_ref[...], gpre_ref[...]).astype(BF16)
    z = {}
    proj = lambda q: _dot(xn, win_ref[:, q * quarter:(q + 1) * quarter])

    def zcol(k, rs=slice(None)):
        return z[k // 2][rs, (k % 2) * D_GROUP:(k % 2 + 1) * D_GROUP]

    z[1] = proj(1)
    hb = zcol(2) * _sigmoid(zcol(3))
    for s in range(V7X_SUBLANES):
        extb[s, HIST_B - s:HIST_B - s + tt, :] = hb
    n_chunks = tt // ROWS
    later = {0: 0, n_chunks // 3: 2, 2 * n_chunks // 3: 3}
    for c in range(n_chunks):
        if c in later:
            z[later[c]] = proj(later[c])
        for g in range(N_SUB):
            ls = slice(g * D_SUB, (g + 1) * D_SUB)
            acc = jnp.zeros((ROWS, D_SUB), F32)
            for k in range(CONV_B_WIDTH):
                tiles, s = divmod(HIST_B - (CONV_B_WIDTH - 1) + k, V7X_SUBLANES)
                r0 = c * ROWS + tiles * V7X_SUBLANES
                acc = acc + bw_ref[k:k + 1, ls] * extb[s, r0:r0 + ROWS, ls]
            yb = _group_norm_silu(acc + bb_ref[:, ls], gng_ref[:, ls], gnb_ref[:, ls])
            y_ref[c * ROWS:(c + 1) * ROWS, D_GROUP + g * D_SUB:D_GROUP + (g + 1) * D_SUB] = yb.astype(BF16)

    vn = _layer_norm(zcol(1), ag_ref[...]).astype(BF16)
    row = lax.broadcasted_iota(jnp.int32, (CHUNK, CHUNK), 0)
    col = lax.broadcasted_iota(jnp.int32, (CHUNK, CHUNK), 1)
    for h in range(N_SUB):
        wm = jnp.where(row >= col, aws_ref[h], 0.0).astype(BF16)
        bias = abt_ref[:, h:h + 1]
        for c in range(tt // CHUNK):
            rs = slice(c * CHUNK, (c + 1) * CHUNK)
            ls = slice(h * D_SUB, (h + 1) * D_SUB)
            zz = _dot(wm, vn[rs, ls]) + bias
            y_ref[rs, ls] = (z[0][rs, ls] * zz).astype(BF16)

    extc[HIST_C:HIST_C + tt, :] = zcol(4)
    for c in range(tt // ROWS):
        pos = t * tt + c * ROWS + lax.broadcasted_iota(jnp.int32, (ROWS, 1), 0)
        for g, win in enumerate(POOL_WINDOWS):
            ls = slice(g * D_SUB, (g + 1) * D_SUB)
            r0 = c * ROWS + HIST_C
            x = extc[r0:r0 + ROWS, ls]
            s = x
            for i in range(1, win):
                s = s + extc[r0 - i:r0 - i + ROWS, ls]
            cnt = jnp.minimum(pos + 1, win).astype(F32)
            pooled[c * ROWS:(c + 1) * ROWS, ls] = (s / cnt - x).astype(BF16)
    for g in range(N_SUB):
        ls = slice(g * D_SUB, (g + 1) * D_SUB)
        yc = _dot(pooled[:, ls], clin_ref[g].astype(BF16)) * cs_ref[:, ls]
        y_ref[:, 2 * D_GROUP + g * D_SUB:2 * D_GROUP + (g + 1) * D_SUB] = yc.astype(BF16)

    extd[HIST_D:HIST_D + tt, :] = zcol(7) * zcol(5)
    for c in range(tt // ROWS):
        r0 = c * ROWS + HIST_D
        conv = (dw_ref[0:1, :] * extd[r0 - 2:r0 - 2 + ROWS, :]
                + dw_ref[1:2, :] * extd[r0 - 1:r0 - 1 + ROWS, :]
                + dw_ref[2:3, :] * extd[r0:r0 + ROWS, :])
        rs = slice(c * ROWS, (c + 1) * ROWS)
        y_ref[rs, 3 * D_GROUP:4 * D_GROUP] = (zcol(6, rs) * conv).astype(BF16)

    for s in range(V7X_SUBLANES):
        extb[s, 0:HIST_B, :] = extb[s, tt:tt + HIST_B, :]
    extc[0:HIST_C, :] = extc[tt:tt + HIST_C, :]
    extd[0:HIST_D, :] = extd[tt:tt + HIST_D, :]

    @pl.when(t == pl.num_programs(1) - 1)
    def _():
        nb_ref[...] = extb[0, 0:HIST_B, :]
        np_ref[...] = extc[0:HIST_C, :]
        ns_ref[...] = extd[0:HIST_D, :]


def inmix_prompt(x, gpre, win, ag, aws, abt, bw, bb, gng, gnb, clin, cs, dw, l):
    tt = TT
    full = lambda a: _layer_param(a, l)
    hist = lambda r: pl.BlockSpec((None, r, D_GROUP), lambda b, t: (b, 0, 0))
    return pl.pallas_call(
        _inmix_prompt_kernel,
        grid=(BATCH, SEQ // tt),
        in_specs=[pl.BlockSpec((None, tt, D_MODEL), lambda b, t: (b, t, 0)),
                  full(gpre),
                  pl.BlockSpec((D_MODEL, D_IN), lambda b, t: (0, 0), pipeline_mode=pl.Buffered(1)),
                  full(ag), full(aws), full(abt), full(bw), full(bb), full(gng), full(gnb),
                  full(clin), full(cs), full(dw)],
        out_specs=[pl.BlockSpec((None, tt, D_MODEL), lambda b, t: (b, t, 0)),
                   hist(HIST_B), hist(HIST_C), hist(HIST_D)],
        out_shape=[jax.ShapeDtypeStruct((BATCH, SEQ, D_MODEL), BF16),
                   jax.ShapeDtypeStruct((BATCH, HIST_B, D_GROUP), F32),
                   jax.ShapeDtypeStruct((BATCH, HIST_C, D_GROUP), F32),
                   jax.ShapeDtypeStruct((BATCH, HIST_D, D_GROUP), F32)],
        scratch_shapes=[pltpu.VMEM((V7X_SUBLANES, HIST_B + tt, D_GROUP), F32),
                        pltpu.VMEM((HIST_C + tt, D_GROUP), F32),
                        pltpu.VMEM((HIST_D + tt, D_GROUP), F32),
                        pltpu.VMEM((tt, D_GROUP), BF16)],
        compiler_params=_params("parallel", "arbitrary"),
        name="inmix_prompt",
    )(x, gpre, win, ag, aws, abt, bw, bb, gng, gnb, clin, cs, dw)


def _mixer_sample_kernel(z_ref, cb_ref, cp_ref, csc_ref, ag_ref, aw4_ref, ab4_ref, bw_ref, bb_ref,
                         gng_ref, gnb_ref, clin_ref, cs_ref, dw_ref,
                         y_ref, nb_ref, np_ref, nsc_ref, v_ref, pooled):
    bblk = z_ref.shape[1]
    blk = lambda r: slice(r * D_GROUP, (r + 1) * D_GROUP)

    vn = [_layer_norm(z_ref[t, :, D_GROUP:2 * D_GROUP], ag_ref[...]) for t in range(DEC_SEQ)]
    for i in range(DEC_SEQ):
        v_ref[:, blk(i)] = vn[i]
        zz = ab4_ref[i:i + 1, :]
        for j in range(i + 1):
            zz = zz + aw4_ref[i * DEC_SEQ + j:i * DEC_SEQ + j + 1, :] * vn[j]
        y_ref[i, :, 0:D_GROUP] = (z_ref[i, :, 0:D_GROUP] * zz).astype(BF16)

    nprev = CONV_B_WIDTH - 1
    hb = [z_ref[t, :, 2 * D_GROUP:3 * D_GROUP] * _sigmoid(z_ref[t, :, 3 * D_GROUP:4 * D_GROUP])
          for t in range(DEC_SEQ)]
    ext_b = lambda r: cb_ref[:, blk(r)] if r < nprev else hb[r - nprev]
    for t in range(DEC_SEQ):
        acc = jnp.zeros((bblk, D_GROUP), F32)
        for k in range(CONV_B_WIDTH):
            acc = acc + bw_ref[k:k + 1, :] * ext_b(t + k)
        acc = acc + bb_ref[...]
        for g in range(N_SUB):
            ls = slice(g * D_SUB, (g + 1) * D_SUB)
            yb = _group_norm_silu(acc[:, ls], gng_ref[:, ls], gnb_ref[:, ls])
            y_ref[t, :, D_GROUP + g * D_SUB:D_GROUP + (g + 1) * D_SUB] = yb.astype(BF16)
    for r in range(nprev):
        nb_ref[:, blk(r)] = ext_b(r + DEC_SEQ)

    cx = [z_ref[t, :, 4 * D_GROUP:5 * D_GROUP] for t in range(DEC_SEQ)]
    ext_c = lambda r: cp_ref[:, blk(r)] if r < POOL_PREV else cx[r - POOL_PREV]
    for t in range(DEC_SEQ):
        for g, win in enumerate(POOL_WINDOWS):
            ls = slice(g * D_SUB, (g + 1) * D_SUB)
            s = cx[t][:, ls]
            for i in range(1, win):
                s = s + ext_c(POOL_PREV + t - i)[:, ls]
            cnt = float(min(PAST_LEN + t + 1, win))
            pooled[t * bblk:(t + 1) * bblk, ls] = (s / cnt - cx[t][:, ls]).astype(BF16)
    for g in range(N_SUB):
        ls = slice(g * D_SUB, (g + 1) * D_SUB)
        yc = _dot(pooled[:, ls], clin_ref[g].astype(BF16)) * cs_ref[:, ls]
        for t in range(DEC_SEQ):
            y_ref[t, :, 2 * D_GROUP + g * D_SUB:2 * D_GROUP + (g + 1) * D_SUB] = (
                yc[t * bblk:(t + 1) * bblk].astype(BF16))
    for r in range(POOL_PREV):
        np_ref[:, blk(r)] = ext_c(r + DEC_SEQ)

    nsp = SCONV_WIDTH - 1
    hd = [z_ref[t, :, 7 * D_GROUP:8 * D_GROUP] * z_ref[t, :, 5 * D_GROUP:6 * D_GROUP] for t in range(DEC_SEQ)]
    ext_d = lambda r: csc_ref[:, blk(r)] if r < nsp else hd[r - nsp]
    for t in range(DEC_SEQ):
        conv = dw_ref[0:1, :] * ext_d(t) + dw_ref[1:2, :] * ext_d(t + 1) + dw_ref[2:3, :] * ext_d(t + 2)
        y_ref[t, :, 3 * D_GROUP:4 * D_GROUP] = (z_ref[t, :, 6 * D_GROUP:7 * D_GROUP] * conv).astype(BF16)
    for r in range(nsp):
        nsc_ref[:, blk(r)] = ext_d(r + DEC_SEQ)


def mixer_sample(z, cb, cp, csc, ag, aw4, ab4, bw, bb, gng, gnb, clin, cs, dw, l):
    bblk = BB_MIX
    full = lambda a: _layer_param(a, l)
    st_in = lambda a: pl.BlockSpec((None, bblk, a.shape[2]), lambda b: (l, b, 0))
    st = lambda a: pl.BlockSpec((bblk, a.shape[2]), lambda b: (b, 0))
    nv = DEC_SEQ * D_GROUP
    return pl.pallas_call(
        _mixer_sample_kernel,
        grid=(DEC_BATCH // bblk,),
        in_specs=[pl.BlockSpec((DEC_SEQ, bblk, D_IN), lambda b: (0, b, 0)),
                  st_in(cb), st_in(cp), st_in(csc),
                  full(ag), full(aw4), full(ab4), full(bw), full(bb), full(gng), full(gnb),
                  full(clin), full(cs), full(dw)],
        out_specs=[pl.BlockSpec((DEC_SEQ, bblk, D_MODEL), lambda b: (0, b, 0)),
                   st(cb), st(cp), st(csc),
                   pl.BlockSpec((bblk, nv), lambda b: (b, 0))],
        out_shape=[jax.ShapeDtypeStruct((DEC_SEQ, DEC_BATCH, D_MODEL), BF16),
                   jax.ShapeDtypeStruct(cb.shape[1:], F32),
                   jax.ShapeDtypeStruct(cp.shape[1:], F32),
                   jax.ShapeDtypeStruct(csc.shape[1:], F32),
                   jax.ShapeDtypeStruct((DEC_BATCH, nv), F32)],
        scratch_shapes=[pltpu.VMEM((DEC_SEQ * bblk, D_GROUP), BF16)],
        compiler_params=_params("parallel"),
        name="mixer_sample",
    )(z, cb, cp, csc, ag, aw4, ab4, bw, bb, gng, gnb, clin, cs, dw)


def _stage_a(y, h, wout_ref, gpost_ref, gpre_ref, wxq_ref):
    h1 = h + _rms(_dot(y, wout_ref[...]), gpost_ref[...])
    q = _dot(_rms(h1, gpre_ref[...]).astype(BF16), wxq_ref[...])
    return h1, q


def _stage_c(o, h1, wxo_ref, gpost_ref, gffn_ref):
    h2 = h1 + _rms(_dot(o, wxo_ref[...]), gpost_ref[...])
    return h2, _rms(h2, gffn_ref[...]).astype(BF16)


def _softmax_rows(s):
    e = jnp.exp(s - jnp.max(s, axis=-1, keepdims=True))
    return e / jnp.sum(e, axis=-1, keepdims=True)


_NT = (((1,), (1,)), ((), ()))


def _mid_prompt_kernel(y_ref, h_ref, wout_ref, gmp_ref, gxp_ref, wxq_ref, mk_ref, mv_ref, wxo_ref,
                       gxo_ref, gffn_ref, h2_ref, xn_ref):
    k = mk_ref[...].astype(BF16)
    v = mv_ref[...].astype(BF16)
    halves = [slice(r * SUB_MID, (r + 1) * SUB_MID) for r in range(y_ref.shape[0] // SUB_MID)]

    def out_proj(rs):
        return _dot(y_ref[rs, :], wout_ref[...])

    def residual_and_query_in(rs, mix):
        h1 = h_ref[rs, :] + _rms(mix, gmp_ref[...])
        return h1, _rms(h1, gxp_ref[...]).astype(BF16)

    def attention(xq):
        q = _dot(xq, wxq_ref[...]).astype(BF16)
        heads = []
        for hd in range(N_XHEADS):
            ls = slice(hd * D_XHEAD, (hd + 1) * D_XHEAD)
            s = lax.dot_general(q[:, ls], k[:, ls], _NT, preferred_element_type=F32) * (D_XHEAD ** -0.5)
            heads.append(_dot(_softmax_rows(s).astype(BF16), v[:, ls]))
        return _dot(jnp.concatenate(heads, axis=-1).astype(BF16), wxo_ref[...])

    def finish(rs, h1, xa):
        h2 = h1 + _rms(xa, gxo_ref[...])
        h2_ref[rs, :] = h2
        xn_ref[rs, :] = _rms(h2, gffn_ref[...]).astype(BF16)

    mix = [out_proj(rs) for rs in halves]
    state = [residual_and_query_in(halves[0], mix[0])]
    xa = []
    for r, rs in enumerate(halves):
        xa.append(attention(state[r][1]))
        if r + 1 < len(halves):
            state.append(residual_and_query_in(halves[r + 1], mix[r + 1]))
        if r > 0:
            finish(halves[r - 1], state[r - 1][0], xa[r - 1])
    finish(halves[-1], state[-1][0], xa[-1])


def _resident(w):
    return pl.BlockSpec(w.shape, lambda i: (0, 0), pipeline_mode=pl.Buffered(1))


def mid_prompt(y, h, wout, gmp, gxp, wxq, kv, wxo, gxo, gffn, l):
    m = y.shape[0]
    tm = TM
    per_seq = SEQ // tm
    rows = lambda c: pl.BlockSpec((tm, c), lambda i: (i, 0))
    full = lambda a: _layer_param(a, l)
    mem = lambda half: pl.BlockSpec((None, N_MEM, D_X), lambda i: (l, i // per_seq, half))
    return pl.pallas_call(
        _mid_prompt_kernel,
        grid=(m // tm,),
        in_specs=[rows(D_MODEL), rows(D_MODEL), _resident(wout), full(gmp), full(gxp),
                  _resident(wxq), mem(0), mem(1), _resident(wxo), full(gxo), full(gffn)],
        out_specs=[rows(D_MODEL), rows(D_MODEL)],
        out_shape=[jax.ShapeDtypeStruct((m, D_MODEL), F32), jax.ShapeDtypeStruct((m, D_MODEL), BF16)],
        compiler_params=_params("parallel"),
        name="mid_prompt",
    )(y, h, wout, gmp, gxp, wxq, kv, kv, wxo, gxo, gffn)


def _mid_a_kernel(y_ref, h_ref, wout_ref, gmp_ref, gxp_ref, wxq_ref, h1_ref, q_ref):
    h1, q = _stage_a(y_ref[...], h_ref[...], wout_ref, gmp_ref, gxp_ref, wxq_ref)
    h1_ref[...] = h1
    q_ref[...] = q.astype(BF16)


def mid_a(y, h, wout, gmp, gxp, wxq, l):
    m = y.shape[0]
    tm = TM
    rows = lambda c: pl.BlockSpec((tm, c), lambda i: (i, 0))
    full = lambda a: _layer_param(a, l)
    return pl.pallas_call(
        _mid_a_kernel,
        grid=(m // tm,),
        in_specs=[rows(D_MODEL), rows(D_MODEL), _resident(wout), full(gmp), full(gxp),
                  _resident(wxq)],
        out_specs=[rows(D_MODEL), rows(D_X)],
        out_shape=[jax.ShapeDtypeStruct((m, D_MODEL), F32), jax.ShapeDtypeStruct((m, D_X), BF16)],
        compiler_params=_params("parallel"),
        name="mid_a",
    )(y, h, wout, gmp, gxp, wxq)


def _mid_c_kernel(o_ref, h1_ref, wxo_ref, gxo_ref, gffn_ref, h2_ref, xn_ref):
    h2, xn = _stage_c(o_ref[...], h1_ref[...], wxo_ref, gxo_ref, gffn_ref)
    h2_ref[...] = h2
    xn_ref[...] = xn


def mid_c(o, h1, wxo, gxo, gffn, l):
    m = o.shape[0]
    tm = TM
    rows = lambda c: pl.BlockSpec((tm, c), lambda i: (i, 0))
    full = lambda a: _layer_param(a, l)
    return pl.pallas_call(
        _mid_c_kernel,
        grid=(m // tm,),
        in_specs=[rows(D_X), rows(D_MODEL), _resident(wxo), full(gxo), full(gffn)],
        out_specs=[rows(D_MODEL), rows(D_MODEL)],
        out_shape=[jax.ShapeDtypeStruct((m, D_MODEL), F32), jax.ShapeDtypeStruct((m, D_MODEL), BF16)],
        compiler_params=_params("parallel"),
        name="mid_c",
    )(o, h1, wxo, gxo, gffn)


def _attn_sample_kernel(q_ref, k_ref, v_ref, o_ref):
    bblk = q_ref.shape[0]
    lane_head = lax.broadcasted_iota(jnp.int32, (V7X_SUBLANES, D_X), 1) // D_XHEAD

    def scores(b):
        q8 = q_ref[b]
        qbd = jnp.concatenate([jnp.where(lane_head == hd, q8, jnp.zeros_like(q8)) for hd in range(N_XHEADS)], axis=0)
        return lax.dot_general(qbd, k_ref[b].astype(BF16), _NT, preferred_element_type=F32) * (D_XHEAD ** -0.5)

    def attend(b, s):
        of = _dot(_softmax_rows(s).astype(BF16), v_ref[b].astype(BF16))
        o8 = jnp.zeros((V7X_SUBLANES, D_X), F32)
        for hd in range(N_XHEADS):
            o8 = o8 + jnp.where(lane_head == hd, of[hd * V7X_SUBLANES:(hd + 1) * V7X_SUBLANES], 0.0)
        o_ref[b] = o8.astype(BF16)

    s = scores(0)
    for b in range(1, bblk):
        s_next = scores(b)
        attend(b - 1, s)
        s = s_next
    attend(bblk - 1, s)


def attn_sample(q, k, v, l):
    bblk = BB_ATT
    qs = pl.BlockSpec((bblk, V7X_SUBLANES, D_X), lambda b: (b, 0, 0))
    ms = pl.BlockSpec((None, bblk, N_MEM, D_X), lambda b: (l, b, 0, 0))
    return pl.pallas_call(
        _attn_sample_kernel,
        grid=(DEC_BATCH // bblk,),
        in_specs=[qs, ms, ms],
        out_specs=qs,
        out_shape=jax.ShapeDtypeStruct(q.shape, BF16),
        compiler_params=_params("parallel"),
        name="attn_sample",
    )(q, k, v)


def _gate(cg, cu):
    return (cg * _sigmoid(cg) * cu).astype(BF16)


def _shift_conv3(prev, h, fw_ref):
    rows = h.shape[0]
    ext = jnp.concatenate([prev, h], axis=0)
    return (fw_ref[0:1, :] * ext[HIST_F - 2:HIST_F - 2 + rows]
            + fw_ref[1:2, :] * ext[HIST_F - 1:HIST_F - 1 + rows]
            + fw_ref[2:3, :] * h)


def _up_prompt_kernel(*refs, per_seq, subs, cast_next):
    n_cast = N_CAST if cast_next else 0
    xn_ref, wg_ref, wu_ref, fwg_ref, fwu_ref = refs[:5]
    src = refs[5:5 + n_cast]
    o_ref, tg_ref, tu_ref = refs[5 + n_cast:8 + n_cast]
    dst = refs[8 + n_cast:8 + 2 * n_cast]
    carg, caru = refs[8 + 2 * n_cast:]
    i = pl.program_id(0)
    j = pl.program_id(1)

    @pl.when(i % per_seq == 0)
    def _():
        carg[j] = jnp.zeros(carg.shape[1:], F32)
        caru[j] = jnp.zeros(caru.shape[1:], F32)

    if cast_next:
        _cast_chunk(jnp.minimum(i * pl.num_programs(1) + j, CAST_CHUNKS - 1), src, dst)

    prev_g = carg[j]
    prev_u = caru[j]
    r0 = 0
    for sub in subs:
        rs = slice(r0, r0 + sub)
        r0 += sub
        hg = _dot(xn_ref[rs, :], wg_ref[...])
        hu = _dot(xn_ref[rs, :], wu_ref[...])
        o_ref[rs, :] = _gate(_shift_conv3(prev_g, hg, fwg_ref), _shift_conv3(prev_u, hu, fwu_ref))
        prev_g = hg[sub - HIST_F:]
        prev_u = hu[sub - HIST_F:]
    carg[j] = prev_g
    caru[j] = prev_u
    tg_ref[...] = prev_g
    tu_ref[...] = prev_u


def up_prompt(xn, wg, wu, fwg, fwu, l, next_sources=None):
    m, k = xn.shape
    tm, tn = TM_UP, TN_UP
    assert sum(SUBS_UP) == tm
    nj = FF_PAD // tn
    assert (m // tm) * nj >= CAST_CHUNKS
    wcol = pl.BlockSpec((k, tn), lambda i, j: (0, j))
    fcol = pl.BlockSpec((None, HIST_F, tn), lambda i, j: (l, 0, j))
    tail = pl.BlockSpec((HIST_F, tn), lambda i, j: (i, j))
    cast_in, cast_out, cast_shape = ([], [], [])
    if next_sources is not None:
        cast_in, cast_out, cast_shape = _cast_specs(l + 1, lambda i, j: jnp.minimum(i * nj + j, CAST_CHUNKS - 1))
    res = pl.pallas_call(
        functools.partial(_up_prompt_kernel, per_seq=SEQ // tm, subs=SUBS_UP, cast_next=next_sources is not None),
        grid=(m // tm, nj),
        in_specs=[pl.BlockSpec((tm, k), lambda i, j: (i, 0)), wcol, wcol, fcol, fcol] + cast_in,
        out_specs=[pl.BlockSpec((tm, tn), lambda i, j: (i, j)), tail, tail] + cast_out,
        out_shape=[jax.ShapeDtypeStruct((m, FF_PAD), BF16),
                   jax.ShapeDtypeStruct((m // tm * HIST_F, FF_PAD), F32),
                   jax.ShapeDtypeStruct((m // tm * HIST_F, FF_PAD), F32)] + cast_shape,
        scratch_shapes=[pltpu.VMEM((nj, HIST_F, tn), F32), pltpu.VMEM((nj, HIST_F, tn), F32)],
        compiler_params=_params("arbitrary", "arbitrary"),
        name="up_prompt",
    )(xn, wg, wu, fwg, fwu, *(next_sources or ()))
    return res[0], res[1], res[2], tuple(res[3:])


def _down_kernel(g_ref, w_ref, h_ref, gn_ref, o_ref, acc):
    j = pl.program_id(1)
    tn = w_ref.shape[1]
    acc[:, pl.ds(pl.multiple_of(j * tn, tn), tn)] = _dot(g_ref[...], w_ref[...])

    @pl.when(j == pl.num_programs(1) - 1)
    def _():
        o_ref[...] = h_ref[...] + _rms(acc[...], gn_ref[...])


def down(g, w, h, gn, l):
    m, k = g.shape
    tm, tn = TM, TN_DOWN
    return pl.pallas_call(
        _down_kernel,
        grid=(m // tm, D_MODEL // tn),
        in_specs=[pl.BlockSpec((tm, k), lambda i, j: (i, 0)),
                  pl.BlockSpec((k, tn), lambda i, j: (0, j)),
                  pl.BlockSpec((tm, D_MODEL), lambda i, j: (i, 0)),
                  _layer_param(gn, l)],
        out_specs=pl.BlockSpec((tm, D_MODEL), lambda i, j: (i, 0)),
        out_shape=jax.ShapeDtypeStruct((m, D_MODEL), F32),
        scratch_shapes=[pltpu.VMEM((tm, D_MODEL), F32)],
        compiler_params=_params("parallel", "arbitrary"),
        name="down",
    )(g, w, h, gn)


def _ffn_sample_kernel(xn_ref, wg_ref, wu_ref, fwg_ref, fwu_ref, p0g_ref, p1g_ref, p0u_ref, p1u_ref,
                       wd_ref, h_ref, gn_ref, o_ref, tg_ref, tu_ref):
    nb = DEC_BATCH
    j = pl.program_id(0)

    @pl.when(j == 0)
    def _():
        o_ref[...] = jnp.zeros(o_ref.shape, F32)

    xn = xn_ref[...]
    hg = _dot(xn, wg_ref[...])
    hu = _dot(xn, wu_ref[...])
    tg_ref[...] = hg[(DEC_SEQ - 2) * nb:, :]
    tu_ref[...] = hu[(DEC_SEQ - 2) * nb:, :]
    ext_g = [p0g_ref[...], p1g_ref[...]] + [hg[t * nb:(t + 1) * nb] for t in range(DEC_SEQ)]
    ext_u = [p0u_ref[...], p1u_ref[...]] + [hu[t * nb:(t + 1) * nb] for t in range(DEC_SEQ)]
    for t in range(DEC_SEQ):
        cg = fwg_ref[0:1, :] * ext_g[t] + fwg_ref[1:2, :] * ext_g[t + 1] + fwg_ref[2:3, :] * ext_g[t + 2]
        cu = fwu_ref[0:1, :] * ext_u[t] + fwu_ref[1:2, :] * ext_u[t + 1] + fwu_ref[2:3, :] * ext_u[t + 2]
        ts = slice(t * nb, (t + 1) * nb)
        o_ref[ts, :] += _dot(_gate(cg, cu), wd_ref[...])

    @pl.when(j == pl.num_programs(0) - 1)
    def _():
        o_ref[...] = h_ref[...] + _rms(o_ref[...], gn_ref[...])


def ffn_sample(xn, wg, wu, fwg, fwu, prev, wd, h, gn, l):
    m, k = xn.shape
    tn = TN_UP
    nj = FF_PAD // tn
    col = lambda r: pl.BlockSpec((r, tn), lambda j: (0, j))
    wcol = pl.BlockSpec((k, tn), lambda j: (0, j))
    fcol = pl.BlockSpec((None, HIST_F, tn), lambda j: (l, 0, j))
    prev_row = lambda r, half: pl.BlockSpec((None, None, None, DEC_BATCH, tn), lambda j: (l, r, half, 0, j))
    whole = lambda r, c: pl.BlockSpec((r, c), lambda j: (0, 0))
    nt = (FFN_CONV_WIDTH - 1) * DEC_BATCH
    return pl.pallas_call(
        _ffn_sample_kernel,
        grid=(nj,),
        in_specs=[whole(m, k), wcol, wcol, fcol, fcol, prev_row(0, 0), prev_row(1, 0), prev_row(0, 1), prev_row(1, 1),
                  pl.BlockSpec((tn, D_MODEL), lambda j: (j, 0)), whole(m, D_MODEL), _layer_param(gn, l)],
        out_specs=[whole(m, D_MODEL), col(nt), col(nt)],
        out_shape=[jax.ShapeDtypeStruct((m, D_MODEL), F32),
                   jax.ShapeDtypeStruct((nt, FF_PAD), F32),
                   jax.ShapeDtypeStruct((nt, FF_PAD), F32)],
        compiler_params=_params("arbitrary"),
        name="ffn_sample",
    )(xn, wg, wu, fwg, fwu, prev, prev, prev, prev, wd, h, gn)


def kernel(x_prompt, x_sample, cache_mem_k, cache_mem_v, state_conv_b, state_pool, state_sconv, state_ffn_conv, mem_prompt, g_mix_pre, g_mix_post, g_mem, g_x_pre, g_x_post, g_ffn_pre, g_ffn_post, w_in, w_out, a_norm_g, a_ws, a_bs, b_conv_w, b_conv_b, b_gn_g, b_gn_b, c_lin, c_scale, d_conv_w, w_xq, w_xk, w_xv, w_xo, w_up, f_conv_w, w_down):
    nb, ns = DEC_BATCH, DEC_SEQ
    w_sources = (w_in, w_out, w_xq, w_xo, w_up, w_up, w_down)
    w_kv_b = cast_weight_pair(w_xk, w_xv)
    layer_w = cast_layer(w_sources, 0)

    hp = x_prompt.reshape(BATCH * SEQ, D_MODEL)
    hs = jnp.transpose(x_sample, (1, 0, 2)).reshape(ns * nb, D_MODEL)
    mem = mem_prompt.reshape(BATCH * N_MEM, D_MODEL)
    st_b = state_conv_b.reshape(DEPTH, nb, -1)
    st_p = state_pool.reshape(DEPTH, nb, -1)
    st_s = state_sconv.reshape(DEPTH, nb, -1)

    row = lambda a: a.reshape(DEPTH, 1, -1)
    ag, bb, gng, gnb, cs = (row(a) for a in (a_norm_g, b_conv_b, b_gn_g, b_gn_b, c_scale))
    gpre, gmp, gxp, gxo, gfp, gfo = (row(a) for a in (g_mix_pre, g_mix_post, g_x_pre, g_x_post, g_ffn_pre,
                                                      g_ffn_post))
    abt = jnp.transpose(a_bs, (0, 2, 1))
    pad_rows = lambda a, rows: jnp.pad(a, ((0, 0), (0, rows - a.shape[1]), (0, 0)))
    bw = pad_rows(b_conv_w, HIST_B)
    dw = pad_rows(d_conv_w, HIST_D)
    fw = jnp.pad(f_conv_w.reshape(DEPTH, FFN_CONV_WIDTH, 2, D_FF),
                 ((0, 0), (0, HIST_F - FFN_CONV_WIDTH), (0, 0), (0, FF_PAD - D_FF)))
    fwg, fwu = fw[:, :, 0], fw[:, :, 1]
    aw4 = jnp.repeat(jnp.transpose(a_ws[:, :, :ns, :ns], (0, 2, 3, 1)).reshape(DEPTH, ns * ns, N_SUB), D_SUB, axis=2)
    ab4 = jnp.repeat(jnp.transpose(a_bs[:, :, :ns], (0, 2, 1)), D_SUB, axis=2)
    keep = FFN_CONV_WIDTH - 1
    prev_f = jnp.pad(jnp.transpose(state_ffn_conv.reshape(DEPTH, nb, keep, 2, D_FF), (0, 2, 3, 1, 4)),
                     ((0, 0),) * 4 + ((0, FF_PAD - D_FF),))

    kv = norm_matmul_layers(mem, row(g_mem), w_kv_b, TM, D_X)

    p_states, s_states = [], []
    for l in range(DEPTH):
        w_in_b, w_out_b, w_xq_b, w_xo_b, w_g_b, w_u_b, w_down_b = layer_w
        y, nbp, npp, nsp = inmix_prompt(hp.reshape(BATCH, SEQ, D_MODEL), gpre, w_in_b, ag, a_ws, abt, bw, bb,
                                        gng, gnb, c_lin, cs, dw, l)
        h2, xn = mid_prompt(y.reshape(BATCH * SEQ, D_MODEL), hp, w_out_b, gmp, gxp, w_xq_b, kv, w_xo_b, gxo, gfp, l)
        gate, tg, tu, layer_w = up_prompt(xn, w_g_b, w_u_b, fwg, fwu, l, w_sources if l + 1 < DEPTH else None)
        hp = down(gate, w_down_b, h2, gfo, l)
        p_states.append((nbp, npp, nsp, tg, tu))

        zs = norm_matmul(hs, gpre, w_in_b, l, TM, TN_IN)
        ys, nbs, nps, nss, vs = mixer_sample(zs.reshape(ns, nb, D_IN), st_b, st_p, st_s, ag, aw4, ab4, bw, bb,
                                             gng, gnb, c_lin, cs, dw, l)
        h1s, qs = mid_a(ys.reshape(ns * nb, D_MODEL), hs, w_out_b, gmp, gxp, w_xq_b, l)
        q8 = jnp.pad(jnp.transpose(qs.reshape(ns, nb, D_X), (1, 0, 2)), ((0, 0), (0, V7X_SUBLANES - ns), (0, 0)))
        o8 = attn_sample(q8, cache_mem_k, cache_mem_v, l)
        os_ = jnp.transpose(o8[:, :ns], (1, 0, 2)).reshape(ns * nb, D_X)
        h2s, xns = mid_c(os_, h1s, w_xo_b, gxo, gfp, l)
        hs, tgs, tus = ffn_sample(xns, w_g_b, w_u_b, fwg, fwu, prev_f, w_down_b, h2s, gfo, l)
        s_states.append((nbs, nps, nss, tgs, tus, vs))

    nbp, npp, nsp, tg, tu = (jnp.stack(a) for a in zip(*p_states))
    nbs, nps, nss, tgs, tus, vs = (jnp.stack(a) for a in zip(*s_states))
    last = lambda a: a.reshape(DEPTH, BATCH, -1, HIST_F, FF_PAD)[:, :, -1, HIST_F - keep:, :D_FF]
    bmajor = lambda a: jnp.transpose(a.reshape(DEPTH, keep, nb, FF_PAD)[..., :D_FF], (0, 2, 1, 3))
    return (hp.reshape(BATCH, SEQ, D_MODEL),
            jnp.transpose(hs.reshape(ns, nb, D_MODEL), (1, 0, 2)),
            kv[:, :, :D_X].reshape(DEPTH, BATCH, N_MEM, D_X),
            kv[:, :, D_X:].reshape(DEPTH, BATCH, N_MEM, D_X),
            nbp[:, :, HIST_B - (CONV_B_WIDTH - 1):],
            npp[:, :, HIST_C - POOL_PREV:],
            nsp[:, :, HIST_D - (SCONV_WIDTH - 1):],
            jnp.concatenate([last(tg), last(tu)], axis=-1),
            nbs.reshape(DEPTH, nb, CONV_B_WIDTH - 1, D_GROUP),
            nps.reshape(DEPTH, nb, POOL_PREV, D_GROUP),
            nss.reshape(DEPTH, nb, SCONV_WIDTH - 1, D_GROUP),
            jnp.concatenate([bmajor(tgs), bmajor(tus)], axis=-1),
            vs.reshape(DEPTH, nb, ns, D_GROUP))
```

```python
import functools

import jax
import jax.numpy as jnp
from jax import lax
from jax.experimental import pallas as pl
from jax.experimental.pallas import tpu as pltpu

F32 = jnp.float32
BF16 = jnp.bfloat16

D_MODEL = 2048
BATCH = 4
SEQ = 2048
DEPTH = 4
DEC_BATCH = 128
DEC_SEQ = 4
PAST_LEN = 16384
D_GROUP = 512
N_SUB = 4
D_SUB = 128
D_IN = 8 * D_GROUP
CHUNK = 128
CONV_B_WIDTH = 31
POOL_WINDOWS = (2, 4, 8, 16)
POOL_PREV = 15
SCONV_WIDTH = 3
FFN_CONV_WIDTH = 3
D_FF = 5504
N_MEM = 256
N_XHEADS = 4
D_XHEAD = 128
D_X = 512
EPS = 1e-6

V7X_SUBLANES = 8
V7X_LANES = 128
V7X_VMEM_LIMIT_BYTES = 56 * 1024 * 1024

HIST_B = 32
HIST_C = 16
HIST_D = 8
HIST_F = 8

TM = 512
TM_UP = 2048
SUBS_UP = (512, 512, 512, 512)
TM_IN = 1024
SUB_MID = 256
TT = 256
ROWS = 32
TN_IN = 1024
TN_UP = 512
FF_PAD = -(-D_FF // TN_UP) * TN_UP
TN_DOWN = 512
CAST_CHUNKS = 32
BB_MIX = 32
BB_ATT = 8


def _layer_param(a, l):
    tail = a.shape[1:]
    return pl.BlockSpec((None,) + tail, lambda *_: (l,) + (0,) * len(tail))


def _params(*sem):
    return pltpu.CompilerParams(dimension_semantics=sem, vmem_limit_bytes=V7X_VMEM_LIMIT_BYTES)


def _rms(x, g):
    return x * lax.rsqrt(jnp.mean(x * x, axis=-1, keepdims=True) + EPS) * g


def _sigmoid(x):
    return 1.0 / (1.0 + jnp.exp(-x))


def _dot(a, b):
    return jnp.dot(a, b, preferred_element_type=F32)


N_CAST = 7


def _cast_specs(layer, chunk):
    def pair(rows_total, cols, col_block=0, out_rows=None, out_cols=None):
        rows = (out_rows or rows_total) // CAST_CHUNKS
        src = pl.BlockSpec((None, rows, cols), lambda *g: (layer, chunk(*g), col_block))
        dst = pl.BlockSpec((rows, out_cols or cols), lambda *g: (chunk(*g), 0))
        return src, dst, jax.ShapeDtypeStruct((out_rows or rows_total, out_cols or cols), BF16)
    table = [pair(D_MODEL, D_IN), pair(D_MODEL, D_MODEL), pair(D_MODEL, D_X), pair(D_X, D_MODEL),
             pair(D_MODEL, D_FF, 0, out_cols=FF_PAD), pair(D_MODEL, D_FF, 1, out_cols=FF_PAD),
             pair(D_FF, D_MODEL, out_rows=FF_PAD)]
    return tuple(list(t) for t in zip(*table))


def _cast_chunk(c, src, dst):
    for s, d in zip(src[:4], dst[:4]):
        d[...] = s[...].astype(BF16)
    for s, d in zip(src[4:6], dst[4:6]):
        d[:, 0:D_FF] = s[...].astype(BF16)
        d[:, D_FF:] = jnp.zeros((d.shape[0], d.shape[1] - D_FF), BF16)
    rows = dst[6].shape[0]
    row = c * rows + lax.broadcasted_iota(jnp.int32, dst[6].shape, 0)
    dst[6][...] = jnp.where(row < D_FF, src[6][...], 0.0).astype(BF16)


def _cast_layer_kernel(*refs):
    _cast_chunk(pl.program_id(0), refs[:N_CAST], refs[N_CAST:])


def cast_layer(sources, layer):
    in_specs, out_specs, out_shape = _cast_specs(layer, lambda c: c)
    return pl.pallas_call(
        _cast_layer_kernel,
        grid=(CAST_CHUNKS,),
        in_specs=in_specs,
        out_specs=out_specs,
        out_shape=out_shape,
        compiler_params=_params("parallel"),
        name="cast_layer",
    )(*sources)


def _cast_pair_kernel(a_ref, b_ref, o_ref):
    n = a_ref.shape[-1]
    o_ref[:, 0:n] = a_ref[...].astype(BF16)
    o_ref[:, n:2 * n] = b_ref[...].astype(BF16)


def cast_weight_pair(a, b):
    depth, r, c = a.shape
    spec = pl.BlockSpec((None, r, c), lambda l: (l, 0, 0))
    return pl.pallas_call(
        _cast_pair_kernel,
        grid=(depth,),
        in_specs=[spec, spec],
        out_specs=pl.BlockSpec((None, r, 2 * c), lambda l: (l, 0, 0)),
        out_shape=jax.ShapeDtypeStruct((depth, r, 2 * c), BF16),
        compiler_params=_params("parallel"),
        name="cast_weight_pair",
    )(a, b)


def _norm_matmul_kernel(x_ref, g_ref, w_ref, o_ref, xn_ref):
    @pl.when(pl.program_id(1) == 0)
    def _():
        xn_ref[...] = _rms(x_ref[...], g_ref[...]).astype(BF16)

    o_ref[...] = _dot(xn_ref[...], w_ref[...])


def norm_matmul(x, g, w, l, tm, tn):
    m, k = x.shape
    n = w.shape[1]
    return pl.pallas_call(
        _norm_matmul_kernel,
        grid=(m // tm, n // tn),
        in_specs=[pl.BlockSpec((tm, k), lambda i, j: (i, 0)),
                  _layer_param(g, l),
                  pl.BlockSpec((k, tn), lambda i, j: (0, j))],
        out_specs=pl.BlockSpec((tm, tn), lambda i, j: (i, j)),
        out_shape=jax.ShapeDtypeStruct((m, n), F32),
        scratch_shapes=[pltpu.VMEM((tm, k), BF16)],
        compiler_params=_params("parallel", "arbitrary"),
        name="norm_matmul",
    )(x, g, w)


def _memory_kv_kernel(x_ref, g_ref, w_ref, k_ref, v_ref):
    kv = _dot(_rms(x_ref[...], g_ref[...]).astype(BF16), w_ref[...])
    n = k_ref.shape[-1]
    k_ref[...] = kv[:, 0:n]
    v_ref[...] = kv[:, n:2 * n]


def memory_kv(x, g, w, tm):
    m, k = x.shape
    depth, _, n2 = w.shape
    half = pl.BlockSpec((None, tm, n2 // 2), lambda l, i: (l, i, 0))
    shape = jax.ShapeDtypeStruct((depth, m, n2 // 2), F32)
    return pl.pallas_call(
        _memory_kv_kernel,
        grid=(depth, m // tm),
        in_specs=[pl.BlockSpec((tm, k), lambda l, i: (i, 0)),
                  pl.BlockSpec((None, 1, k), lambda l, i: (l, 0, 0)),
                  pl.BlockSpec((None, k, n2), lambda l, i: (l, 0, 0))],
        out_specs=[half, half],
        out_shape=[shape, shape],
        compiler_params=_params("parallel", "parallel"),
        name="memory_kv",
    )(x, g, w)


def _layer_norm(v, g):
    mu = jnp.mean(v, axis=-1, keepdims=True)
    d = v - mu
    var = jnp.mean(d * d, axis=-1, keepdims=True)
    return d * lax.rsqrt(var + EPS) * g


def _group_norm_silu(y, g, b):
    mu = jnp.mean(y, axis=-1, keepdims=True)
    d = y - mu
    var = jnp.mean(d * d, axis=-1, keepdims=True)
    yn = d * lax.rsqrt(var + EPS) * g + b
    return yn * _sigmoid(yn)


def _inmix_prompt_kernel(x_ref, gpre_ref, win_ref, ag_ref, aws_ref, abt_ref, bw_ref, bb_ref, gng_ref, gnb_ref,
                         clin_ref, cs_ref, dw_ref,
                         y_ref, nb_ref, np_ref, ns_ref,
                         extb, extc, extd, pooled):
    t = pl.program_id(1)
    tt = y_ref.shape[0]
    quarter = 2 * D_GROUP

    @pl.when(t == 0)
    def _():
        for s in range(V7X_SUBLANES):
            extb[s, 0:HIST_B, :] = jnp.zeros((HIST_B, D_GROUP), F32)
            extb[s, tt + HIST_B - V7X_SUBLANES:tt + HIST_B, :] = jnp.zeros((V7X_SUBLANES, D_GROUP), F32)
        extc[0:HIST_C, :] = jnp.zeros((HIST_C, D_GROUP), F32)
        extd[0:HIST_D, :] = jnp.zeros((HIST_D, D_GROUP), F32)

    xn = _rms(x_ref[...], gpre_ref[...]).astype(BF16)
    z = {}
    proj = lambda q: _dot(xn, win_ref[:, q * quarter:(q + 1) * quarter])

    def zcol(k, rs=slice(None)):
        return z[k // 2][rs, (k % 2) * D_GROUP:(k % 2 + 1) * D_GROUP]

    z[1] = proj(1)
    hb = zcol(2) * _sigmoid(zcol(3))
    for s in range(V7X_SUBLANES):
        extb[s, HIST_B - s:HIST_B - s + tt, :] = hb
    n_chunks = tt // ROWS
    later = {0: 0, n_chunks // 3: 2, 2 * n_chunks // 3: 3}
    for c in range(n_chunks):
        if c in later:
            z[later[c]] = proj(later[c])
        for g in range(N_SUB):
            ls = slice(g * D_SUB, (g + 1) * D_SUB)
            acc = jnp.zeros((ROWS, D_SUB), F32)
            for k in range(CONV_B_WIDTH):
                tiles, s = divmod(HIST_B - (CONV_B_WIDTH - 1) + k, V7X_SUBLANES)
                r0 = c * ROWS + tiles * V7X_SUBLANES
                acc = acc + bw_ref[k:k + 1, ls] * extb[s, r0:r0 + ROWS, ls]
            yb = _group_norm_silu(acc + bb_ref[:, ls], gng_ref[:, ls], gnb_ref[:, ls])
            y_ref[c * ROWS:(c + 1) * ROWS, D_GROUP + g * D_SUB:D_GROUP + (g + 1) * D_SUB] = yb.astype(BF16)

    vn = _layer_norm(zcol(1), ag_ref[...]).astype(BF16)
    row = lax.broadcasted_iota(jnp.int32, (CHUNK, CHUNK), 0)
    col = lax.broadcasted_iota(jnp.int32, (CHUNK, CHUNK), 1)
    for h in range(N_SUB):
        wm = jnp.where(row >= col, aws_ref[h], 0.0).astype(BF16)
        bias = abt_ref[:, h:h + 1]
        for c in range(tt // CHUNK):
            rs = slice(c * CHUNK, (c + 1) * CHUNK)
            ls = slice(h * D_SUB, (h + 1) * D_SUB)
            zz = _dot(wm, vn[rs, ls]) + bias
            y_ref[rs, ls] = (z[0][rs, ls] * zz).astype(BF16)

    extc[HIST_C:HIST_C + tt, :] = zcol(4)
    for c in range(tt // ROWS):
        pos = t * tt + c * ROWS + lax.broadcasted_iota(jnp.int32, (ROWS, 1), 0)
        for g, win in enumerate(POOL_WINDOWS):
            ls = slice(g * D_SUB, (g + 1) * D_SUB)
            r0 = c * ROWS + HIST_C
            x = extc[r0:r0 + ROWS, ls]
            s = x
            for i in range(1, win):
                s = s + extc[r0 - i:r0 - i + ROWS, ls]
            cnt = jnp.minimum(pos + 1, win).astype(F32)
            pooled[c * ROWS:(c + 1) * ROWS, ls] = (s / cnt - x).astype(BF16)
    for g in range(N_SUB):
        ls = slice(g * D_SUB, (g + 1) * D_SUB)
        yc = _dot(pooled[:, ls], clin_ref[g].astype(BF16)) * cs_ref[:, ls]
        y_ref[:, 2 * D_GROUP + g * D_SUB:2 * D_GROUP + (g + 1) * D_SUB] = yc.astype(BF16)

    extd[HIST_D:HIST_D + tt, :] = zcol(7) * zcol(5)
    for c in range(tt // ROWS):
        r0 = c * ROWS + HIST_D
        conv = (dw_ref[0:1, :] * extd[r0 - 2:r0 - 2 + ROWS, :]
                + dw_ref[1:2, :] * extd[r0 - 1:r0 - 1 + ROWS, :]
                + dw_ref[2:3, :] * extd[r0:r0 + ROWS, :])
        rs = slice(c * ROWS, (c + 1) * ROWS)
        y_ref[rs, 3 * D_GROUP:4 * D_GROUP] = (zcol(6, rs) * conv).astype(BF16)

    for s in range(V7X_SUBLANES):
        extb[s, 0:HIST_B, :] = extb[s, tt:tt + HIST_B, :]
    extc[0:HIST_C, :] = extc[tt:tt + HIST_C, :]
    extd[0:HIST_D, :] = extd[tt:tt + HIST_D, :]

    @pl.when(t == pl.num_programs(1) - 1)
    def _():
        nb_ref[...] = extb[0, 0:HIST_B, :]
        np_ref[...] = extc[0:HIST_C, :]
        ns_ref[...] = extd[0:HIST_D, :]


def inmix_prompt(x, gpre, win, ag, aws, abt, bw, bb, gng, gnb, clin, cs, dw, l):
    tt = TT
    full = lambda a: _layer_param(a, l)
    hist = lambda r: pl.BlockSpec((None, r, D_GROUP), lambda b, t: (b, 0, 0))
    return pl.pallas_call(
        _inmix_prompt_kernel,
        grid=(BATCH, SEQ // tt),
        in_specs=[pl.BlockSpec((None, tt, D_MODEL), lambda b, t: (b, t, 0)),
                  full(gpre),
                  pl.BlockSpec((D_MODEL, D_IN), lambda b, t: (0, 0), pipeline_mode=pl.Buffered(1)),
                  full(ag), full(aws), full(abt), full(bw), full(bb), full(gng), full(gnb),
                  full(clin), full(cs), full(dw)],
        out_specs=[pl.BlockSpec((None, tt, D_MODEL), lambda b, t: (b, t, 0)),
                   hist(HIST_B), hist(HIST_C), hist(HIST_D)],
        out_shape=[jax.ShapeDtypeStruct((BATCH, SEQ, D_MODEL), BF16),
                   jax.ShapeDtypeStruct((BATCH, HIST_B, D_GROUP), F32),
                   jax.ShapeDtypeStruct((BATCH, HIST_C, D_GROUP), F32),
                   jax.ShapeDtypeStruct((BATCH, HIST_D, D_GROUP), F32)],
        scratch_shapes=[pltpu.VMEM((V7X_SUBLANES, HIST_B + tt, D_GROUP), F32),
                        pltpu.VMEM((HIST_C + tt, D_GROUP), F32),
                        pltpu.VMEM((HIST_D + tt, D_GROUP), F32),
                        pltpu.VMEM((tt, D_GROUP), BF16)],
        compiler_params=_params("parallel", "arbitrary"),
        name="inmix_prompt",
    )(x, gpre, win, ag, aws, abt, bw, bb, gng, gnb, clin, cs, dw)


def _mixer_sample_kernel(z_ref, cb_ref, cp_ref, csc_ref, ag_ref, aw4_ref, ab4_ref, bw_ref, bb_ref,
                         gng_ref, gnb_ref, clin_ref, cs_ref, dw_ref,
                         y_ref, nb_ref, np_ref, nsc_ref, v_ref, pooled):
    bblk = z_ref.shape[1]
    blk = lambda r: slice(r * D_GROUP, (r + 1) * D_GROUP)

    vn = [_layer_norm(z_ref[t, :, D_GROUP:2 * D_GROUP], ag_ref[...]) for t in range(DEC_SEQ)]
    for i in range(DEC_SEQ):
        v_ref[:, blk(i)] = vn[i]
        zz = ab4_ref[i:i + 1, :]
        for j in range(i + 1):
            zz = zz + aw4_ref[i * DEC_SEQ + j:i * DEC_SEQ + j + 1, :] * vn[j]
        y_ref[i, :, 0:D_GROUP] = (z_ref[i, :, 0:D_GROUP] * zz).astype(BF16)

    nprev = CONV_B_WIDTH - 1
    hb = [z_ref[t, :, 2 * D_GROUP:3 * D_GROUP] * _sigmoid(z_ref[t, :, 3 * D_GROUP:4 * D_GROUP])
          for t in range(DEC_SEQ)]
    ext_b = lambda r: cb_ref[:, blk(r)] if r < nprev else hb[r - nprev]
    for t in range(DEC_SEQ):
        acc = jnp.zeros((bblk, D_GROUP), F32)
        for k in range(CONV_B_WIDTH):
            acc = acc + bw_ref[k:k + 1, :] * ext_b(t + k)
        acc = acc + bb_ref[...]
        for g in range(N_SUB):
            ls = slice(g * D_SUB, (g + 1) * D_SUB)
            yb = _group_norm_silu(acc[:, ls], gng_ref[:, ls], gnb_ref[:, ls])
            y_ref[t, :, D_GROUP + g * D_SUB:D_GROUP + (g + 1) * D_SUB] = yb.astype(BF16)
    for r in range(nprev):
        nb_ref[:, blk(r)] = ext_b(r + DEC_SEQ)

    cx = [z_ref[t, :, 4 * D_GROUP:5 * D_GROUP] for t in range(DEC_SEQ)]
    ext_c = lambda r: cp_ref[:, blk(r)] if r < POOL_PREV else cx[r - POOL_PREV]
    for t in range(DEC_SEQ):
        for g, win in enumerate(POOL_WINDOWS):
            ls = slice(g * D_SUB, (g + 1) * D_SUB)
            s = cx[t][:, ls]
            for i in range(1, win):
                s = s + ext_c(POOL_PREV + t - i)[:, ls]
            cnt = float(min(PAST_LEN + t + 1, win))
            pooled[t * bblk:(t + 1) * bblk, ls] = (s / cnt - cx[t][:, ls]).astype(BF16)
    for g in range(N_SUB):
        ls = slice(g * D_SUB, (g + 1) * D_SUB)
        yc = _dot(pooled[:, ls], clin_ref[g].astype(BF16)) * cs_ref[:, ls]
        for t in range(DEC_SEQ):
            y_ref[t, :, 2 * D_GROUP + g * D_SUB:2 * D_GROUP + (g + 1) * D_SUB] = (
                yc[t * bblk:(t + 1) * bblk].astype(BF16))
    for r in range(POOL_PREV):
        np_ref[:, blk(r)] = ext_c(r + DEC_SEQ)

    nsp = SCONV_WIDTH - 1
    hd = [z_ref[t, :, 7 * D_GROUP:8 * D_GROUP] * z_ref[t, :, 5 * D_GROUP:6 * D_GROUP] for t in range(DEC_SEQ)]
    ext_d = lambda r: csc_ref[:, blk(r)] if r < nsp else hd[r - nsp]
    for t in range(DEC_SEQ):
        conv = dw_ref[0:1, :] * ext_d(t) + dw_ref[1:2, :] * ext_d(t + 1) + dw_ref[2:3, :] * ext_d(t + 2)
        y_ref[t, :, 3 * D_GROUP:4 * D_GROUP] = (z_ref[t, :, 6 * D_GROUP:7 * D_GROUP] * conv).astype(BF16)
    for r in range(nsp):
        nsc_ref[:, blk(r)] = ext_d(r + DEC_SEQ)


def mixer_sample(z, cb, cp, csc, ag, aw4, ab4, bw, bb, gng, gnb, clin, cs, dw, l):
    bblk = BB_MIX
    full = lambda a: _layer_param(a, l)
    st_in = lambda a: pl.BlockSpec((None, bblk, a.shape[2]), lambda b: (l, b, 0))
    st = lambda a: pl.BlockSpec((bblk, a.shape[2]), lambda b: (b, 0))
    nv = DEC_SEQ * D_GROUP
    return pl.pallas_call(
        _mixer_sample_kernel,
        grid=(DEC_BATCH // bblk,),
        in_specs=[pl.BlockSpec((DEC_SEQ, bblk, D_IN), lambda b: (0, b, 0)),
                  st_in(cb), st_in(cp), st_in(csc),
                  full(ag), full(aw4), full(ab4), full(bw), full(bb), full(gng), full(gnb),
                  full(clin), full(cs), full(dw)],
        out_specs=[pl.BlockSpec((DEC_SEQ, bblk, D_MODEL), lambda b: (0, b, 0)),
                   st(cb), st(cp), st(csc),
                   pl.BlockSpec((bblk, nv), lambda b: (b, 0))],
        out_shape=[jax.ShapeDtypeStruct((DEC_SEQ, DEC_BATCH, D_MODEL), BF16),
                   jax.ShapeDtypeStruct(cb.shape[1:], F32),
                   jax.ShapeDtypeStruct(cp.shape[1:], F32),
                   jax.ShapeDtypeStruct(csc.shape[1:], F32),
                   jax.ShapeDtypeStruct((DEC_BATCH, nv), F32)],
        scratch_shapes=[pltpu.VMEM((DEC_SEQ * bblk, D_GROUP), BF16)],
        compiler_params=_params("parallel"),
        name="mixer_sample",
    )(z, cb, cp, csc, ag, aw4, ab4, bw, bb, gng, gnb, clin, cs, dw)


def _stage_a(y, h, wout_ref, gpost_ref, gpre_ref, wxq_ref):
    h1 = h + _rms(_dot(y, wout_ref[...]), gpost_ref[...])
    q = _dot(_rms(h1, gpre_ref[...]).astype(BF16), wxq_ref[...])
    return h1, q


def _stage_c(o, h1, wxo_ref, gpost_ref, gffn_ref):
    h2 = h1 + _rms(_dot(o, wxo_ref[...]), gpost_ref[...])
    return h2, _rms(h2, gffn_ref[...]).astype(BF16)


def _softmax_rows(s):
    e = jnp.exp(s - jnp.max(s, axis=-1, keepdims=True))
    return e / jnp.sum(e, axis=-1, keepdims=True)


_NT = (((1,), (1,)), ((), ()))


def _mid_prompt_kernel(y_ref, h_ref, wout_ref, gmp_ref, gxp_ref, wxq_ref, mk_ref, mv_ref, wxo_ref,
                       gxo_ref, gffn_ref, h2_ref, xn_ref):
    k = mk_ref[...].astype(BF16)
    v = mv_ref[...].astype(BF16)
    halves = [slice(r * SUB_MID, (r + 1) * SUB_MID) for r in range(y_ref.shape[0] // SUB_MID)]

    def out_proj(rs):
        return _dot(y_ref[rs, :], wout_ref[...])

    def residual_and_query_in(rs, mix):
        h1 = h_ref[rs, :] + _rms(mix, gmp_ref[...])
        return h1, _rms(h1, gxp_ref[...]).astype(BF16)

    def attention(xq):
        q = _dot(xq, wxq_ref[...]).astype(BF16)
        heads = []
        for hd in range(N_XHEADS):
            ls = slice(hd * D_XHEAD, (hd + 1) * D_XHEAD)
            s = lax.dot_general(q[:, ls], k[:, ls], _NT, preferred_element_type=F32) * (D_XHEAD ** -0.5)
            heads.append(_dot(_softmax_rows(s).astype(BF16), v[:, ls]))
        return _dot(jnp.concatenate(heads, axis=-1).astype(BF16), wxo_ref[...])

    def finish(rs, h1, xa):
        h2 = h1 + _rms(xa, gxo_ref[...])
        h2_ref[rs, :] = h2
        xn_ref[rs, :] = _rms(h2, gffn_ref[...]).astype(BF16)

    mix = [out_proj(rs) for rs in halves]
    state = [residual_and_query_in(halves[0], mix[0])]
    xa = []
    for r, rs in enumerate(halves):
        xa.append(attention(state[r][1]))
        if r + 1 < len(halves):
            state.append(residual_and_query_in(halves[r + 1], mix[r + 1]))
        if r > 0:
            finish(halves[r - 1], state[r - 1][0], xa[r - 1])
    finish(halves[-1], state[-1][0], xa[-1])


def _resident(w):
    return pl.BlockSpec(w.shape, lambda i: (0, 0), pipeline_mode=pl.Buffered(1))


def mid_prompt(y, h, wout, gmp, gxp, wxq, mk, mv, wxo, gxo, gffn, l):
    m = y.shape[0]
    tm = TM
    per_seq = SEQ // tm
    rows = lambda c: pl.BlockSpec((tm, c), lambda i: (i, 0))
    full = lambda a: _layer_param(a, l)
    mem = pl.BlockSpec((None, N_MEM, D_X), lambda i: (l, i // per_seq, 0))
    return pl.pallas_call(
        _mid_prompt_kernel,
        grid=(m // tm,),
        in_specs=[rows(D_MODEL), rows(D_MODEL), _resident(wout), full(gmp), full(gxp),
                  _resident(wxq), mem, mem, _resident(wxo), full(gxo), full(gffn)],
        out_specs=[rows(D_MODEL), rows(D_MODEL)],
        out_shape=[jax.ShapeDtypeStruct((m, D_MODEL), F32), jax.ShapeDtypeStruct((m, D_MODEL), BF16)],
        compiler_params=_params("parallel"),
        name="mid_prompt",
    )(y, h, wout, gmp, gxp, wxq, mk, mv, wxo, gxo, gffn)


def _mid_a_kernel(y_ref, h_ref, wout_ref, gmp_ref, gxp_ref, wxq_ref, h1_ref, q_ref):
    h1, q = _stage_a(y_ref[...], h_ref[...], wout_ref, gmp_ref, gxp_ref, wxq_ref)
    h1_ref[...] = h1
    q_ref[...] = q.astype(BF16)


def mid_a(y, h, wout, gmp, gxp, wxq, l):
    m = y.shape[0]
    tm = TM
    rows = lambda c: pl.BlockSpec((tm, c), lambda i: (i, 0))
    full = lambda a: _layer_param(a, l)
    return pl.pallas_call(
        _mid_a_kernel,
        grid=(m // tm,),
        in_specs=[rows(D_MODEL), rows(D_MODEL), _resident(wout), full(gmp), full(gxp),
                  _resident(wxq)],
        out_specs=[rows(D_MODEL), rows(D_X)],
        out_shape=[jax.ShapeDtypeStruct((m, D_MODEL), F32), jax.ShapeDtypeStruct((m, D_X), BF16)],
        compiler_params=_params("parallel"),
        name="mid_a",
    )(y, h, wout, gmp, gxp, wxq)


def _mid_c_kernel(o_ref, h1_ref, wxo_ref, gxo_ref, gffn_ref, h2_ref, xn_ref):
    h2, xn = _stage_c(o_ref[...], h1_ref[...], wxo_ref, gxo_ref, gffn_ref)
    h2_ref[...] = h2
    xn_ref[...] = xn


def mid_c(o, h1, wxo, gxo, gffn, l):
    m = o.shape[0]
    tm = TM
    rows = lambda c: pl.BlockSpec((tm, c), lambda i: (i, 0))
    full = lambda a: _layer_param(a, l)
    return pl.pallas_call(
        _mid_c_kernel,
        grid=(m // tm,),
        in_specs=[rows(D_X), rows(D_MODEL), _resident(wxo), full(gxo), full(gffn)],
        out_specs=[rows(D_MODEL), rows(D_MODEL)],
        out_shape=[jax.ShapeDtypeStruct((m, D_MODEL), F32), jax.ShapeDtypeStruct((m, D_MODEL), BF16)],
        compiler_params=_params("parallel"),
        name="mid_c",
    )(o, h1, wxo, gxo, gffn)


def _attn_sample_kernel(q_ref, k_ref, v_ref, o_ref):
    bblk = q_ref.shape[0]
    lane_head = lax.broadcasted_iota(jnp.int32, (V7X_SUBLANES, D_X), 1) // D_XHEAD

    def scores(b):
        q8 = q_ref[b]
        qbd = jnp.concatenate([jnp.where(lane_head == hd, q8, jnp.zeros_like(q8)) for hd in range(N_XHEADS)], axis=0)
        return lax.dot_general(qbd, k_ref[b].astype(BF16), _NT, preferred_element_type=F32) * (D_XHEAD ** -0.5)

    def attend(b, s):
        of = _dot(_softmax_rows(s).astype(BF16), v_ref[b].astype(BF16))
        o8 = jnp.zeros((V7X_SUBLANES, D_X), F32)
        for hd in range(N_XHEADS):
            o8 = o8 + jnp.where(lane_head == hd, of[hd * V7X_SUBLANES:(hd + 1) * V7X_SUBLANES], 0.0)
        o_ref[b] = o8.astype(BF16)

    s = scores(0)
    for b in range(1, bblk):
        s_next = scores(b)
        attend(b - 1, s)
        s = s_next
    attend(bblk - 1, s)


def attn_sample(q, k, v, l):
    bblk = BB_ATT
    qs = pl.BlockSpec((bblk, V7X_SUBLANES, D_X), lambda b: (b, 0, 0))
    ms = pl.BlockSpec((None, bblk, N_MEM, D_X), lambda b: (l, b, 0, 0))
    return pl.pallas_call(
        _attn_sample_kernel,
        grid=(DEC_BATCH // bblk,),
        in_specs=[qs, ms, ms],
        out_specs=qs,
        out_shape=jax.ShapeDtypeStruct(q.shape, BF16),
        compiler_params=_params("parallel"),
        name="attn_sample",
    )(q, k, v)


def _gate(cg, cu):
    return (cg * _sigmoid(cg) * cu).astype(BF16)


def _shift_conv3(prev, h, fw_ref):
    rows = h.shape[0]
    ext = jnp.concatenate([prev, h], axis=0)
    return (fw_ref[0:1, :] * ext[HIST_F - 2:HIST_F - 2 + rows]
            + fw_ref[1:2, :] * ext[HIST_F - 1:HIST_F - 1 + rows]
            + fw_ref[2:3, :] * h)


def _up_prompt_kernel(*refs, per_seq, subs, cast_next):
    n_cast = N_CAST if cast_next else 0
    xn_ref, wg_ref, wu_ref, fwg_ref, fwu_ref = refs[:5]
    src = refs[5:5 + n_cast]
    o_ref, tg_ref, tu_ref = refs[5 + n_cast:8 + n_cast]
    dst = refs[8 + n_cast:8 + 2 * n_cast]
    carg, caru = refs[8 + 2 * n_cast:]
    i = pl.program_id(0)
    j = pl.program_id(1)

    @pl.when(i % per_seq == 0)
    def _():
        carg[j] = jnp.zeros(carg.shape[1:], F32)
        caru[j] = jnp.zeros(caru.shape[1:], F32)

    if cast_next:
        _cast_chunk(jnp.minimum(i * pl.num_programs(1) + j, CAST_CHUNKS - 1), src, dst)

    prev_g = carg[j]
    prev_u = caru[j]
    r0 = 0
    for sub in subs:
        rs = slice(r0, r0 + sub)
        r0 += sub
        hg = _dot(xn_ref[rs, :], wg_ref[...])
        hu = _dot(xn_ref[rs, :], wu_ref[...])
        o_ref[rs, :] = _gate(_shift_conv3(prev_g, hg, fwg_ref), _shift_conv3(prev_u, hu, fwu_ref))
        prev_g = hg[sub - HIST_F:]
        prev_u = hu[sub - HIST_F:]
    carg[j] = prev_g
    caru[j] = prev_u
    tg_ref[...] = prev_g
    tu_ref[...] = prev_u


def up_prompt(xn, wg, wu, fwg, fwu, l, next_sources=None):
    m, k = xn.shape
    tm, tn = TM_UP, TN_UP
    assert sum(SUBS_UP) == tm
    nj = FF_PAD // tn
    assert (m // tm) * nj >= CAST_CHUNKS
    wcol = pl.BlockSpec((k, tn), lambda i, j: (0, j))
    fcol = pl.BlockSpec((None, HIST_F, tn), lambda i, j: (l, 0, j))
    tail = pl.BlockSpec((HIST_F, tn), lambda i, j: (i, j))
    cast_in, cast_out, cast_shape = ([], [], [])
    if next_sources is not None:
        cast_in, cast_out, cast_shape = _cast_specs(l + 1, lambda i, j: jnp.minimum(i * nj + j, CAST_CHUNKS - 1))
    res = pl.pallas_call(
        functools.partial(_up_prompt_kernel, per_seq=SEQ // tm, subs=SUBS_UP, cast_next=next_sources is not None),
        grid=(m // tm, nj),
        in_specs=[pl.BlockSpec((tm, k), lambda i, j: (i, 0)), wcol, wcol, fcol, fcol] + cast_in,
        out_specs=[pl.BlockSpec((tm, tn), lambda i, j: (i, j)), tail, tail] + cast_out,
        out_shape=[jax.ShapeDtypeStruct((m, FF_PAD), BF16),
                   jax.ShapeDtypeStruct((m // tm * HIST_F, FF_PAD), F32),
                   jax.ShapeDtypeStruct((m // tm * HIST_F, FF_PAD), F32)] + cast_shape,
        scratch_shapes=[pltpu.VMEM((nj, HIST_F, tn), F32), pltpu.VMEM((nj, HIST_F, tn), F32)],
        compiler_params=_params("arbitrary", "arbitrary"),
        name="up_prompt",
    )(xn, wg, wu, fwg, fwu, *(next_sources or ()))
    return res[0], res[1], res[2], tuple(res[3:])


def _down_kernel(g_ref, w_ref, h_ref, gn_ref, o_ref, acc):
    j = pl.program_id(1)
    last = pl.num_programs(1) - 1
    tm = g_ref.shape[0]
    tn = w_ref.shape[1]
    done = D_MODEL - tn

    @pl.when(j < last)
    def _():
        acc[:, pl.ds(pl.multiple_of(j * tn, tn), tn)] = _dot(g_ref[...], w_ref[...])

    @pl.when(j == last)
    def _():
        for r in range(0, tm, tm // 2):
            rs = slice(r, r + tm // 2)
            f = jnp.concatenate([acc[rs, 0:done], _dot(g_ref[rs, :], w_ref[...])], axis=-1)
            o_ref[rs, :] = h_ref[rs, :] + _rms(f, gn_ref[...])


def down(g, w, h, gn, l):
    m, k = g.shape
    tm, tn = TM, TN_DOWN
    return pl.pallas_call(
        _down_kernel,
        grid=(m // tm, D_MODEL // tn),
        in_specs=[pl.BlockSpec((tm, k), lambda i, j: (i, 0)),
                  pl.BlockSpec((k, tn), lambda i, j: (0, j)),
                  pl.BlockSpec((tm, D_MODEL), lambda i, j: (i, 0)),
                  _layer_param(gn, l)],
        out_specs=pl.BlockSpec((tm, D_MODEL), lambda i, j: (i, 0)),
        out_shape=jax.ShapeDtypeStruct((m, D_MODEL), F32),
        scratch_shapes=[pltpu.VMEM((tm, D_MODEL - tn), F32)],
        compiler_params=_params("parallel", "arbitrary"),
        name="down",
    )(g, w, h, gn)


def _ffn_sample_kernel(xn_ref, wg_ref, wu_ref, fwg_ref, fwu_ref, p0g_ref, p1g_ref, p0u_ref, p1u_ref,
                       wd_ref, h_ref, gn_ref, o_ref, tg_ref, tu_ref):
    nb = DEC_BATCH
    j = pl.program_id(0)

    @pl.when(j == 0)
    def _():
        o_ref[...] = jnp.zeros(o_ref.shape, F32)

    xn = xn_ref[...]
    hg = _dot(xn, wg_ref[...])
    hu = _dot(xn, wu_ref[...])
    tg_ref[...] = hg[(DEC_SEQ - 2) * nb:, :]
    tu_ref[...] = hu[(DEC_SEQ - 2) * nb:, :]
    ext_g = [p0g_ref[...], p1g_ref[...]] + [hg[t * nb:(t + 1) * nb] for t in range(DEC_SEQ)]
    ext_u = [p0u_ref[...], p1u_ref[...]] + [hu[t * nb:(t + 1) * nb] for t in range(DEC_SEQ)]
    for t in range(DEC_SEQ):
        cg = fwg_ref[0:1, :] * ext_g[t] + fwg_ref[1:2, :] * ext_g[t + 1] + fwg_ref[2:3, :] * ext_g[t + 2]
        cu = fwu_ref[0:1, :] * ext_u[t] + fwu_ref[1:2, :] * ext_u[t + 1] + fwu_ref[2:3, :] * ext_u[t + 2]
        ts = slice(t * nb, (t + 1) * nb)
        o_ref[ts, :] += _dot(_gate(cg, cu), wd_ref[...])

    @pl.when(j == pl.num_programs(0) - 1)
    def _():
        o_ref[...] = h_ref[...] + _rms(o_ref[...], gn_ref[...])


def ffn_sample(xn, wg, wu, fwg, fwu, prev, wd, h, gn, l):
    m, k = xn.shape
    tn = TN_UP
    nj = FF_PAD // tn
    col = lambda r: pl.BlockSpec((r, tn), lambda j: (0, j))
    wcol = pl.BlockSpec((k, tn), lambda j: (0, j))
    fcol = pl.BlockSpec((None, HIST_F, tn), lambda j: (l, 0, j))
    prev_row = lambda r, half: pl.BlockSpec((None, None, None, DEC_BATCH, tn), lambda j: (l, r, half, 0, j))
    whole = lambda r, c: pl.BlockSpec((r, c), lambda j: (0, 0))
    nt = (FFN_CONV_WIDTH - 1) * DEC_BATCH
    return pl.pallas_call(
        _ffn_sample_kernel,
        grid=(nj,),
        in_specs=[whole(m, k), wcol, wcol, fcol, fcol, prev_row(0, 0), prev_row(1, 0), prev_row(0, 1), prev_row(1, 1),
                  pl.BlockSpec((tn, D_MODEL), lambda j: (j, 0)), whole(m, D_MODEL), _layer_param(gn, l)],
        out_specs=[whole(m, D_MODEL), col(nt), col(nt)],
        out_shape=[jax.ShapeDtypeStruct((m, D_MODEL), F32),
                   jax.ShapeDtypeStruct((nt, FF_PAD), F32),
                   jax.ShapeDtypeStruct((nt, FF_PAD), F32)],
        compiler_params=_params("arbitrary"),
        name="ffn_sample",
    )(xn, wg, wu, fwg, fwu, prev, prev, prev, prev, wd, h, gn)


def kernel(x_prompt, x_sample, cache_mem_k, cache_mem_v, state_conv_b, state_pool, state_sconv, state_ffn_conv, mem_prompt, g_mix_pre, g_mix_post, g_mem, g_x_pre, g_x_post, g_ffn_pre, g_ffn_post, w_in, w_out, a_norm_g, a_ws, a_bs, b_conv_w, b_conv_b, b_gn_g, b_gn_b, c_lin, c_scale, d_conv_w, w_xq, w_xk, w_xv, w_xo, w_up, f_conv_w, w_down):
    nb, ns = DEC_BATCH, DEC_SEQ
    w_sources = (w_in, w_out, w_xq, w_xo, w_up, w_up, w_down)
    w_kv_b = cast_weight_pair(w_xk, w_xv)
    layer_w = cast_layer(w_sources, 0)

    hp = x_prompt.reshape(BATCH * SEQ, D_MODEL)
    hs = jnp.transpose(x_sample, (1, 0, 2)).reshape(ns * nb, D_MODEL)
    mem = mem_prompt.reshape(BATCH * N_MEM, D_MODEL)
    st_b = state_conv_b.reshape(DEPTH, nb, -1)
    st_p = state_pool.reshape(DEPTH, nb, -1)
    st_s = state_sconv.reshape(DEPTH, nb, -1)

    row = lambda a: a.reshape(DEPTH, 1, -1)
    ag, bb, gng, gnb, cs = (row(a) for a in (a_norm_g, b_conv_b, b_gn_g, b_gn_b, c_scale))
    gpre, gmp, gxp, gxo, gfp, gfo = (row(a) for a in (g_mix_pre, g_mix_post, g_x_pre, g_x_post, g_ffn_pre,
                                                      g_ffn_post))
    abt = jnp.transpose(a_bs, (0, 2, 1))
    pad_rows = lambda a, rows: jnp.pad(a, ((0, 0), (0, rows - a.shape[1]), (0, 0)))
    bw = pad_rows(b_conv_w, HIST_B)
    dw = pad_rows(d_conv_w, HIST_D)
    fw = jnp.pad(f_conv_w.reshape(DEPTH, FFN_CONV_WIDTH, 2, D_FF),
                 ((0, 0), (0, HIST_F - FFN_CONV_WIDTH), (0, 0), (0, FF_PAD - D_FF)))
    fwg, fwu = fw[:, :, 0], fw[:, :, 1]
    aw4 = jnp.repeat(jnp.transpose(a_ws[:, :, :ns, :ns], (0, 2, 3, 1)).reshape(DEPTH, ns * ns, N_SUB), D_SUB, axis=2)
    ab4 = jnp.repeat(jnp.transpose(a_bs[:, :, :ns], (0, 2, 1)), D_SUB, axis=2)
    keep = FFN_CONV_WIDTH - 1
    prev_f = jnp.pad(jnp.transpose(state_ffn_conv.reshape(DEPTH, nb, keep, 2, D_FF), (0, 2, 3, 1, 4)),
                     ((0, 0),) * 4 + ((0, FF_PAD - D_FF),))

    mk, mv = memory_kv(mem, row(g_mem), w_kv_b, TM)

    p_states, s_states = [], []
    for l in range(DEPTH):
        w_in_b, w_out_b, w_xq_b, w_xo_b, w_g_b, w_u_b, w_down_b = layer_w
        y, nbp, npp, nsp = inmix_prompt(hp.reshape(BATCH, SEQ, D_MODEL), gpre, w_in_b, ag, a_ws, abt, bw, bb,
                                        gng, gnb, c_lin, cs, dw, l)
        h2, xn = mid_prompt(y.reshape(BATCH * SEQ, D_MODEL), hp, w_out_b, gmp, gxp, w_xq_b, mk, mv, w_xo_b, gxo,
                            gfp, l)
        gate, tg, tu, layer_w = up_prompt(xn, w_g_b, w_u_b, fwg, fwu, l, w_sources if l + 1 < DEPTH else None)
        hp = down(gate, w_down_b, h2, gfo, l)
        p_states.append((nbp, npp, nsp, tg, tu))

        zs = norm_matmul(hs, gpre, w_in_b, l, TM, TN_IN)
        ys, nbs, nps, nss, vs = mixer_sample(zs.reshape(ns, nb, D_IN), st_b, st_p, st_s, ag, aw4, ab4, bw, bb,
                                             gng, gnb, c_lin, cs, dw, l)
        h1s, qs = mid_a(ys.reshape(ns * nb, D_MODEL), hs, w_out_b, gmp, gxp, w_xq_b, l)
        q8 = jnp.pad(jnp.transpose(qs.reshape(ns, nb, D_X), (1, 0, 2)), ((0, 0), (0, V7X_SUBLANES - ns), (0, 0)))
        o8 = attn_sample(q8, cache_mem_k, cache_mem_v, l)
        os_ = jnp.transpose(o8[:, :ns], (1, 0, 2)).reshape(ns * nb, D_X)
        h2s, xns = mid_c(os_, h1s, w_xo_b, gxo, gfp, l)
        hs, tgs, tus = ffn_sample(xns, w_g_b, w_u_b, fwg, fwu, prev_f, w_down_b, h2s, gfo, l)
        s_states.append((nbs, nps, nss, tgs, tus, vs))

    nbp, npp, nsp, tg, tu = (jnp.stack(a) for a in zip(*p_states))
    nbs, nps, nss, tgs, tus, vs = (jnp.stack(a) for a in zip(*s_states))
    last = lambda a: a.reshape(DEPTH, BATCH, -1, HIST_F, FF_PAD)[:, :, -1, HIST_F - keep:, :D_FF]
    bmajor = lambda a: jnp.transpose(a.reshape(DEPTH, keep, nb, FF_PAD)[..., :D_FF], (0, 2, 1, 3))
    return (hp.reshape(BATCH, SEQ, D_MODEL),
            jnp.transpose(hs.reshape(ns, nb, D_MODEL), (1, 0, 2)),
            mk.reshape(DEPTH, BATCH, N_MEM, D_X),
            mv.reshape(DEPTH, BATCH, N_MEM, D_X),
            nbp[:, :, HIST_B - (CONV_B_WIDTH - 1):],
            npp[:, :, HIST_C - POOL_PREV:],
            nsp[:, :, HIST_D - (SCONV_WIDTH - 1):],
            jnp.concatenate([last(tg), last(tu)], axis=-1),
            nbs.reshape(DEPTH, nb, CONV_B_WIDTH - 1, D_GROUP),
            nps.reshape(DEPTH, nb, POOL_PREV, D_GROUP),
            nss.reshape(DEPTH, nb, SCONV_WIDTH - 1, D_GROUP),
            jnp.concatenate([bmajor(tgs), bmajor(tus)], axis=-1),
            vs.reshape(DEPTH, nb, ns, D_GROUP))
```

```python
import functools

import jax
import jax.numpy as jnp
from jax import lax
from jax.experimental import pallas as pl
from jax.experimental.pallas import tpu as pltpu

F32 = jnp.float32
BF16 = jnp.bfloat16

D_MODEL = 2048
BATCH = 4
SEQ = 2048
DEPTH = 4
DEC_BATCH = 128
DEC_SEQ = 4
PAST_LEN = 16384
D_GROUP = 512
N_SUB = 4
D_SUB = 128
D_IN = 8 * D_GROUP
CHUNK = 128
CONV_B_WIDTH = 31
POOL_WINDOWS = (2, 4, 8, 16)
POOL_PREV = 15
SCONV_WIDTH = 3
FFN_CONV_WIDTH = 3
D_FF = 5504
N_MEM = 256
N_XHEADS = 4
D_XHEAD = 128
D_X = 512
EPS = 1e-6

V7X_SUBLANES = 8
V7X_LANES = 128
V7X_VMEM_LIMIT_BYTES = 56 * 1024 * 1024

HIST_B = 32
HIST_C = 16
HIST_D = 8
HIST_F = 8

TM = 512
TM_UP = 2048
SUBS_UP = (512, 512, 512, 512)
TM_IN = 1024
SUB_MID = 256
TT = 512
ROWS = 32
TN_IN = 1024
TN_UP = 512
FF_PAD = -(-D_FF // TN_UP) * TN_UP
TN_DOWN = 512
CAST_CHUNKS = 32
BB_MIX = 32
BB_ATT = 8


def _layer_param(a, l):
    tail = a.shape[1:]
    return pl.BlockSpec((None,) + tail, lambda *_: (l,) + (0,) * len(tail))


def _params(*sem):
    return pltpu.CompilerParams(dimension_semantics=sem, vmem_limit_bytes=V7X_VMEM_LIMIT_BYTES)


def _rms(x, g):
    return x * lax.rsqrt(jnp.mean(x * x, axis=-1, keepdims=True) + EPS) * g


def _sigmoid(x):
    return 1.0 / (1.0 + jnp.exp(-x))


def _dot(a, b):
    return jnp.dot(a, b, preferred_element_type=F32)


N_CAST = 7


def _cast_specs(layer, chunk):
    def pair(rows_total, cols, col_block=0, out_rows=None, out_cols=None):
        rows = (out_rows or rows_total) // CAST_CHUNKS
        src = pl.BlockSpec((None, rows, cols), lambda *g: (layer, chunk(*g), col_block))
        dst = pl.BlockSpec((rows, out_cols or cols), lambda *g: (chunk(*g), 0))
        return src, dst, jax.ShapeDtypeStruct((out_rows or rows_total, out_cols or cols), BF16)
    table = [pair(D_MODEL, D_IN), pair(D_MODEL, D_MODEL), pair(D_MODEL, D_X), pair(D_X, D_MODEL),
             pair(D_MODEL, D_FF, 0, out_cols=FF_PAD), pair(D_MODEL, D_FF, 1, out_cols=FF_PAD),
             pair(D_FF, D_MODEL, out_rows=FF_PAD)]
    return tuple(list(t) for t in zip(*table))


def _cast_chunk(c, src, dst):
    for s, d in zip(src[:4], dst[:4]):
        d[...] = s[...].astype(BF16)
    for s, d in zip(src[4:6], dst[4:6]):
        d[:, 0:D_FF] = s[...].astype(BF16)
        d[:, D_FF:] = jnp.zeros((d.shape[0], d.shape[1] - D_FF), BF16)
    rows = dst[6].shape[0]
    row = c * rows + lax.broadcasted_iota(jnp.int32, dst[6].shape, 0)
    dst[6][...] = jnp.where(row < D_FF, src[6][...], 0.0).astype(BF16)


def _cast_layer_kernel(*refs):
    _cast_chunk(pl.program_id(0), refs[:N_CAST], refs[N_CAST:])


def cast_layer(sources, layer):
    in_specs, out_specs, out_shape = _cast_specs(layer, lambda c: c)
    return pl.pallas_call(
        _cast_layer_kernel,
        grid=(CAST_CHUNKS,),
        in_specs=in_specs,
        out_specs=out_specs,
        out_shape=out_shape,
        compiler_params=_params("parallel"),
        name="cast_layer",
    )(*sources)


def _cast_pair_kernel(a_ref, b_ref, o_ref):
    n = a_ref.shape[-1]
    o_ref[:, 0:n] = a_ref[...].astype(BF16)
    o_ref[:, n:2 * n] = b_ref[...].astype(BF16)


def cast_weight_pair(a, b):
    depth, r, c = a.shape
    spec = pl.BlockSpec((None, r, c), lambda l: (l, 0, 0))
    return pl.pallas_call(
        _cast_pair_kernel,
        grid=(depth,),
        in_specs=[spec, spec],
        out_specs=pl.BlockSpec((None, r, 2 * c), lambda l: (l, 0, 0)),
        out_shape=jax.ShapeDtypeStruct((depth, r, 2 * c), BF16),
        compiler_params=_params("parallel"),
        name="cast_weight_pair",
    )(a, b)


def _norm_matmul_kernel(x_ref, g_ref, w_ref, o_ref, xn_ref):
    @pl.when(pl.program_id(1) == 0)
    def _():
        xn_ref[...] = _rms(x_ref[...], g_ref[...]).astype(BF16)

    o_ref[...] = _dot(xn_ref[...], w_ref[...])


def norm_matmul(x, g, w, l, tm, tn):
    m, k = x.shape
    n = w.shape[1]
    return pl.pallas_call(
        _norm_matmul_kernel,
        grid=(m // tm, n // tn),
        in_specs=[pl.BlockSpec((tm, k), lambda i, j: (i, 0)),
                  _layer_param(g, l),
                  pl.BlockSpec((k, tn), lambda i, j: (0, j))],
        out_specs=pl.BlockSpec((tm, tn), lambda i, j: (i, j)),
        out_shape=jax.ShapeDtypeStruct((m, n), F32),
        scratch_shapes=[pltpu.VMEM((tm, k), BF16)],
        compiler_params=_params("parallel", "arbitrary"),
        name="norm_matmul",
    )(x, g, w)


def _memory_kv_kernel(x_ref, g_ref, w_ref, k_ref, v_ref):
    kv = _dot(_rms(x_ref[...], g_ref[...]).astype(BF16), w_ref[...])
    n = k_ref.shape[-1]
    k_ref[...] = kv[:, 0:n]
    v_ref[...] = kv[:, n:2 * n]


def memory_kv(x, g, w, tm):
    m, k = x.shape
    depth, _, n2 = w.shape
    half = pl.BlockSpec((None, tm, n2 // 2), lambda l, i: (l, i, 0))
    shape = jax.ShapeDtypeStruct((depth, m, n2 // 2), F32)
    return pl.pallas_call(
        _memory_kv_kernel,
        grid=(depth, m // tm),
        in_specs=[pl.BlockSpec((tm, k), lambda l, i: (i, 0)),
                  pl.BlockSpec((None, 1, k), lambda l, i: (l, 0, 0)),
                  pl.BlockSpec((None, k, n2), lambda l, i: (l, 0, 0))],
        out_specs=[half, half],
        out_shape=[shape, shape],
        compiler_params=_params("parallel", "parallel"),
        name="memory_kv",
    )(x, g, w)


def _layer_norm(v, g):
    mu = jnp.mean(v, axis=-1, keepdims=True)
    d = v - mu
    var = jnp.mean(d * d, axis=-1, keepdims=True)
    return d * lax.rsqrt(var + EPS) * g


def _group_norm_silu(y, g, b):
    mu = jnp.mean(y, axis=-1, keepdims=True)
    d = y - mu
    var = jnp.mean(d * d, axis=-1, keepdims=True)
    yn = d * lax.rsqrt(var + EPS) * g + b
    return yn * _sigmoid(yn)


def _inmix_prompt_kernel(x_ref, gpre_ref, win_ref, ag_ref, aws_ref, abt_ref, bw_ref, bb_ref, gng_ref, gnb_ref,
                         clin_ref, cs_ref, dw_ref,
                         y_ref, nb_ref, np_ref, ns_ref,
                         extb, extc, extd, pooled):
    t = pl.program_id(1)
    tt = y_ref.shape[0]
    quarter = 2 * D_GROUP

    @pl.when(t == 0)
    def _():
        for s in range(V7X_SUBLANES):
            extb[s, 0:HIST_B, :] = jnp.zeros((HIST_B, D_GROUP), F32)
            extb[s, tt + HIST_B - V7X_SUBLANES:tt + HIST_B, :] = jnp.zeros((V7X_SUBLANES, D_GROUP), F32)
        extc[0:HIST_C, :] = jnp.zeros((HIST_C, D_GROUP), F32)
        extd[0:HIST_D, :] = jnp.zeros((HIST_D, D_GROUP), F32)

    xn = _rms(x_ref[...], gpre_ref[...]).astype(BF16)
    z = {}
    proj = lambda q: _dot(xn, win_ref[:, q * quarter:(q + 1) * quarter])

    def zcol(k, rs=slice(None)):
        return z[k // 2][rs, (k % 2) * D_GROUP:(k % 2 + 1) * D_GROUP]

    z[1] = proj(1)
    hb = zcol(2) * _sigmoid(zcol(3))
    for s in range(V7X_SUBLANES):
        extb[s, HIST_B - s:HIST_B - s + tt, :] = hb
    n_chunks = tt // ROWS
    later = {0: 0, n_chunks // 3: 2, 2 * n_chunks // 3: 3}
    for c in range(n_chunks):
        if c in later:
            z[later[c]] = proj(later[c])
        for g in range(N_SUB):
            ls = slice(g * D_SUB, (g + 1) * D_SUB)
            acc = jnp.zeros((ROWS, D_SUB), F32)
            for k in range(CONV_B_WIDTH):
                tiles, s = divmod(HIST_B - (CONV_B_WIDTH - 1) + k, V7X_SUBLANES)
                r0 = c * ROWS + tiles * V7X_SUBLANES
                acc = acc + bw_ref[k:k + 1, ls] * extb[s, r0:r0 + ROWS, ls]
            yb = _group_norm_silu(acc + bb_ref[:, ls], gng_ref[:, ls], gnb_ref[:, ls])
            y_ref[c * ROWS:(c + 1) * ROWS, D_GROUP + g * D_SUB:D_GROUP + (g + 1) * D_SUB] = yb.astype(BF16)

    vn = _layer_norm(zcol(1), ag_ref[...]).astype(BF16)
    row = lax.broadcasted_iota(jnp.int32, (CHUNK, CHUNK), 0)
    col = lax.broadcasted_iota(jnp.int32, (CHUNK, CHUNK), 1)
    for h in range(N_SUB):
        wm = jnp.where(row >= col, aws_ref[h], 0.0).astype(BF16)
        bias = abt_ref[:, h:h + 1]
        for c in range(tt // CHUNK):
            rs = slice(c * CHUNK, (c + 1) * CHUNK)
            ls = slice(h * D_SUB, (h + 1) * D_SUB)
            zz = _dot(wm, vn[rs, ls]) + bias
            y_ref[rs, ls] = (z[0][rs, ls] * zz).astype(BF16)

    extc[HIST_C:HIST_C + tt, :] = zcol(4)
    for c in range(tt // ROWS):
        pos = t * tt + c * ROWS + lax.broadcasted_iota(jnp.int32, (ROWS, 1), 0)
        for g, win in enumerate(POOL_WINDOWS):
            ls = slice(g * D_SUB, (g + 1) * D_SUB)
            r0 = c * ROWS + HIST_C
            x = extc[r0:r0 + ROWS, ls]
            s = x
            for i in range(1, win):
                s = s + extc[r0 - i:r0 - i + ROWS, ls]
            cnt = jnp.minimum(pos + 1, win).astype(F32)
            pooled[c * ROWS:(c + 1) * ROWS, ls] = (s / cnt - x).astype(BF16)
    for g in range(N_SUB):
        ls = slice(g * D_SUB, (g + 1) * D_SUB)
        yc = _dot(pooled[:, ls], clin_ref[g].astype(BF16)) * cs_ref[:, ls]
        y_ref[:, 2 * D_GROUP + g * D_SUB:2 * D_GROUP + (g + 1) * D_SUB] = yc.astype(BF16)

    extd[HIST_D:HIST_D + tt, :] = zcol(7) * zcol(5)
    for c in range(tt // ROWS):
        r0 = c * ROWS + HIST_D
        conv = (dw_ref[0:1, :] * extd[r0 - 2:r0 - 2 + ROWS, :]
                + dw_ref[1:2, :] * extd[r0 - 1:r0 - 1 + ROWS, :]
                + dw_ref[2:3, :] * extd[r0:r0 + ROWS, :])
        rs = slice(c * ROWS, (c + 1) * ROWS)
        y_ref[rs, 3 * D_GROUP:4 * D_GROUP] = (zcol(6, rs) * conv).astype(BF16)

    for s in range(V7X_SUBLANES):
        extb[s, 0:HIST_B, :] = extb[s, tt:tt + HIST_B, :]
    extc[0:HIST_C, :] = extc[tt:tt + HIST_C, :]
    extd[0:HIST_D, :] = extd[tt:tt + HIST_D, :]

    @pl.when(t == pl.num_programs(1) - 1)
    def _():
        nb_ref[...] = extb[0, 0:HIST_B, :]
        np_ref[...] = extc[0:HIST_C, :]
        ns_ref[...] = extd[0:HIST_D, :]


def inmix_prompt(x, gpre, win, ag, aws, abt, bw, bb, gng, gnb, clin, cs, dw, l):
    tt = TT
    full = lambda a: _layer_param(a, l)
    hist = lambda r: pl.BlockSpec((None, r, D_GROUP), lambda b, t: (b, 0, 0))
    return pl.pallas_call(
        _inmix_prompt_kernel,
        grid=(BATCH, SEQ // tt),
        in_specs=[pl.BlockSpec((None, tt, D_MODEL), lambda b, t: (b, t, 0)),
                  full(gpre),
                  pl.BlockSpec((D_MODEL, D_IN), lambda b, t: (0, 0), pipeline_mode=pl.Buffered(1)),
                  full(ag), full(aws), full(abt), full(bw), full(bb), full(gng), full(gnb),
                  full(clin), full(cs), full(dw)],
        out_specs=[pl.BlockSpec((None, tt, D_MODEL), lambda b, t: (b, t, 0)),
                   hist(HIST_B), hist(HIST_C), hist(HIST_D)],
        out_shape=[jax.ShapeDtypeStruct((BATCH, SEQ, D_MODEL), BF16),
                   jax.ShapeDtypeStruct((BATCH, HIST_B, D_GROUP), F32),
                   jax.ShapeDtypeStruct((BATCH, HIST_C, D_GROUP), F32),
                   jax.ShapeDtypeStruct((BATCH, HIST_D, D_GROUP), F32)],
        scratch_shapes=[pltpu.VMEM((V7X_SUBLANES, HIST_B + tt, D_GROUP), F32),
                        pltpu.VMEM((HIST_C + tt, D_GROUP), F32),
                        pltpu.VMEM((HIST_D + tt, D_GROUP), F32),
                        pltpu.VMEM((tt, D_GROUP), BF16)],
        compiler_params=_params("parallel", "arbitrary"),
        name="inmix_prompt",
    )(x, gpre, win, ag, aws, abt, bw, bb, gng, gnb, clin, cs, dw)


def _mixer_sample_kernel(z_ref, cb_ref, cp_ref, csc_ref, ag_ref, aw4_ref, ab4_ref, bw_ref, bb_ref,
                         gng_ref, gnb_ref, clin_ref, cs_ref, dw_ref,
                         y_ref, nb_ref, np_ref, nsc_ref, v_ref, pooled):
    bblk = z_ref.shape[1]
    blk = lambda r: slice(r * D_GROUP, (r + 1) * D_GROUP)

    vn = [_layer_norm(z_ref[t, :, D_GROUP:2 * D_GROUP], ag_ref[...]) for t in range(DEC_SEQ)]
    for i in range(DEC_SEQ):
        v_ref[:, blk(i)] = vn[i]
        zz = ab4_ref[i:i + 1, :]
        for j in range(i + 1):
            zz = zz + aw4_ref[i * DEC_SEQ + j:i * DEC_SEQ + j + 1, :] * vn[j]
        y_ref[i, :, 0:D_GROUP] = (z_ref[i, :, 0:D_GROUP] * zz).astype(BF16)

    nprev = CONV_B_WIDTH - 1
    hb = [z_ref[t, :, 2 * D_GROUP:3 * D_GROUP] * _sigmoid(z_ref[t, :, 3 * D_GROUP:4 * D_GROUP])
          for t in range(DEC_SEQ)]
    ext_b = lambda r: cb_ref[:, blk(r)] if r < nprev else hb[r - nprev]
    for t in range(DEC_SEQ):
        acc = jnp.zeros((bblk, D_GROUP), F32)
        for k in range(CONV_B_WIDTH):
            acc = acc + bw_ref[k:k + 1, :] * ext_b(t + k)
        acc = acc + bb_ref[...]
        for g in range(N_SUB):
            ls = slice(g * D_SUB, (g + 1) * D_SUB)
            yb = _group_norm_silu(acc[:, ls], gng_ref[:, ls], gnb_ref[:, ls])
            y_ref[t, :, D_GROUP + g * D_SUB:D_GROUP + (g + 1) * D_SUB] = yb.astype(BF16)
    for r in range(nprev):
        nb_ref[:, blk(r)] = ext_b(r + DEC_SEQ)

    cx = [z_ref[t, :, 4 * D_GROUP:5 * D_GROUP] for t in range(DEC_SEQ)]
    ext_c = lambda r: cp_ref[:, blk(r)] if r < POOL_PREV else cx[r - POOL_PREV]
    for t in range(DEC_SEQ):
        for g, win in enumerate(POOL_WINDOWS):
            ls = slice(g * D_SUB, (g + 1) * D_SUB)
            s = cx[t][:, ls]
            for i in range(1, win):
                s = s + ext_c(POOL_PREV + t - i)[:, ls]
            cnt = float(min(PAST_LEN + t + 1, win))
            pooled[t * bblk:(t + 1) * bblk, ls] = (s / cnt - cx[t][:, ls]).astype(BF16)
    for g in range(N_SUB):
        ls = slice(g * D_SUB, (g + 1) * D_SUB)
        yc = _dot(pooled[:, ls], clin_ref[g].astype(BF16)) * cs_ref[:, ls]
        for t in range(DEC_SEQ):
            y_ref[t, :, 2 * D_GROUP + g * D_SUB:2 * D_GROUP + (g + 1) * D_SUB] = (
                yc[t * bblk:(t + 1) * bblk].astype(BF16))
    for r in range(POOL_PREV):
        np_ref[:, blk(r)] = ext_c(r + DEC_SEQ)

    nsp = SCONV_WIDTH - 1
    hd = [z_ref[t, :, 7 * D_GROUP:8 * D_GROUP] * z_ref[t, :, 5 * D_GROUP:6 * D_GROUP] for t in range(DEC_SEQ)]
    ext_d = lambda r: csc_ref[:, blk(r)] if r < nsp else hd[r - nsp]
    for t in range(DEC_SEQ):
        conv = dw_ref[0:1, :] * ext_d(t) + dw_ref[1:2, :] * ext_d(t + 1) + dw_ref[2:3, :] * ext_d(t + 2)
        y_ref[t, :, 3 * D_GROUP:4 * D_GROUP] = (z_ref[t, :, 6 * D_GROUP:7 * D_GROUP] * conv).astype(BF16)
    for r in range(nsp):
        nsc_ref[:, blk(r)] = ext_d(r + DEC_SEQ)


def mixer_sample(z, cb, cp, csc, ag, aw4, ab4, bw, bb, gng, gnb, clin, cs, dw, l):
    bblk = BB_MIX
    full = lambda a: _layer_param(a, l)
    st_in = lambda a: pl.BlockSpec((None, bblk, a.shape[2]), lambda b: (l, b, 0))
    st = lambda a: pl.BlockSpec((bblk, a.shape[2]), lambda b: (b, 0))
    nv = DEC_SEQ * D_GROUP
    return pl.pallas_call(
        _mixer_sample_kernel,
        grid=(DEC_BATCH // bblk,),
        in_specs=[pl.BlockSpec((DEC_SEQ, bblk, D_IN), lambda b: (0, b, 0)),
                  st_in(cb), st_in(cp), st_in(csc),
                  full(ag), full(aw4), full(ab4), full(bw), full(bb), full(gng), full(gnb),
                  full(clin), full(cs), full(dw)],
        out_specs=[pl.BlockSpec((DEC_SEQ, bblk, D_MODEL), lambda b: (0, b, 0)),
                   st(cb), st(cp), st(csc),
                   pl.BlockSpec((bblk, nv), lambda b: (b, 0))],
        out_shape=[jax.ShapeDtypeStruct((DEC_SEQ, DEC_BATCH, D_MODEL), BF16),
                   jax.ShapeDtypeStruct(cb.shape[1:], F32),
                   jax.ShapeDtypeStruct(cp.shape[1:], F32),
                   jax.ShapeDtypeStruct(csc.shape[1:], F32),
                   jax.ShapeDtypeStruct((DEC_BATCH, nv), F32)],
        scratch_shapes=[pltpu.VMEM((DEC_SEQ * bblk, D_GROUP), BF16)],
        compiler_params=_params("parallel"),
        name="mixer_sample",
    )(z, cb, cp, csc, ag, aw4, ab4, bw, bb, gng, gnb, clin, cs, dw)


def _stage_a(y, h, wout_ref, gpost_ref, gpre_ref, wxq_ref):
    h1 = h + _rms(_dot(y, wout_ref[...]), gpost_ref[...])
    q = _dot(_rms(h1, gpre_ref[...]).astype(BF16), wxq_ref[...])
    return h1, q


def _stage_c(o, h1, wxo_ref, gpost_ref, gffn_ref):
    h2 = h1 + _rms(_dot(o, wxo_ref[...]), gpost_ref[...])
    return h2, _rms(h2, gffn_ref[...]).astype(BF16)


def _softmax_rows(s):
    e = jnp.exp(s - jnp.max(s, axis=-1, keepdims=True))
    return e / jnp.sum(e, axis=-1, keepdims=True)


_NT = (((1,), (1,)), ((), ()))


def _mid_prompt_kernel(y_ref, h_ref, wout_ref, gmp_ref, gxp_ref, wxq_ref, mk_ref, mv_ref, wxo_ref,
                       gxo_ref, gffn_ref, h2_ref, xn_ref):
    k = mk_ref[...].astype(BF16)
    v = mv_ref[...].astype(BF16)
    halves = [slice(r * SUB_MID, (r + 1) * SUB_MID) for r in range(y_ref.shape[0] // SUB_MID)]

    def out_proj(rs):
        return _dot(y_ref[rs, :], wout_ref[...])

    def residual_and_query_in(rs, mix):
        h1 = h_ref[rs, :] + _rms(mix, gmp_ref[...])
        return h1, _rms(h1, gxp_ref[...]).astype(BF16)

    def attention(xq):
        q = _dot(xq, wxq_ref[...]).astype(BF16)
        heads = []
        for hd in range(N_XHEADS):
            ls = slice(hd * D_XHEAD, (hd + 1) * D_XHEAD)
            s = lax.dot_general(q[:, ls], k[:, ls], _NT, preferred_element_type=F32) * (D_XHEAD ** -0.5)
            heads.append(_dot(_softmax_rows(s).astype(BF16), v[:, ls]))
        return _dot(jnp.concatenate(heads, axis=-1).astype(BF16), wxo_ref[...])

    def finish(rs, h1, xa):
        h2 = h1 + _rms(xa, gxo_ref[...])
        h2_ref[rs, :] = h2
        xn_ref[rs, :] = _rms(h2, gffn_ref[...]).astype(BF16)

    mix = [out_proj(rs) for rs in halves]
    state = [residual_and_query_in(halves[0], mix[0])]
    xa = []
    for r, rs in enumerate(halves):
        xa.append(attention(state[r][1]))
        if r + 1 < len(halves):
            state.append(residual_and_query_in(halves[r + 1], mix[r + 1]))
        if r > 0:
            finish(halves[r - 1], state[r - 1][0], xa[r - 1])
    finish(halves[-1], state[-1][0], xa[-1])


def _resident(w):
    return pl.BlockSpec(w.shape, lambda i: (0, 0), pipeline_mode=pl.Buffered(1))


def mid_prompt(y, h, wout, gmp, gxp, wxq, mk, mv, wxo, gxo, gffn, l):
    m = y.shape[0]
    tm = TM
    per_seq = SEQ // tm
    rows = lambda c: pl.BlockSpec((tm, c), lambda i: (i, 0))
    full = lambda a: _layer_param(a, l)
    mem = pl.BlockSpec((None, N_MEM, D_X), lambda i: (l, i // per_seq, 0))
    return pl.pallas_call(
        _mid_prompt_kernel,
        grid=(m // tm,),
        in_specs=[rows(D_MODEL), rows(D_MODEL), _resident(wout), full(gmp), full(gxp),
                  _resident(wxq), mem, mem, _resident(wxo), full(gxo), full(gffn)],
        out_specs=[rows(D_MODEL), rows(D_MODEL)],
        out_shape=[jax.ShapeDtypeStruct((m, D_MODEL), F32), jax.ShapeDtypeStruct((m, D_MODEL), BF16)],
        compiler_params=_params("parallel"),
        name="mid_prompt",
    )(y, h, wout, gmp, gxp, wxq, mk, mv, wxo, gxo, gffn)


def _mid_a_kernel(y_ref, h_ref, wout_ref, gmp_ref, gxp_ref, wxq_ref, h1_ref, q_ref):
    h1, q = _stage_a(y_ref[...], h_ref[...], wout_ref, gmp_ref, gxp_ref, wxq_ref)
    h1_ref[...] = h1
    q_ref[...] = q.astype(BF16)


def mid_a(y, h, wout, gmp, gxp, wxq, l):
    m = y.shape[0]
    tm = TM
    rows = lambda c: pl.BlockSpec((tm, c), lambda i: (i, 0))
    full = lambda a: _layer_param(a, l)
    return pl.pallas_call(
        _mid_a_kernel,
        grid=(m // tm,),
        in_specs=[rows(D_MODEL), rows(D_MODEL), _resident(wout), full(gmp), full(gxp),
                  _resident(wxq)],
        out_specs=[rows(D_MODEL), rows(D_X)],
        out_shape=[jax.ShapeDtypeStruct((m, D_MODEL), F32), jax.ShapeDtypeStruct((m, D_X), BF16)],
        compiler_params=_params("parallel"),
        name="mid_a",
    )(y, h, wout, gmp, gxp, wxq)


def _mid_c_kernel(o_ref, h1_ref, wxo_ref, gxo_ref, gffn_ref, h2_ref, xn_ref):
    h2, xn = _stage_c(o_ref[...], h1_ref[...], wxo_ref, gxo_ref, gffn_ref)
    h2_ref[...] = h2
    xn_ref[...] = xn


def mid_c(o, h1, wxo, gxo, gffn, l):
    m = o.shape[0]
    tm = TM
    rows = lambda c: pl.BlockSpec((tm, c), lambda i: (i, 0))
    full = lambda a: _layer_param(a, l)
    return pl.pallas_call(
        _mid_c_kernel,
        grid=(m // tm,),
        in_specs=[rows(D_X), rows(D_MODEL), _resident(wxo), full(gxo), full(gffn)],
        out_specs=[rows(D_MODEL), rows(D_MODEL)],
        out_shape=[jax.ShapeDtypeStruct((m, D_MODEL), F32), jax.ShapeDtypeStruct((m, D_MODEL), BF16)],
        compiler_params=_params("parallel"),
        name="mid_c",
    )(o, h1, wxo, gxo, gffn)


def _attn_sample_kernel(q_ref, k_ref, v_ref, o_ref):
    bblk = q_ref.shape[0]
    lane_head = lax.broadcasted_iota(jnp.int32, (V7X_SUBLANES, D_X), 1) // D_XHEAD

    def scores(b):
        q8 = q_ref[b]
        qbd = jnp.concatenate([jnp.where(lane_head == hd, q8, jnp.zeros_like(q8)) for hd in range(N_XHEADS)], axis=0)
        return lax.dot_general(qbd, k_ref[b].astype(BF16), _NT, preferred_element_type=F32) * (D_XHEAD ** -0.5)

    def attend(b, s):
        of = _dot(_softmax_rows(s).astype(BF16), v_ref[b].astype(BF16))
        o8 = jnp.zeros((V7X_SUBLANES, D_X), F32)
        for hd in range(N_XHEADS):
            o8 = o8 + jnp.where(lane_head == hd, of[hd * V7X_SUBLANES:(hd + 1) * V7X_SUBLANES], 0.0)
        o_ref[b] = o8.astype(BF16)

    s = scores(0)
    for b in range(1, bblk):
        s_next = scores(b)
        attend(b - 1, s)
        s = s_next
    attend(bblk - 1, s)


def attn_sample(q, k, v, l):
    bblk = BB_ATT
    qs = pl.BlockSpec((bblk, V7X_SUBLANES, D_X), lambda b: (b, 0, 0))
    ms = pl.BlockSpec((None, bblk, N_MEM, D_X), lambda b: (l, b, 0, 0))
    return pl.pallas_call(
        _attn_sample_kernel,
        grid=(DEC_BATCH // bblk,),
        in_specs=[qs, ms, ms],
        out_specs=qs,
        out_shape=jax.ShapeDtypeStruct(q.shape, BF16),
        compiler_params=_params("parallel"),
        name="attn_sample",
    )(q, k, v)


def _gate(cg, cu):
    return (cg * _sigmoid(cg) * cu).astype(BF16)


def _shift_conv3(prev, h, fw_ref):
    rows = h.shape[0]
    ext = jnp.concatenate([prev, h], axis=0)
    return (fw_ref[0:1, :] * ext[HIST_F - 2:HIST_F - 2 + rows]
            + fw_ref[1:2, :] * ext[HIST_F - 1:HIST_F - 1 + rows]
            + fw_ref[2:3, :] * h)


def _up_prompt_kernel(*refs, per_seq, subs, cast_next):
    n_cast = N_CAST if cast_next else 0
    xn_ref, wg_ref, wu_ref, fwg_ref, fwu_ref = refs[:5]
    src = refs[5:5 + n_cast]
    o_ref, tg_ref, tu_ref = refs[5 + n_cast:8 + n_cast]
    dst = refs[8 + n_cast:8 + 2 * n_cast]
    carg, caru = refs[8 + 2 * n_cast:]
    i = pl.program_id(0)
    j = pl.program_id(1)

    @pl.when(i % per_seq == 0)
    def _():
        carg[j] = jnp.zeros(carg.shape[1:], F32)
        caru[j] = jnp.zeros(caru.shape[1:], F32)

    if cast_next:
        _cast_chunk(jnp.minimum(i * pl.num_programs(1) + j, CAST_CHUNKS - 1), src, dst)

    prev_g = carg[j]
    prev_u = caru[j]
    r0 = 0
    for sub in subs:
        rs = slice(r0, r0 + sub)
        r0 += sub
        hg = _dot(xn_ref[rs, :], wg_ref[...])
        hu = _dot(xn_ref[rs, :], wu_ref[...])
        o_ref[rs, :] = _gate(_shift_conv3(prev_g, hg, fwg_ref), _shift_conv3(prev_u, hu, fwu_ref))
        prev_g = hg[sub - HIST_F:]
        prev_u = hu[sub - HIST_F:]
    carg[j] = prev_g
    caru[j] = prev_u
    tg_ref[...] = prev_g
    tu_ref[...] = prev_u


def up_prompt(xn, wg, wu, fwg, fwu, l, next_sources=None):
    m, k = xn.shape
    tm, tn = TM_UP, TN_UP
    assert sum(SUBS_UP) == tm
    nj = FF_PAD // tn
    assert (m // tm) * nj >= CAST_CHUNKS
    wcol = pl.BlockSpec((k, tn), lambda i, j: (0, j))
    fcol = pl.BlockSpec((None, HIST_F, tn), lambda i, j: (l, 0, j))
    tail = pl.BlockSpec((HIST_F, tn), lambda i, j: (i, j))
    cast_in, cast_out, cast_shape = ([], [], [])
    if next_sources is not None:
        cast_in, cast_out, cast_shape = _cast_specs(l + 1, lambda i, j: jnp.minimum(i * nj + j, CAST_CHUNKS - 1))
    res = pl.pallas_call(
        functools.partial(_up_prompt_kernel, per_seq=SEQ // tm, subs=SUBS_UP, cast_next=next_sources is not None),
        grid=(m // tm, nj),
        in_specs=[pl.BlockSpec((tm, k), lambda i, j: (i, 0)), wcol, wcol, fcol, fcol] + cast_in,
        out_specs=[pl.BlockSpec((tm, tn), lambda i, j: (i, j)), tail, tail] + cast_out,
        out_shape=[jax.ShapeDtypeStruct((m, FF_PAD), BF16),
                   jax.ShapeDtypeStruct((m // tm * HIST_F, FF_PAD), F32),
                   jax.ShapeDtypeStruct((m // tm * HIST_F, FF_PAD), F32)] + cast_shape,
        scratch_shapes=[pltpu.VMEM((nj, HIST_F, tn), F32), pltpu.VMEM((nj, HIST_F, tn), F32)],
        compiler_params=_params("arbitrary", "arbitrary"),
        name="up_prompt",
    )(xn, wg, wu, fwg, fwu, *(next_sources or ()))
    return res[0], res[1], res[2], tuple(res[3:])


def _down_kernel(g_ref, w_ref, h_ref, gn_ref, o_ref, acc):
    j = pl.program_id(1)
    tn = w_ref.shape[1]
    acc[:, pl.ds(pl.multiple_of(j * tn, tn), tn)] = _dot(g_ref[...], w_ref[...])

    @pl.when(j == pl.num_programs(1) - 1)
    def _():
        o_ref[...] = h_ref[...] + _rms(acc[...], gn_ref[...])


def down(g, w, h, gn, l):
    m, k = g.shape
    tm, tn = TM, TN_DOWN
    return pl.pallas_call(
        _down_kernel,
        grid=(m // tm, D_MODEL // tn),
        in_specs=[pl.BlockSpec((tm, k), lambda i, j: (i, 0)),
                  pl.BlockSpec((k, tn), lambda i, j: (0, j)),
                  pl.BlockSpec((tm, D_MODEL), lambda i, j: (i, 0)),
                  _layer_param(gn, l)],
        out_specs=pl.BlockSpec((tm, D_MODEL), lambda i, j: (i, 0)),
        out_shape=jax.ShapeDtypeStruct((m, D_MODEL), F32),
        scratch_shapes=[pltpu.VMEM((tm, D_MODEL), F32)],
        compiler_params=_params("parallel", "arbitrary"),
        name="down",
    )(g, w, h, gn)


def _ffn_sample_kernel(xn_ref, wg_ref, wu_ref, fwg_ref, fwu_ref, p0g_ref, p1g_ref, p0u_ref, p1u_ref,
                       wd_ref, h_ref, gn_ref, o_ref, tg_ref, tu_ref):
    nb = DEC_BATCH
    j = pl.program_id(0)

    @pl.when(j == 0)
    def _():
        o_ref[...] = jnp.zeros(o_ref.shape, F32)

    xn = xn_ref[...]
    hg = _dot(xn, wg_ref[...])
    hu = _dot(xn, wu_ref[...])
    tg_ref[...] = hg[(DEC_SEQ - 2) * nb:, :]
    tu_ref[...] = hu[(DEC_SEQ - 2) * nb:, :]
    tn = wg_ref.shape[1]
    valid = j * tn + lax.broadcasted_iota(jnp.int32, (nb, tn), 1) < D_FF
    plane = lambda ref: jnp.where(valid, ref[...], 0.0)
    ext_g = [plane(p0g_ref), plane(p1g_ref)] + [hg[t * nb:(t + 1) * nb] for t in range(DEC_SEQ)]
    ext_u = [plane(p0u_ref), plane(p1u_ref)] + [hu[t * nb:(t + 1) * nb] for t in range(DEC_SEQ)]
    for t in range(DEC_SEQ):
        cg = fwg_ref[0:1, :] * ext_g[t] + fwg_ref[1:2, :] * ext_g[t + 1] + fwg_ref[2:3, :] * ext_g[t + 2]
        cu = fwu_ref[0:1, :] * ext_u[t] + fwu_ref[1:2, :] * ext_u[t + 1] + fwu_ref[2:3, :] * ext_u[t + 2]
        ts = slice(t * nb, (t + 1) * nb)
        o_ref[ts, :] += _dot(_gate(cg, cu), wd_ref[...])

    @pl.when(j == pl.num_programs(0) - 1)
    def _():
        o_ref[...] = h_ref[...] + _rms(o_ref[...], gn_ref[...])


def ffn_sample(xn, wg, wu, fwg, fwu, prev, wd, h, gn, l):
    m, k = xn.shape
    tn = TN_UP
    nj = FF_PAD // tn
    col = lambda r: pl.BlockSpec((r, tn), lambda j: (0, j))
    wcol = pl.BlockSpec((k, tn), lambda j: (0, j))
    fcol = pl.BlockSpec((None, HIST_F, tn), lambda j: (l, 0, j))
    prev_row = lambda r, half: pl.BlockSpec((None, None, None, DEC_BATCH, tn), lambda j: (l, r, half, 0, j))
    whole = lambda r, c: pl.BlockSpec((r, c), lambda j: (0, 0))
    nt = (FFN_CONV_WIDTH - 1) * DEC_BATCH
    return pl.pallas_call(
        _ffn_sample_kernel,
        grid=(nj,),
        in_specs=[whole(m, k), wcol, wcol, fcol, fcol, prev_row(0, 0), prev_row(1, 0), prev_row(0, 1), prev_row(1, 1),
                  pl.BlockSpec((tn, D_MODEL), lambda j: (j, 0)), whole(m, D_MODEL), _layer_param(gn, l)],
        out_specs=[whole(m, D_MODEL), col(nt), col(nt)],
        out_shape=[jax.ShapeDtypeStruct((m, D_MODEL), F32),
                   jax.ShapeDtypeStruct((nt, FF_PAD), F32),
                   jax.ShapeDtypeStruct((nt, FF_PAD), F32)],
        compiler_params=_params("arbitrary"),
        name="ffn_sample",
    )(xn, wg, wu, fwg, fwu, prev, prev, prev, prev, wd, h, gn)


def kernel(x_prompt, x_sample, cache_mem_k, cache_mem_v, state_conv_b, state_pool, state_sconv, state_ffn_conv, mem_prompt, g_mix_pre, g_mix_post, g_mem, g_x_pre, g_x_post, g_ffn_pre, g_ffn_post, w_in, w_out, a_norm_g, a_ws, a_bs, b_conv_w, b_conv_b, b_gn_g, b_gn_b, c_lin, c_scale, d_conv_w, w_xq, w_xk, w_xv, w_xo, w_up, f_conv_w, w_down):
    nb, ns = DEC_BATCH, DEC_SEQ
    w_sources = (w_in, w_out, w_xq, w_xo, w_up, w_up, w_down)
    w_kv_b = cast_weight_pair(w_xk, w_xv)
    layer_w = cast_layer(w_sources, 0)

    hp = x_prompt.reshape(BATCH * SEQ, D_MODEL)
    hs = jnp.transpose(x_sample, (1, 0, 2)).reshape(ns * nb, D_MODEL)
    mem = mem_prompt.reshape(BATCH * N_MEM, D_MODEL)
    st_b = state_conv_b.reshape(DEPTH, nb, -1)
    st_p = state_pool.reshape(DEPTH, nb, -1)
    st_s = state_sconv.reshape(DEPTH, nb, -1)

    row = lambda a: a.reshape(DEPTH, 1, -1)
    ag, bb, gng, gnb, cs = (row(a) for a in (a_norm_g, b_conv_b, b_gn_g, b_gn_b, c_scale))
    gpre, gmp, gxp, gxo, gfp, gfo = (row(a) for a in (g_mix_pre, g_mix_post, g_x_pre, g_x_post, g_ffn_pre,
                                                      g_ffn_post))
    abt = jnp.transpose(a_bs, (0, 2, 1))
    pad_rows = lambda a, rows: jnp.pad(a, ((0, 0), (0, rows - a.shape[1]), (0, 0)))
    bw = pad_rows(b_conv_w, HIST_B)
    dw = pad_rows(d_conv_w, HIST_D)
    fw = jnp.pad(f_conv_w.reshape(DEPTH, FFN_CONV_WIDTH, 2, D_FF),
                 ((0, 0), (0, HIST_F - FFN_CONV_WIDTH), (0, 0), (0, FF_PAD - D_FF)))
    fwg, fwu = fw[:, :, 0], fw[:, :, 1]
    aw4 = jnp.repeat(jnp.transpose(a_ws[:, :, :ns, :ns], (0, 2, 3, 1)).reshape(DEPTH, ns * ns, N_SUB), D_SUB, axis=2)
    ab4 = jnp.repeat(jnp.transpose(a_bs[:, :, :ns], (0, 2, 1)), D_SUB, axis=2)
    keep = FFN_CONV_WIDTH - 1
    prev_f = jnp.transpose(state_ffn_conv.reshape(DEPTH, nb, keep, 2, D_FF), (0, 2, 3, 1, 4))

    mk, mv = memory_kv(mem, row(g_mem), w_kv_b, TM)

    p_states, s_states = [], []
    for l in range(DEPTH):
        w_in_b, w_out_b, w_xq_b, w_xo_b, w_g_b, w_u_b, w_down_b = layer_w
        y, nbp, npp, nsp = inmix_prompt(hp.reshape(BATCH, SEQ, D_MODEL), gpre, w_in_b, ag, a_ws, abt, bw, bb,
                                        gng, gnb, c_lin, cs, dw, l)
        h2, xn = mid_prompt(y.reshape(BATCH * SEQ, D_MODEL), hp, w_out_b, gmp, gxp, w_xq_b, mk, mv, w_xo_b, gxo,
                            gfp, l)
        gate, tg, tu, layer_w = up_prompt(xn, w_g_b, w_u_b, fwg, fwu, l, w_sources if l + 1 < DEPTH else None)
        hp = down(gate, w_down_b, h2, gfo, l)
        p_states.append((nbp, npp, nsp, tg, tu))

        zs = norm_matmul(hs, gpre, w_in_b, l, TM, TN_IN)
        ys, nbs, nps, nss, vs = mixer_sample(zs.reshape(ns, nb, D_IN), st_b, st_p, st_s, ag, aw4, ab4, bw, bb,
                                             gng, gnb, c_lin, cs, dw, l)
        h1s, qs = mid_a(ys.reshape(ns * nb, D_MODEL), hs, w_out_b, gmp, gxp, w_xq_b, l)
        q8 = jnp.pad(jnp.transpose(qs.reshape(ns, nb, D_X), (1, 0, 2)), ((0, 0), (0, V7X_SUBLANES - ns), (0, 0)))
        o8 = attn_sample(q8, cache_mem_k, cache_mem_v, l)
        os_ = jnp.transpose(o8[:, :ns], (1, 0, 2)).reshape(ns * nb, D_X)
        h2s, xns = mid_c(os_, h1s, w_xo_b, gxo, gfp, l)
        hs, tgs, tus = ffn_sample(xns, w_g_b, w_u_b, fwg, fwu, prev_f, w_down_b, h2s, gfo, l)
        s_states.append((nbs, nps, nss, tgs, tus, vs))

    nbp, npp, nsp, tg, tu = (jnp.stack(a) for a in zip(*p_states))
    nbs, nps, nss, tgs, tus, vs = (jnp.stack(a) for a in zip(*s_states))
    last = lambda a: a.reshape(DEPTH, BATCH, -1, HIST_F, FF_PAD)[:, :, -1, HIST_F - keep:, :D_FF]
    bmajor = lambda a: jnp.transpose(a.reshape(DEPTH, keep, nb, FF_PAD)[..., :D_FF], (0, 2, 1, 3))
    return (hp.reshape(BATCH, SEQ, D_MODEL),
            jnp.transpose(hs.reshape(ns, nb, D_MODEL), (1, 0, 2)),
            mk.reshape(DEPTH, BATCH, N_MEM, D_X),
            mv.reshape(DEPTH, BATCH, N_MEM, D_X),
            nbp[:, :, HIST_B - (CONV_B_WIDTH - 1):],
            npp[:, :, HIST_C - POOL_PREV:],
            nsp[:, :, HIST_D - (SCONV_WIDTH - 1):],
            jnp.concatenate([last(tg), last(tu)], axis=-1),
            nbs.reshape(DEPTH, nb, CONV_B_WIDTH - 1, D_GROUP),
            nps.reshape(DEPTH, nb, POOL_PREV, D_GROUP),
            nss.reshape(DEPTH, nb, SCONV_WIDTH - 1, D_GROUP),
            jnp.concatenate([bmajor(tgs), bmajor(tus)], axis=-1),
            vs.reshape(DEPTH, nb, ns, D_GROUP))
```

```python
import functools

import jax
import jax.numpy as jnp
from jax import lax
from jax.experimental import pallas as pl
from jax.experimental.pallas import tpu as pltpu

F32 = jnp.float32
BF16 = jnp.bfloat16

D_MODEL = 2048
BATCH = 4
SEQ = 2048
DEPTH = 4
DEC_BATCH = 128
DEC_SEQ = 4
PAST_LEN = 16384
D_GROUP = 512
N_SUB = 4
D_SUB = 128
D_IN = 8 * D_GROUP
CHUNK = 128
CONV_B_WIDTH = 31
POOL_WINDOWS = (2, 4, 8, 16)
POOL_PREV = 15
SCONV_WIDTH = 3
FFN_CONV_WIDTH = 3
D_FF = 5504
N_MEM = 256
N_XHEADS = 4
D_XHEAD = 128
D_X = 512
EPS = 1e-6

V7X_SUBLANES = 8
V7X_LANES = 128
V7X_VMEM_LIMIT_BYTES = 56 * 1024 * 1024

HIST_B = 32
HIST_C = 16
HIST_D = 8
HIST_F = 8

TM = 512
TM_UP = 2048
SUBS_UP = (512, 512, 512, 512)
TM_IN = 1024
SUB_MID = 256
TT = 512
ROWS = 32
TN_IN = 1024
TN_UP = 512
FF_PAD = -(-D_FF // TN_UP) * TN_UP
TN_DOWN = 512
CAST_CHUNKS = 32
BB_MIX = 32
BB_ATT = 16


def _layer_param(a, l):
    tail = a.shape[1:]
    return pl.BlockSpec((None,) + tail, lambda *_: (l,) + (0,) * len(tail))


def _params(*sem):
    return pltpu.CompilerParams(dimension_semantics=sem, vmem_limit_bytes=V7X_VMEM_LIMIT_BYTES)


def _rms(x, g):
    return x * lax.rsqrt(jnp.mean(x * x, axis=-1, keepdims=True) + EPS) * g


def _sigmoid(x):
    return 1.0 / (1.0 + jnp.exp(-x))


def _dot(a, b):
    return jnp.dot(a, b, preferred_element_type=F32)


N_CAST = 7


def _cast_specs(layer, chunk):
    def pair(rows_total, cols, col_block=0, out_rows=None, out_cols=None):
        rows = (out_rows or rows_total) // CAST_CHUNKS
        src = pl.BlockSpec((None, rows, cols), lambda *g: (layer, chunk(*g), col_block))
        dst = pl.BlockSpec((rows, out_cols or cols), lambda *g: (chunk(*g), 0))
        return src, dst, jax.ShapeDtypeStruct((out_rows or rows_total, out_cols or cols), BF16)
    table = [pair(D_MODEL, D_IN), pair(D_MODEL, D_MODEL), pair(D_MODEL, D_X), pair(D_X, D_MODEL),
             pair(D_MODEL, D_FF, 0, out_cols=FF_PAD), pair(D_MODEL, D_FF, 1, out_cols=FF_PAD),
             pair(D_FF, D_MODEL, out_rows=FF_PAD)]
    return tuple(list(t) for t in zip(*table))


def _cast_chunk(c, src, dst):
    for s, d in zip(src[:4], dst[:4]):
        d[...] = s[...].astype(BF16)
    for s, d in zip(src[4:6], dst[4:6]):
        d[:, 0:D_FF] = s[...].astype(BF16)
        d[:, D_FF:] = jnp.zeros((d.shape[0], d.shape[1] - D_FF), BF16)
    rows = dst[6].shape[0]
    row = c * rows + lax.broadcasted_iota(jnp.int32, dst[6].shape, 0)
    dst[6][...] = jnp.where(row < D_FF, src[6][...], 0.0).astype(BF16)


def _cast_layer_kernel(*refs):
    _cast_chunk(pl.program_id(0), refs[:N_CAST], refs[N_CAST:])


def cast_layer(sources, layer):
    in_specs, out_specs, out_shape = _cast_specs(layer, lambda c: c)
    return pl.pallas_call(
        _cast_layer_kernel,
        grid=(CAST_CHUNKS,),
        in_specs=in_specs,
        out_specs=out_specs,
        out_shape=out_shape,
        compiler_params=_params("parallel"),
        name="cast_layer",
    )(*sources)


def _cast_pair_kernel(a_ref, b_ref, o_ref):
    n = a_ref.shape[-1]
    o_ref[:, 0:n] = a_ref[...].astype(BF16)
    o_ref[:, n:2 * n] = b_ref[...].astype(BF16)


def cast_weight_pair(a, b):
    depth, r, c = a.shape
    spec = pl.BlockSpec((None, r, c), lambda l: (l, 0, 0))
    return pl.pallas_call(
        _cast_pair_kernel,
        grid=(depth,),
        in_specs=[spec, spec],
        out_specs=pl.BlockSpec((None, r, 2 * c), lambda l: (l, 0, 0)),
        out_shape=jax.ShapeDtypeStruct((depth, r, 2 * c), BF16),
        compiler_params=_params("parallel"),
        name="cast_weight_pair",
    )(a, b)


def _norm_matmul_kernel(x_ref, g_ref, w_ref, o_ref, xn_ref):
    @pl.when(pl.program_id(1) == 0)
    def _():
        xn_ref[...] = _rms(x_ref[...], g_ref[...]).astype(BF16)

    o_ref[...] = _dot(xn_ref[...], w_ref[...])


def norm_matmul(x, g, w, l, tm, tn):
    m, k = x.shape
    n = w.shape[1]
    return pl.pallas_call(
        _norm_matmul_kernel,
        grid=(m // tm, n // tn),
        in_specs=[pl.BlockSpec((tm, k), lambda i, j: (i, 0)),
                  _layer_param(g, l),
                  pl.BlockSpec((k, tn), lambda i, j: (0, j))],
        out_specs=pl.BlockSpec((tm, tn), lambda i, j: (i, j)),
        out_shape=jax.ShapeDtypeStruct((m, n), F32),
        scratch_shapes=[pltpu.VMEM((tm, k), BF16)],
        compiler_params=_params("parallel", "arbitrary"),
        name="norm_matmul",
    )(x, g, w)


def _memory_kv_kernel(x_ref, g_ref, w_ref, k_ref, v_ref):
    kv = _dot(_rms(x_ref[...], g_ref[...]).astype(BF16), w_ref[...])
    n = k_ref.shape[-1]
    k_ref[...] = kv[:, 0:n]
    v_ref[...] = kv[:, n:2 * n]


def memory_kv(x, g, w, tm):
    m, k = x.shape
    depth, _, n2 = w.shape
    half = pl.BlockSpec((None, tm, n2 // 2), lambda l, i: (l, i, 0))
    shape = jax.ShapeDtypeStruct((depth, m, n2 // 2), F32)
    return pl.pallas_call(
        _memory_kv_kernel,
        grid=(depth, m // tm),
        in_specs=[pl.BlockSpec((tm, k), lambda l, i: (i, 0)),
                  pl.BlockSpec((None, 1, k), lambda l, i: (l, 0, 0)),
                  pl.BlockSpec((None, k, n2), lambda l, i: (l, 0, 0))],
        out_specs=[half, half],
        out_shape=[shape, shape],
        compiler_params=_params("parallel", "parallel"),
        name="memory_kv",
    )(x, g, w)


def _layer_norm(v, g):
    mu = jnp.mean(v, axis=-1, keepdims=True)
    d = v - mu
    var = jnp.mean(d * d, axis=-1, keepdims=True)
    return d * lax.rsqrt(var + EPS) * g


def _group_norm_silu(y, g, b):
    mu = jnp.mean(y, axis=-1, keepdims=True)
    d = y - mu
    var = jnp.mean(d * d, axis=-1, keepdims=True)
    yn = d * lax.rsqrt(var + EPS) * g + b
    return yn * _sigmoid(yn)


def _inmix_prompt_kernel(x_ref, gpre_ref, win_ref, ag_ref, aws_ref, abt_ref, bw_ref, bb_ref, gng_ref, gnb_ref,
                         clin_ref, cs_ref, dw_ref,
                         y_ref, nb_ref, np_ref, ns_ref,
                         extb, extc, extd, pooled):
    t = pl.program_id(1)
    tt = y_ref.shape[0]
    quarter = 2 * D_GROUP

    @pl.when(t == 0)
    def _():
        for s in range(V7X_SUBLANES):
            extb[s, 0:HIST_B, :] = jnp.zeros((HIST_B, D_GROUP), F32)
            extb[s, tt + HIST_B - V7X_SUBLANES:tt + HIST_B, :] = jnp.zeros((V7X_SUBLANES, D_GROUP), F32)
        extc[0:HIST_C, :] = jnp.zeros((HIST_C, D_GROUP), F32)
        extd[0:HIST_D, :] = jnp.zeros((HIST_D, D_GROUP), F32)

    xn = _rms(x_ref[...], gpre_ref[...]).astype(BF16)
    z = {}
    proj = lambda q: _dot(xn, win_ref[:, q * quarter:(q + 1) * quarter])

    def zcol(k, rs=slice(None)):
        return z[k // 2][rs, (k % 2) * D_GROUP:(k % 2 + 1) * D_GROUP]

    z[1] = proj(1)
    hb = zcol(2) * _sigmoid(zcol(3))
    for s in range(V7X_SUBLANES):
        extb[s, HIST_B - s:HIST_B - s + tt, :] = hb
    n_chunks = tt // ROWS
    later = {0: 0, n_chunks // 3: 2, 2 * n_chunks // 3: 3}
    for c in range(n_chunks):
        if c in later:
            z[later[c]] = proj(later[c])
        for g in range(N_SUB):
            ls = slice(g * D_SUB, (g + 1) * D_SUB)
            acc = jnp.zeros((ROWS, D_SUB), F32)
            for k in range(CONV_B_WIDTH):
                tiles, s = divmod(HIST_B - (CONV_B_WIDTH - 1) + k, V7X_SUBLANES)
                r0 = c * ROWS + tiles * V7X_SUBLANES
                acc = acc + bw_ref[k:k + 1, ls] * extb[s, r0:r0 + ROWS, ls]
            yb = _group_norm_silu(acc + bb_ref[:, ls], gng_ref[:, ls], gnb_ref[:, ls])
            y_ref[c * ROWS:(c + 1) * ROWS, D_GROUP + g * D_SUB:D_GROUP + (g + 1) * D_SUB] = yb.astype(BF16)

    vn = _layer_norm(zcol(1), ag_ref[...]).astype(BF16)
    row = lax.broadcasted_iota(jnp.int32, (CHUNK, CHUNK), 0)
    col = lax.broadcasted_iota(jnp.int32, (CHUNK, CHUNK), 1)
    for h in range(N_SUB):
        wm = jnp.where(row >= col, aws_ref[h], 0.0).astype(BF16)
        bias = abt_ref[:, h:h + 1]
        for c in range(tt // CHUNK):
            rs = slice(c * CHUNK, (c + 1) * CHUNK)
            ls = slice(h * D_SUB, (h + 1) * D_SUB)
            zz = _dot(wm, vn[rs, ls]) + bias
            y_ref[rs, ls] = (z[0][rs, ls] * zz).astype(BF16)

    extc[HIST_C:HIST_C + tt, :] = zcol(4)
    for c in range(tt // ROWS):
        pos = t * tt + c * ROWS + lax.broadcasted_iota(jnp.int32, (ROWS, 1), 0)
        for g, win in enumerate(POOL_WINDOWS):
            ls = slice(g * D_SUB, (g + 1) * D_SUB)
            r0 = c * ROWS + HIST_C
            x = extc[r0:r0 + ROWS, ls]
            s = x
            for i in range(1, win):
                s = s + extc[r0 - i:r0 - i + ROWS, ls]
            cnt = jnp.minimum(pos + 1, win).astype(F32)
            pooled[c * ROWS:(c + 1) * ROWS, ls] = (s / cnt - x).astype(BF16)
    for g in range(N_SUB):
        ls = slice(g * D_SUB, (g + 1) * D_SUB)
        yc = _dot(pooled[:, ls], clin_ref[g].astype(BF16)) * cs_ref[:, ls]
        y_ref[:, 2 * D_GROUP + g * D_SUB:2 * D_GROUP + (g + 1) * D_SUB] = yc.astype(BF16)

    extd[HIST_D:HIST_D + tt, :] = zcol(7) * zcol(5)
    for c in range(tt // ROWS):
        r0 = c * ROWS + HIST_D
        conv = (dw_ref[0:1, :] * extd[r0 - 2:r0 - 2 + ROWS, :]
                + dw_ref[1:2, :] * extd[r0 - 1:r0 - 1 + ROWS, :]
                + dw_ref[2:3, :] * extd[r0:r0 + ROWS, :])
        rs = slice(c * ROWS, (c + 1) * ROWS)
        y_ref[rs, 3 * D_GROUP:4 * D_GROUP] = (zcol(6, rs) * conv).astype(BF16)

    for s in range(V7X_SUBLANES):
        extb[s, 0:HIST_B, :] = extb[s, tt:tt + HIST_B, :]
    extc[0:HIST_C, :] = extc[tt:tt + HIST_C, :]
    extd[0:HIST_D, :] = extd[tt:tt + HIST_D, :]

    @pl.when(t == pl.num_programs(1) - 1)
    def _():
        nb_ref[...] = extb[0, 0:HIST_B, :]
        np_ref[...] = extc[0:HIST_C, :]
        ns_ref[...] = extd[0:HIST_D, :]


def inmix_prompt(x, gpre, win, ag, aws, abt, bw, bb, gng, gnb, clin, cs, dw, l):
    tt = TT
    full = lambda a: _layer_param(a, l)
    hist = lambda r: pl.BlockSpec((None, r, D_GROUP), lambda b, t: (b, 0, 0))
    return pl.pallas_call(
        _inmix_prompt_kernel,
        grid=(BATCH, SEQ // tt),
        in_specs=[pl.BlockSpec((None, tt, D_MODEL), lambda b, t: (b, t, 0)),
                  full(gpre),
                  pl.BlockSpec((D_MODEL, D_IN), lambda b, t: (0, 0), pipeline_mode=pl.Buffered(1)),
                  full(ag), full(aws), full(abt), full(bw), full(bb), full(gng), full(gnb),
                  full(clin), full(cs), full(dw)],
        out_specs=[pl.BlockSpec((None, tt, D_MODEL), lambda b, t: (b, t, 0)),
                   hist(HIST_B), hist(HIST_C), hist(HIST_D)],
        out_shape=[jax.ShapeDtypeStruct((BATCH, SEQ, D_MODEL), BF16),
                   jax.ShapeDtypeStruct((BATCH, HIST_B, D_GROUP), F32),
                   jax.ShapeDtypeStruct((BATCH, HIST_C, D_GROUP), F32),
                   jax.ShapeDtypeStruct((BATCH, HIST_D, D_GROUP), F32)],
        scratch_shapes=[pltpu.VMEM((V7X_SUBLANES, HIST_B + tt, D_GROUP), F32),
                        pltpu.VMEM((HIST_C + tt, D_GROUP), F32),
                        pltpu.VMEM((HIST_D + tt, D_GROUP), F32),
                        pltpu.VMEM((tt, D_GROUP), BF16)],
        compiler_params=_params("parallel", "arbitrary"),
        name="inmix_prompt",
    )(x, gpre, win, ag, aws, abt, bw, bb, gng, gnb, clin, cs, dw)


def _mixer_sample_kernel(z_ref, cb_ref, cp_ref, csc_ref, ag_ref, aw4_ref, ab4_ref, bw_ref, bb_ref,
                         gng_ref, gnb_ref, clin_ref, cs_ref, dw_ref,
                         y_ref, nb_ref, np_ref, nsc_ref, v_ref, pooled):
    bblk = z_ref.shape[1]
    blk = lambda r: slice(r * D_GROUP, (r + 1) * D_GROUP)

    vn = [_layer_norm(z_ref[t, :, D_GROUP:2 * D_GROUP], ag_ref[...]) for t in range(DEC_SEQ)]
    for i in range(DEC_SEQ):
        v_ref[:, blk(i)] = vn[i]
        zz = ab4_ref[i:i + 1, :]
        for j in range(i + 1):
            zz = zz + aw4_ref[i * DEC_SEQ + j:i * DEC_SEQ + j + 1, :] * vn[j]
        y_ref[i, :, 0:D_GROUP] = (z_ref[i, :, 0:D_GROUP] * zz).astype(BF16)

    nprev = CONV_B_WIDTH - 1
    hb = [z_ref[t, :, 2 * D_GROUP:3 * D_GROUP] * _sigmoid(z_ref[t, :, 3 * D_GROUP:4 * D_GROUP])
          for t in range(DEC_SEQ)]
    ext_b = lambda r: cb_ref[:, blk(r)] if r < nprev else hb[r - nprev]
    for t in range(DEC_SEQ):
        acc = jnp.zeros((bblk, D_GROUP), F32)
        for k in range(CONV_B_WIDTH):
            acc = acc + bw_ref[k:k + 1, :] * ext_b(t + k)
        acc = acc + bb_ref[...]
        for g in range(N_SUB):
            ls = slice(g * D_SUB, (g + 1) * D_SUB)
            yb = _group_norm_silu(acc[:, ls], gng_ref[:, ls], gnb_ref[:, ls])
            y_ref[t, :, D_GROUP + g * D_SUB:D_GROUP + (g + 1) * D_SUB] = yb.astype(BF16)
    for r in range(nprev):
        nb_ref[:, blk(r)] = ext_b(r + DEC_SEQ)

    cx = [z_ref[t, :, 4 * D_GROUP:5 * D_GROUP] for t in range(DEC_SEQ)]
    ext_c = lambda r: cp_ref[:, blk(r)] if r < POOL_PREV else cx[r - POOL_PREV]
    for t in range(DEC_SEQ):
        for g, win in enumerate(POOL_WINDOWS):
            ls = slice(g * D_SUB, (g + 1) * D_SUB)
            s = cx[t][:, ls]
            for i in range(1, win):
                s = s + ext_c(POOL_PREV + t - i)[:, ls]
            cnt = float(min(PAST_LEN + t + 1, win))
            pooled[t * bblk:(t + 1) * bblk, ls] = (s / cnt - cx[t][:, ls]).astype(BF16)
    for g in range(N_SUB):
        ls = slice(g * D_SUB, (g + 1) * D_SUB)
        yc = _dot(pooled[:, ls], clin_ref[g].astype(BF16)) * cs_ref[:, ls]
        for t in range(DEC_SEQ):
            y_ref[t, :, 2 * D_GROUP + g * D_SUB:2 * D_GROUP + (g + 1) * D_SUB] = (
                yc[t * bblk:(t + 1) * bblk].astype(BF16))
    for r in range(POOL_PREV):
        np_ref[:, blk(r)] = ext_c(r + DEC_SEQ)

    nsp = SCONV_WIDTH - 1
    hd = [z_ref[t, :, 7 * D_GROUP:8 * D_GROUP] * z_ref[t, :, 5 * D_GROUP:6 * D_GROUP] for t in range(DEC_SEQ)]
    ext_d = lambda r: csc_ref[:, blk(r)] if r < nsp else hd[r - nsp]
    for t in range(DEC_SEQ):
        conv = dw_ref[0:1, :] * ext_d(t) + dw_ref[1:2, :] * ext_d(t + 1) + dw_ref[2:3, :] * ext_d(t + 2)
        y_ref[t, :, 3 * D_GROUP:4 * D_GROUP] = (z_ref[t, :, 6 * D_GROUP:7 * D_GROUP] * conv).astype(BF16)
    for r in range(nsp):
        nsc_ref[:, blk(r)] = ext_d(r + DEC_SEQ)


N_MIXER_SAMPLE_IN = 14


def _mixer_sample_stacked_kernel(*refs):
    n = N_MIXER_SAMPLE_IN
    _mixer_sample_kernel(*refs[:n], *refs[n + 4:])


def mixer_sample(z, cb, cp, csc, ag, aw4, ab4, bw, bb, gng, gnb, clin, cs, dw, l, stacks):
    bblk = BB_MIX
    full = lambda a: _layer_param(a, l)
    st = lambda a: pl.BlockSpec((None, bblk, a.shape[2]), lambda b: (l, b, 0))
    untouched = pl.BlockSpec(memory_space=pl.ANY)
    res = pl.pallas_call(
        _mixer_sample_stacked_kernel,
        grid=(DEC_BATCH // bblk,),
        in_specs=[pl.BlockSpec((DEC_SEQ, bblk, D_IN), lambda b: (0, b, 0)),
                  st(cb), st(cp), st(csc),
                  full(ag), full(aw4), full(ab4), full(bw), full(bb), full(gng), full(gnb),
                  full(clin), full(cs), full(dw)] + [untouched] * len(stacks),
        out_specs=[pl.BlockSpec((DEC_SEQ, bblk, D_MODEL), lambda b: (0, b, 0))] + [st(a) for a in stacks],
        out_shape=[jax.ShapeDtypeStruct((DEC_SEQ, DEC_BATCH, D_MODEL), BF16)]
                  + [jax.ShapeDtypeStruct(a.shape, F32) for a in stacks],
        input_output_aliases={N_MIXER_SAMPLE_IN + k: 1 + k for k in range(len(stacks))},
        scratch_shapes=[pltpu.VMEM((DEC_SEQ * bblk, D_GROUP), BF16)],
        compiler_params=_params("parallel"),
        name="mixer_sample",
    )(z, cb, cp, csc, ag, aw4, ab4, bw, bb, gng, gnb, clin, cs, dw, *stacks)
    return res[0], tuple(res[1:])


def _stage_a(y, h, wout_ref, gpost_ref, gpre_ref, wxq_ref):
    h1 = h + _rms(_dot(y, wout_ref[...]), gpost_ref[...])
    q = _dot(_rms(h1, gpre_ref[...]).astype(BF16), wxq_ref[...])
    return h1, q


def _stage_c(o, h1, wxo_ref, gpost_ref, gffn_ref):
    h2 = h1 + _rms(_dot(o, wxo_ref[...]), gpost_ref[...])
    return h2, _rms(h2, gffn_ref[...]).astype(BF16)


def _softmax_rows(s):
    e = jnp.exp(s - jnp.max(s, axis=-1, keepdims=True))
    return e / jnp.sum(e, axis=-1, keepdims=True)


_NT = (((1,), (1,)), ((), ()))


def _mid_prompt_kernel(y_ref, h_ref, wout_ref, gmp_ref, gxp_ref, wxq_ref, mk_ref, mv_ref, wxo_ref,
                       gxo_ref, gffn_ref, h2_ref, xn_ref):
    k = mk_ref[...].astype(BF16)
    v = mv_ref[...].astype(BF16)
    halves = [slice(r * SUB_MID, (r + 1) * SUB_MID) for r in range(y_ref.shape[0] // SUB_MID)]

    def out_proj(rs):
        return _dot(y_ref[rs, :], wout_ref[...])

    def residual_and_query_in(rs, mix):
        h1 = h_ref[rs, :] + _rms(mix, gmp_ref[...])
        return h1, _rms(h1, gxp_ref[...]).astype(BF16)

    def attention(xq):
        q = _dot(xq, wxq_ref[...]).astype(BF16)
        heads = []
        for hd in range(N_XHEADS):
            ls = slice(hd * D_XHEAD, (hd + 1) * D_XHEAD)
            s = lax.dot_general(q[:, ls], k[:, ls], _NT, preferred_element_type=F32) * (D_XHEAD ** -0.5)
            heads.append(_dot(_softmax_rows(s).astype(BF16), v[:, ls]))
        return _dot(jnp.concatenate(heads, axis=-1).astype(BF16), wxo_ref[...])

    def finish(rs, h1, xa):
        h2 = h1 + _rms(xa, gxo_ref[...])
        h2_ref[rs, :] = h2
        xn_ref[rs, :] = _rms(h2, gffn_ref[...]).astype(BF16)

    mix = [out_proj(rs) for rs in halves]
    state = [residual_and_query_in(halves[0], mix[0])]
    xa = []
    for r, rs in enumerate(halves):
        xa.append(attention(state[r][1]))
        if r + 1 < len(halves):
            state.append(residual_and_query_in(halves[r + 1], mix[r + 1]))
        if r > 0:
            finish(halves[r - 1], state[r - 1][0], xa[r - 1])
    finish(halves[-1], state[-1][0], xa[-1])


def _resident(w):
    return pl.BlockSpec(w.shape, lambda i: (0, 0), pipeline_mode=pl.Buffered(1))


def mid_prompt(y, h, wout, gmp, gxp, wxq, mk, mv, wxo, gxo, gffn, l):
    m = y.shape[0]
    tm = TM
    per_seq = SEQ // tm
    rows = lambda c: pl.BlockSpec((tm, c), lambda i: (i, 0))
    full = lambda a: _layer_param(a, l)
    mem = pl.BlockSpec((None, N_MEM, D_X), lambda i: (l, i // per_seq, 0))
    return pl.pallas_call(
        _mid_prompt_kernel,
        grid=(m // tm,),
        in_specs=[rows(D_MODEL), rows(D_MODEL), _resident(wout), full(gmp), full(gxp),
                  _resident(wxq), mem, mem, _resident(wxo), full(gxo), full(gffn)],
        out_specs=[rows(D_MODEL), rows(D_MODEL)],
        out_shape=[jax.ShapeDtypeStruct((m, D_MODEL), F32), jax.ShapeDtypeStruct((m, D_MODEL), BF16)],
        compiler_params=_params("parallel"),
        name="mid_prompt",
    )(y, h, wout, gmp, gxp, wxq, mk, mv, wxo, gxo, gffn)


def _mid_a_kernel(y_ref, h_ref, wout_ref, gmp_ref, gxp_ref, wxq_ref, h1_ref, q_ref):
    h1, q = _stage_a(y_ref[...], h_ref[...], wout_ref, gmp_ref, gxp_ref, wxq_ref)
    h1_ref[...] = h1
    q_ref[...] = q.astype(BF16)


def mid_a(y, h, wout, gmp, gxp, wxq, l):
    m = y.shape[0]
    tm = TM
    rows = lambda c: pl.BlockSpec((tm, c), lambda i: (i, 0))
    full = lambda a: _layer_param(a, l)
    return pl.pallas_call(
        _mid_a_kernel,
        grid=(m // tm,),
        in_specs=[rows(D_MODEL), rows(D_MODEL), _resident(wout), full(gmp), full(gxp),
                  _resident(wxq)],
        out_specs=[rows(D_MODEL), rows(D_X)],
        out_shape=[jax.ShapeDtypeStruct((m, D_MODEL), F32), jax.ShapeDtypeStruct((m, D_X), BF16)],
        compiler_params=_params("parallel"),
        name="mid_a",
    )(y, h, wout, gmp, gxp, wxq)


def _mid_c_kernel(o_ref, h1_ref, wxo_ref, gxo_ref, gffn_ref, h2_ref, xn_ref):
    h2, xn = _stage_c(o_ref[...], h1_ref[...], wxo_ref, gxo_ref, gffn_ref)
    h2_ref[...] = h2
    xn_ref[...] = xn


def mid_c(o, h1, wxo, gxo, gffn, l):
    m = o.shape[0]
    tm = TM
    rows = lambda c: pl.BlockSpec((tm, c), lambda i: (i, 0))
    full = lambda a: _layer_param(a, l)
    return pl.pallas_call(
        _mid_c_kernel,
        grid=(m // tm,),
        in_specs=[rows(D_X), rows(D_MODEL), _resident(wxo), full(gxo), full(gffn)],
        out_specs=[rows(D_MODEL), rows(D_MODEL)],
        out_shape=[jax.ShapeDtypeStruct((m, D_MODEL), F32), jax.ShapeDtypeStruct((m, D_MODEL), BF16)],
        compiler_params=_params("parallel"),
        name="mid_c",
    )(o, h1, wxo, gxo, gffn)


def _attn_sample_kernel(q_ref, k_ref, v_ref, o_ref):
    bblk = q_ref.shape[0]
    lane_head = lax.broadcasted_iota(jnp.int32, (V7X_SUBLANES, D_X), 1) // D_XHEAD

    def scores(b):
        q8 = q_ref[b]
        qbd = jnp.concatenate([jnp.where(lane_head == hd, q8, jnp.zeros_like(q8)) for hd in range(N_XHEADS)], axis=0)
        return lax.dot_general(qbd, k_ref[b].astype(BF16), _NT, preferred_element_type=F32) * (D_XHEAD ** -0.5)

    def attend(b, s):
        of = _dot(_softmax_rows(s).astype(BF16), v_ref[b].astype(BF16))
        o8 = jnp.zeros((V7X_SUBLANES, D_X), F32)
        for hd in range(N_XHEADS):
            o8 = o8 + jnp.where(lane_head == hd, of[hd * V7X_SUBLANES:(hd + 1) * V7X_SUBLANES], 0.0)
        o_ref[b] = o8.astype(BF16)

    s = scores(0)
    for b in range(1, bblk):
        s_next = scores(b)
        attend(b - 1, s)
        s = s_next
    attend(bblk - 1, s)


def attn_sample(q, k, v, l):
    bblk = BB_ATT
    qs = pl.BlockSpec((bblk, V7X_SUBLANES, D_X), lambda b: (b, 0, 0))
    ms = pl.BlockSpec((None, bblk, N_MEM, D_X), lambda b: (l, b, 0, 0))
    return pl.pallas_call(
        _attn_sample_kernel,
        grid=(DEC_BATCH // bblk,),
        in_specs=[qs, ms, ms],
        out_specs=qs,
        out_shape=jax.ShapeDtypeStruct(q.shape, BF16),
        compiler_params=_params("parallel"),
        name="attn_sample",
    )(q, k, v)


def _gate(cg, cu):
    return (cg * _sigmoid(cg) * cu).astype(BF16)


def _shift_conv3(prev, h, fw_ref):
    rows = h.shape[0]
    ext = jnp.concatenate([prev, h], axis=0)
    return (fw_ref[0:1, :] * ext[HIST_F - 2:HIST_F - 2 + rows]
            + fw_ref[1:2, :] * ext[HIST_F - 1:HIST_F - 1 + rows]
            + fw_ref[2:3, :] * h)


def _up_prompt_kernel(*refs, per_seq, subs, cast_next):
    n_cast = N_CAST if cast_next else 0
    xn_ref, wg_ref, wu_ref, fwg_ref, fwu_ref = refs[:5]
    src = refs[5:5 + n_cast]
    o_ref, tg_ref, tu_ref = refs[5 + n_cast:8 + n_cast]
    dst = refs[8 + n_cast:8 + 2 * n_cast]
    carg, caru = refs[8 + 2 * n_cast:]
    i = pl.program_id(0)
    j = pl.program_id(1)

    @pl.when(i % per_seq == 0)
    def _():
        carg[j] = jnp.zeros(carg.shape[1:], F32)
        caru[j] = jnp.zeros(caru.shape[1:], F32)

    if cast_next:
        _cast_chunk(jnp.minimum(i * pl.num_programs(1) + j, CAST_CHUNKS - 1), src, dst)

    prev_g = carg[j]
    prev_u = caru[j]
    r0 = 0
    for sub in subs:
        rs = slice(r0, r0 + sub)
        r0 += sub
        hg = _dot(xn_ref[rs, :], wg_ref[...])
        hu = _dot(xn_ref[rs, :], wu_ref[...])
        o_ref[rs, :] = _gate(_shift_conv3(prev_g, hg, fwg_ref), _shift_conv3(prev_u, hu, fwu_ref))
        prev_g = hg[sub - HIST_F:]
        prev_u = hu[sub - HIST_F:]
    carg[j] = prev_g
    caru[j] = prev_u
    tg_ref[...] = prev_g
    tu_ref[...] = prev_u


def up_prompt(xn, wg, wu, fwg, fwu, l, next_sources=None):
    m, k = xn.shape
    tm, tn = TM_UP, TN_UP
    assert sum(SUBS_UP) == tm
    nj = FF_PAD // tn
    assert (m // tm) * nj >= CAST_CHUNKS
    wcol = pl.BlockSpec((k, tn), lambda i, j: (0, j))
    fcol = pl.BlockSpec((None, HIST_F, tn), lambda i, j: (l, 0, j))
    tail = pl.BlockSpec((HIST_F, tn), lambda i, j: (i, j))
    cast_in, cast_out, cast_shape = ([], [], [])
    if next_sources is not None:
        cast_in, cast_out, cast_shape = _cast_specs(l + 1, lambda i, j: jnp.minimum(i * nj + j, CAST_CHUNKS - 1))
    res = pl.pallas_call(
        functools.partial(_up_prompt_kernel, per_seq=SEQ // tm, subs=SUBS_UP, cast_next=next_sources is not None),
        grid=(m // tm, nj),
        in_specs=[pl.BlockSpec((tm, k), lambda i, j: (i, 0)), wcol, wcol, fcol, fcol] + cast_in,
        out_specs=[pl.BlockSpec((tm, tn), lambda i, j: (i, j)), tail, tail] + cast_out,
        out_shape=[jax.ShapeDtypeStruct((m, FF_PAD), BF16),
                   jax.ShapeDtypeStruct((m // tm * HIST_F, FF_PAD), F32),
                   jax.ShapeDtypeStruct((m // tm * HIST_F, FF_PAD), F32)] + cast_shape,
        scratch_shapes=[pltpu.VMEM((nj, HIST_F, tn), F32), pltpu.VMEM((nj, HIST_F, tn), F32)],
        compiler_params=_params("arbitrary", "arbitrary"),
        name="up_prompt",
    )(xn, wg, wu, fwg, fwu, *(next_sources or ()))
    return res[0], res[1], res[2], tuple(res[3:])


def _down_kernel(g_ref, w_ref, h_ref, gn_ref, o_ref, acc):
    j = pl.program_id(1)
    tn = w_ref.shape[1]
    acc[:, pl.ds(pl.multiple_of(j * tn, tn), tn)] = _dot(g_ref[...], w_ref[...])

    @pl.when(j == pl.num_programs(1) - 1)
    def _():
        o_ref[...] = h_ref[...] + _rms(acc[...], gn_ref[...])


def down(g, w, h, gn, l):
    m, k = g.shape
    tm, tn = TM, TN_DOWN
    return pl.pallas_call(
        _down_kernel,
        grid=(m // tm, D_MODEL // tn),
        in_specs=[pl.BlockSpec((tm, k), lambda i, j: (i, 0)),
                  pl.BlockSpec((k, tn), lambda i, j: (0, j)),
                  pl.BlockSpec((tm, D_MODEL), lambda i, j: (i, 0)),
                  _layer_param(gn, l)],
        out_specs=pl.BlockSpec((tm, D_MODEL), lambda i, j: (i, 0)),
        out_shape=jax.ShapeDtypeStruct((m, D_MODEL), F32),
        scratch_shapes=[pltpu.VMEM((tm, D_MODEL), F32)],
        compiler_params=_params("parallel", "arbitrary"),
        name="down",
    )(g, w, h, gn)


def _ffn_sample_kernel(xn_ref, wg_ref, wu_ref, fwg_ref, fwu_ref, p0g_ref, p1g_ref, p0u_ref, p1u_ref,
                       wd_ref, h_ref, gn_ref, o_ref, tg_ref, tu_ref):
    nb = DEC_BATCH
    j = pl.program_id(0)

    @pl.when(j == 0)
    def _():
        o_ref[...] = jnp.zeros(o_ref.shape, F32)

    xn = xn_ref[...]
    hg = _dot(xn, wg_ref[...])
    hu = _dot(xn, wu_ref[...])
    tg_ref[...] = hg[(DEC_SEQ - 2) * nb:, :]
    tu_ref[...] = hu[(DEC_SEQ - 2) * nb:, :]
    tn = wg_ref.shape[1]
    valid = j * tn + lax.broadcasted_iota(jnp.int32, (nb, tn), 1) < D_FF
    plane = lambda ref: jnp.where(valid, ref[...], 0.0)
    ext_g = [plane(p0g_ref), plane(p1g_ref)] + [hg[t * nb:(t + 1) * nb] for t in range(DEC_SEQ)]
    ext_u = [plane(p0u_ref), plane(p1u_ref)] + [hu[t * nb:(t + 1) * nb] for t in range(DEC_SEQ)]
    for t in range(DEC_SEQ):
        cg = fwg_ref[0:1, :] * ext_g[t] + fwg_ref[1:2, :] * ext_g[t + 1] + fwg_ref[2:3, :] * ext_g[t + 2]
        cu = fwu_ref[0:1, :] * ext_u[t] + fwu_ref[1:2, :] * ext_u[t + 1] + fwu_ref[2:3, :] * ext_u[t + 2]
        ts = slice(t * nb, (t + 1) * nb)
        o_ref[ts, :] += _dot(_gate(cg, cu), wd_ref[...])

    @pl.when(j == pl.num_programs(0) - 1)
    def _():
        o_ref[...] = h_ref[...] + _rms(o_ref[...], gn_ref[...])


def ffn_sample(xn, wg, wu, fwg, fwu, prev, wd, h, gn, l):
    m, k = xn.shape
    tn = TN_UP
    nj = FF_PAD // tn
    col = lambda r: pl.BlockSpec((r, tn), lambda j: (0, j))
    wcol = pl.BlockSpec((k, tn), lambda j: (0, j))
    fcol = pl.BlockSpec((None, HIST_F, tn), lambda j: (l, 0, j))
    prev_row = lambda r, half: pl.BlockSpec((None, None, None, DEC_BATCH, tn), lambda j: (l, r, half, 0, j))
    whole = lambda r, c: pl.BlockSpec((r, c), lambda j: (0, 0))
    nt = (FFN_CONV_WIDTH - 1) * DEC_BATCH
    return pl.pallas_call(
        _ffn_sample_kernel,
        grid=(nj,),
        in_specs=[whole(m, k), wcol, wcol, fcol, fcol, prev_row(0, 0), prev_row(1, 0), prev_row(0, 1), prev_row(1, 1),
                  pl.BlockSpec((tn, D_MODEL), lambda j: (j, 0)), whole(m, D_MODEL), _layer_param(gn, l)],
        out_specs=[whole(m, D_MODEL), col(nt), col(nt)],
        out_shape=[jax.ShapeDtypeStruct((m, D_MODEL), F32),
                   jax.ShapeDtypeStruct((nt, FF_PAD), F32),
                   jax.ShapeDtypeStruct((nt, FF_PAD), F32)],
        compiler_params=_params("arbitrary"),
        name="ffn_sample",
    )(xn, wg, wu, fwg, fwu, prev, prev, prev, prev, wd, h, gn)


def kernel(x_prompt, x_sample, cache_mem_k, cache_mem_v, state_conv_b, state_pool, state_sconv, state_ffn_conv, mem_prompt, g_mix_pre, g_mix_post, g_mem, g_x_pre, g_x_post, g_ffn_pre, g_ffn_post, w_in, w_out, a_norm_g, a_ws, a_bs, b_conv_w, b_conv_b, b_gn_g, b_gn_b, c_lin, c_scale, d_conv_w, w_xq, w_xk, w_xv, w_xo, w_up, f_conv_w, w_down):
    nb, ns = DEC_BATCH, DEC_SEQ
    w_sources = (w_in, w_out, w_xq, w_xo, w_up, w_up, w_down)
    w_kv_b = cast_weight_pair(w_xk, w_xv)
    layer_w = cast_layer(w_sources, 0)

    hp = x_prompt.reshape(BATCH * SEQ, D_MODEL)
    hs = jnp.transpose(x_sample, (1, 0, 2)).reshape(ns * nb, D_MODEL)
    mem = mem_prompt.reshape(BATCH * N_MEM, D_MODEL)
    st_b = state_conv_b.reshape(DEPTH, nb, -1)
    st_p = state_pool.reshape(DEPTH, nb, -1)
    st_s = state_sconv.reshape(DEPTH, nb, -1)

    row = lambda a: a.reshape(DEPTH, 1, -1)
    ag, bb, gng, gnb, cs = (row(a) for a in (a_norm_g, b_conv_b, b_gn_g, b_gn_b, c_scale))
    gpre, gmp, gxp, gxo, gfp, gfo = (row(a) for a in (g_mix_pre, g_mix_post, g_x_pre, g_x_post, g_ffn_pre,
                                                      g_ffn_post))
    abt = jnp.transpose(a_bs, (0, 2, 1))
    pad_rows = lambda a, rows: jnp.pad(a, ((0, 0), (0, rows - a.shape[1]), (0, 0)))
    bw = pad_rows(b_conv_w, HIST_B)
    dw = pad_rows(d_conv_w, HIST_D)
    fw = jnp.pad(f_conv_w.reshape(DEPTH, FFN_CONV_WIDTH, 2, D_FF),
                 ((0, 0), (0, HIST_F - FFN_CONV_WIDTH), (0, 0), (0, FF_PAD - D_FF)))
    fwg, fwu = fw[:, :, 0], fw[:, :, 1]
    aw4 = jnp.repeat(jnp.transpose(a_ws[:, :, :ns, :ns], (0, 2, 3, 1)).reshape(DEPTH, ns * ns, N_SUB), D_SUB, axis=2)
    ab4 = jnp.repeat(jnp.transpose(a_bs[:, :, :ns], (0, 2, 1)), D_SUB, axis=2)
    keep = FFN_CONV_WIDTH - 1
    prev_f = jnp.transpose(state_ffn_conv.reshape(DEPTH, nb, keep, 2, D_FF), (0, 2, 3, 1, 4))

    mk, mv = memory_kv(mem, row(g_mem), w_kv_b, TM)

    s_stacks = tuple(jnp.zeros((DEPTH, nb, c), F32)
                     for c in (st_b.shape[2], st_p.shape[2], st_s.shape[2], ns * D_GROUP))
    p_states, s_states = [], []
    for l in range(DEPTH):
        w_in_b, w_out_b, w_xq_b, w_xo_b, w_g_b, w_u_b, w_down_b = layer_w
        y, nbp, npp, nsp = inmix_prompt(hp.reshape(BATCH, SEQ, D_MODEL), gpre, w_in_b, ag, a_ws, abt, bw, bb,
                                        gng, gnb, c_lin, cs, dw, l)
        h2, xn = mid_prompt(y.reshape(BATCH * SEQ, D_MODEL), hp, w_out_b, gmp, gxp, w_xq_b, mk, mv, w_xo_b, gxo,
                            gfp, l)
        gate, tg, tu, layer_w = up_prompt(xn, w_g_b, w_u_b, fwg, fwu, l, w_sources if l + 1 < DEPTH else None)
        hp = down(gate, w_down_b, h2, gfo, l)
        p_states.append((nbp, npp, nsp, tg, tu))

        zs = norm_matmul(hs, gpre, w_in_b, l, TM, TN_IN)
        ys, s_stacks = mixer_sample(zs.reshape(ns, nb, D_IN), st_b, st_p, st_s, ag, aw4, ab4, bw, bb,
                                    gng, gnb, c_lin, cs, dw, l, s_stacks)
        h1s, qs = mid_a(ys.reshape(ns * nb, D_MODEL), hs, w_out_b, gmp, gxp, w_xq_b, l)
        q8 = jnp.pad(jnp.transpose(qs.reshape(ns, nb, D_X), (1, 0, 2)), ((0, 0), (0, V7X_SUBLANES - ns), (0, 0)))
        o8 = attn_sample(q8, cache_mem_k, cache_mem_v, l)
        os_ = jnp.transpose(o8[:, :ns], (1, 0, 2)).reshape(ns * nb, D_X)
        h2s, xns = mid_c(os_, h1s, w_xo_b, gxo, gfp, l)
        hs, tgs, tus = ffn_sample(xns, w_g_b, w_u_b, fwg, fwu, prev_f, w_down_b, h2s, gfo, l)
        s_states.append((tgs, tus))

    nbp, npp, nsp, tg, tu = (jnp.stack(a) for a in zip(*p_states))
    tgs, tus = (jnp.stack(a) for a in zip(*s_states))
    nbs, nps, nss, vs = s_stacks
    last = lambda a: a.reshape(DEPTH, BATCH, -1, HIST_F, FF_PAD)[:, :, -1, HIST_F - keep:, :D_FF]
    bmajor = lambda a: jnp.transpose(a.reshape(DEPTH, keep, nb, FF_PAD)[..., :D_FF], (0, 2, 1, 3))
    return (hp.reshape(BATCH, SEQ, D_MODEL),
            jnp.transpose(hs.reshape(ns, nb, D_MODEL), (1, 0, 2)),
            mk.reshape(DEPTH, BATCH, N_MEM, D_X),
            mv.reshape(DEPTH, BATCH, N_MEM, D_X),
            nbp[:, :, HIST_B - (CONV_B_WIDTH - 1):],
            npp[:, :, HIST_C - POOL_PREV:],
            nsp[:, :, HIST_D - (SCONV_WIDTH - 1):],
            jnp.concatenate([last(tg), last(tu)], axis=-1),
            nbs.reshape(DEPTH, nb, CONV_B_WIDTH - 1, D_GROUP),
            nps.reshape(DEPTH, nb, POOL_PREV, D_GROUP),
            nss.reshape(DEPTH, nb, SCONV_WIDTH - 1, D_GROUP),
            jnp.concatenate([bmajor(tgs), bmajor(tus)], axis=-1),
            vs.reshape(DEPTH, nb, ns, D_GROUP))
```

```python
import functools

import jax
import jax.numpy as jnp
from jax import lax
from jax.experimental import pallas as pl
from jax.experimental.pallas import tpu as pltpu

F32 = jnp.float32
BF16 = jnp.bfloat16

D_MODEL = 2048
BATCH = 4
SEQ = 2048
DEPTH = 4
DEC_BATCH = 128
DEC_SEQ = 4
PAST_LEN = 16384
D_GROUP = 512
N_SUB = 4
D_SUB = 128
D_IN = 8 * D_GROUP
CHUNK = 128
CONV_B_WIDTH = 31
POOL_WINDOWS = (2, 4, 8, 16)
POOL_PREV = 15
SCONV_WIDTH = 3
FFN_CONV_WIDTH = 3
D_FF = 5504
N_MEM = 256
N_XHEADS = 4
D_XHEAD = 128
D_X = 512
EPS = 1e-6

V7X_SUBLANES = 8
V7X_LANES = 128
V7X_VMEM_LIMIT_BYTES = 56 * 1024 * 1024

HIST_B = 32
HIST_C = 16
HIST_D = 8
HIST_F = 8

TM = 512
TM_UP = 2048
SUBS_UP = (512, 512, 512, 512)
TM_IN = 1024
SUB_MID = 256
TT = 512
ROWS = 32
TN_IN = 1024
TN_UP = 512
FF_PAD = -(-D_FF // TN_UP) * TN_UP
TN_DOWN = 512
CAST_CHUNKS = 32
BB_MIX = 32
BB_ATT = 16


def _layer_param(a, l):
    tail = a.shape[1:]
    return pl.BlockSpec((None,) + tail, lambda *_: (l,) + (0,) * len(tail))


def _params(*sem):
    return pltpu.CompilerParams(dimension_semantics=sem, vmem_limit_bytes=V7X_VMEM_LIMIT_BYTES)


def _rms(x, g):
    return x * lax.rsqrt(jnp.mean(x * x, axis=-1, keepdims=True) + EPS) * g


def _sigmoid(x):
    return 1.0 / (1.0 + jnp.exp(-x))


def _dot(a, b):
    return jnp.dot(a, b, preferred_element_type=F32)


N_CAST = 7


def _cast_specs(layer, chunk):
    def pair(rows_total, cols, col_block=0, out_rows=None, out_cols=None):
        rows = (out_rows or rows_total) // CAST_CHUNKS
        src = pl.BlockSpec((None, rows, cols), lambda *g: (layer, chunk(*g), col_block))
        dst = pl.BlockSpec((rows, out_cols or cols), lambda *g: (chunk(*g), 0))
        return src, dst, jax.ShapeDtypeStruct((out_rows or rows_total, out_cols or cols), BF16)
    table = [pair(D_MODEL, D_IN), pair(D_MODEL, D_MODEL), pair(D_MODEL, D_X), pair(D_X, D_MODEL),
             pair(D_MODEL, D_FF, 0, out_cols=FF_PAD), pair(D_MODEL, D_FF, 1, out_cols=FF_PAD),
             pair(D_FF, D_MODEL, out_rows=FF_PAD)]
    return tuple(list(t) for t in zip(*table))


def _cast_chunk(c, src, dst):
    for s, d in zip(src[:4], dst[:4]):
        d[...] = s[...].astype(BF16)
    for s, d in zip(src[4:6], dst[4:6]):
        d[:, 0:D_FF] = s[...].astype(BF16)
        d[:, D_FF:] = jnp.zeros((d.shape[0], d.shape[1] - D_FF), BF16)
    rows = dst[6].shape[0]
    row = c * rows + lax.broadcasted_iota(jnp.int32, dst[6].shape, 0)
    dst[6][...] = jnp.where(row < D_FF, src[6][...], 0.0).astype(BF16)


def _cast_layer_kernel(*refs):
    _cast_chunk(pl.program_id(0), refs[:N_CAST], refs[N_CAST:])


def cast_layer(sources, layer):
    in_specs, out_specs, out_shape = _cast_specs(layer, lambda c: c)
    return pl.pallas_call(
        _cast_layer_kernel,
        grid=(CAST_CHUNKS,),
        in_specs=in_specs,
        out_specs=out_specs,
        out_shape=out_shape,
        compiler_params=_params("parallel"),
        name="cast_layer",
    )(*sources)


def _cast_pair_kernel(a_ref, b_ref, o_ref):
    n = a_ref.shape[-1]
    o_ref[:, 0:n] = a_ref[...].astype(BF16)
    o_ref[:, n:2 * n] = b_ref[...].astype(BF16)


def cast_weight_pair(a, b):
    depth, r, c = a.shape
    spec = pl.BlockSpec((None, r, c), lambda l: (l, 0, 0))
    return pl.pallas_call(
        _cast_pair_kernel,
        grid=(depth,),
        in_specs=[spec, spec],
        out_specs=pl.BlockSpec((None, r, 2 * c), lambda l: (l, 0, 0)),
        out_shape=jax.ShapeDtypeStruct((depth, r, 2 * c), BF16),
        compiler_params=_params("parallel"),
        name="cast_weight_pair",
    )(a, b)


def _norm_matmul_kernel(x_ref, g_ref, w_ref, o_ref, xn_ref):
    @pl.when(pl.program_id(1) == 0)
    def _():
        xn_ref[...] = _rms(x_ref[...], g_ref[...]).astype(BF16)

    o_ref[...] = _dot(xn_ref[...], w_ref[...])


def norm_matmul(x, g, w, l, tm, tn):
    m, k = x.shape
    n = w.shape[1]
    return pl.pallas_call(
        _norm_matmul_kernel,
        grid=(m // tm, n // tn),
        in_specs=[pl.BlockSpec((tm, k), lambda i, j: (i, 0)),
                  _layer_param(g, l),
                  pl.BlockSpec((k, tn), lambda i, j: (0, j))],
        out_specs=pl.BlockSpec((tm, tn), lambda i, j: (i, j)),
        out_shape=jax.ShapeDtypeStruct((m, n), F32),
        scratch_shapes=[pltpu.VMEM((tm, k), BF16)],
        compiler_params=_params("parallel", "arbitrary"),
        name="norm_matmul",
    )(x, g, w)


def _memory_kv_kernel(x_ref, g_ref, w_ref, k_ref, v_ref):
    kv = _dot(_rms(x_ref[...], g_ref[...]).astype(BF16), w_ref[...])
    n = k_ref.shape[-1]
    k_ref[...] = kv[:, 0:n]
    v_ref[...] = kv[:, n:2 * n]


def memory_kv(x, g, w, tm):
    m, k = x.shape
    depth, _, n2 = w.shape
    half = pl.BlockSpec((None, tm, n2 // 2), lambda l, i: (l, i, 0))
    shape = jax.ShapeDtypeStruct((depth, m, n2 // 2), F32)
    return pl.pallas_call(
        _memory_kv_kernel,
        grid=(depth, m // tm),
        in_specs=[pl.BlockSpec((tm, k), lambda l, i: (i, 0)),
                  pl.BlockSpec((None, 1, k), lambda l, i: (l, 0, 0)),
                  pl.BlockSpec((None, k, n2), lambda l, i: (l, 0, 0))],
        out_specs=[half, half],
        out_shape=[shape, shape],
        compiler_params=_params("parallel", "parallel"),
        name="memory_kv",
    )(x, g, w)


def _layer_norm(v, g):
    mu = jnp.mean(v, axis=-1, keepdims=True)
    d = v - mu
    var = jnp.mean(d * d, axis=-1, keepdims=True)
    return d * lax.rsqrt(var + EPS) * g


def _group_norm_silu(y, g, b):
    mu = jnp.mean(y, axis=-1, keepdims=True)
    d = y - mu
    var = jnp.mean(d * d, axis=-1, keepdims=True)
    yn = d * lax.rsqrt(var + EPS) * g + b
    return yn * _sigmoid(yn)


def _inmix_prompt_kernel(x_ref, gpre_ref, win_ref, ag_ref, aws_ref, abt_ref, bw_ref, bb_ref, gng_ref, gnb_ref,
                         clin_ref, cs_ref, dw_ref,
                         y_ref, nb_ref, np_ref, ns_ref,
                         extb, extc, extd, pooled):
    t = pl.program_id(1)
    tt = y_ref.shape[0]
    quarter = 2 * D_GROUP

    @pl.when(t == 0)
    def _():
        for s in range(V7X_SUBLANES):
            extb[s, 0:HIST_B, :] = jnp.zeros((HIST_B, D_GROUP), F32)
            extb[s, tt + HIST_B - V7X_SUBLANES:tt + HIST_B, :] = jnp.zeros((V7X_SUBLANES, D_GROUP), F32)
        extc[0:HIST_C, :] = jnp.zeros((HIST_C, D_GROUP), F32)
        extd[0:HIST_D, :] = jnp.zeros((HIST_D, D_GROUP), F32)

    xn = _rms(x_ref[...], gpre_ref[...]).astype(BF16)
    z = {}
    proj = lambda q: _dot(xn, win_ref[:, q * quarter:(q + 1) * quarter])

    def zcol(k, rs=slice(None)):
        return z[k // 2][rs, (k % 2) * D_GROUP:(k % 2 + 1) * D_GROUP]

    z[1] = proj(1)
    hb = zcol(2) * _sigmoid(zcol(3))
    for s in range(V7X_SUBLANES):
        extb[s, HIST_B - s:HIST_B - s + tt, :] = hb
    n_chunks = tt // ROWS
    later = {0: 0, n_chunks // 3: 2, 2 * n_chunks // 3: 3}
    for c in range(n_chunks):
        if c in later:
            z[later[c]] = proj(later[c])
        for g in range(N_SUB):
            ls = slice(g * D_SUB, (g + 1) * D_SUB)
            acc = jnp.zeros((ROWS, D_SUB), F32)
            for k in range(CONV_B_WIDTH):
                tiles, s = divmod(HIST_B - (CONV_B_WIDTH - 1) + k, V7X_SUBLANES)
                r0 = c * ROWS + tiles * V7X_SUBLANES
                acc = acc + bw_ref[k:k + 1, ls] * extb[s, r0:r0 + ROWS, ls]
            yb = _group_norm_silu(acc + bb_ref[:, ls], gng_ref[:, ls], gnb_ref[:, ls])
            y_ref[c * ROWS:(c + 1) * ROWS, D_GROUP + g * D_SUB:D_GROUP + (g + 1) * D_SUB] = yb.astype(BF16)

    vn = _layer_norm(zcol(1), ag_ref[...]).astype(BF16)
    row = lax.broadcasted_iota(jnp.int32, (CHUNK, CHUNK), 0)
    col = lax.broadcasted_iota(jnp.int32, (CHUNK, CHUNK), 1)
    for h in range(N_SUB):
        wm = jnp.where(row >= col, aws_ref[h], 0.0).astype(BF16)
        bias = abt_ref[:, h:h + 1]
        for c in range(tt // CHUNK):
            rs = slice(c * CHUNK, (c + 1) * CHUNK)
            ls = slice(h * D_SUB, (h + 1) * D_SUB)
            zz = _dot(wm, vn[rs, ls]) + bias
            y_ref[rs, ls] = (z[0][rs, ls] * zz).astype(BF16)

    extc[HIST_C:HIST_C + tt, :] = zcol(4)
    for c in range(tt // ROWS):
        pos = t * tt + c * ROWS + lax.broadcasted_iota(jnp.int32, (ROWS, 1), 0)
        for g, win in enumerate(POOL_WINDOWS):
            ls = slice(g * D_SUB, (g + 1) * D_SUB)
            r0 = c * ROWS + HIST_C
            x = extc[r0:r0 + ROWS, ls]
            s = x
            for i in range(1, win):
                s = s + extc[r0 - i:r0 - i + ROWS, ls]
            cnt = jnp.minimum(pos + 1, win).astype(F32)
            pooled[c * ROWS:(c + 1) * ROWS, ls] = (s / cnt - x).astype(BF16)
    for g in range(N_SUB):
        ls = slice(g * D_SUB, (g + 1) * D_SUB)
        yc = _dot(pooled[:, ls], clin_ref[g].astype(BF16)) * cs_ref[:, ls]
        y_ref[:, 2 * D_GROUP + g * D_SUB:2 * D_GROUP + (g + 1) * D_SUB] = yc.astype(BF16)

    extd[HIST_D:HIST_D + tt, :] = zcol(7) * zcol(5)
    for c in range(tt // ROWS):
        r0 = c * ROWS + HIST_D
        conv = (dw_ref[0:1, :] * extd[r0 - 2:r0 - 2 + ROWS, :]
                + dw_ref[1:2, :] * extd[r0 - 1:r0 - 1 + ROWS, :]
                + dw_ref[2:3, :] * extd[r0:r0 + ROWS, :])
        rs = slice(c * ROWS, (c + 1) * ROWS)
        y_ref[rs, 3 * D_GROUP:4 * D_GROUP] = (zcol(6, rs) * conv).astype(BF16)

    for s in range(V7X_SUBLANES):
        extb[s, 0:HIST_B, :] = extb[s, tt:tt + HIST_B, :]
    extc[0:HIST_C, :] = extc[tt:tt + HIST_C, :]
    extd[0:HIST_D, :] = extd[tt:tt + HIST_D, :]

    @pl.when(t == pl.num_programs(1) - 1)
    def _():
        nb_ref[...] = extb[0, 0:HIST_B, :]
        np_ref[...] = extc[0:HIST_C, :]
        ns_ref[...] = extd[0:HIST_D, :]


def inmix_prompt(x, gpre, win, ag, aws, abt, bw, bb, gng, gnb, clin, cs, dw, l):
    tt = TT
    full = lambda a: _layer_param(a, l)
    hist = lambda r: pl.BlockSpec((None, r, D_GROUP), lambda b, t: (b, 0, 0))
    return pl.pallas_call(
        _inmix_prompt_kernel,
        grid=(BATCH, SEQ // tt),
        in_specs=[pl.BlockSpec((None, tt, D_MODEL), lambda b, t: (b, t, 0)),
                  full(gpre),
                  pl.BlockSpec((D_MODEL, D_IN), lambda b, t: (0, 0), pipeline_mode=pl.Buffered(1)),
                  full(ag), full(aws), full(abt), full(bw), full(bb), full(gng), full(gnb),
                  full(clin), full(cs), full(dw)],
        out_specs=[pl.BlockSpec((None, tt, D_MODEL), lambda b, t: (b, t, 0)),
                   hist(HIST_B), hist(HIST_C), hist(HIST_D)],
        out_shape=[jax.ShapeDtypeStruct((BATCH, SEQ, D_MODEL), BF16),
                   jax.ShapeDtypeStruct((BATCH, HIST_B, D_GROUP), F32),
                   jax.ShapeDtypeStruct((BATCH, HIST_C, D_GROUP), F32),
                   jax.ShapeDtypeStruct((BATCH, HIST_D, D_GROUP), F32)],
        scratch_shapes=[pltpu.VMEM((V7X_SUBLANES, HIST_B + tt, D_GROUP), F32),
                        pltpu.VMEM((HIST_C + tt, D_GROUP), F32),
                        pltpu.VMEM((HIST_D + tt, D_GROUP), F32),
                        pltpu.VMEM((tt, D_GROUP), BF16)],
        compiler_params=_params("parallel", "arbitrary"),
        name="inmix_prompt",
    )(x, gpre, win, ag, aws, abt, bw, bb, gng, gnb, clin, cs, dw)


def _mixer_sample_kernel(z_ref, cb_ref, cp_ref, csc_ref, ag_ref, aw4_ref, ab4_ref, bw_ref, bb_ref,
                         gng_ref, gnb_ref, clin_ref, cs_ref, dw_ref,
                         y_ref, nb_ref, np_ref, nsc_ref, v_ref, pooled):
    bblk = z_ref.shape[1]
    blk = lambda r: slice(r * D_GROUP, (r + 1) * D_GROUP)

    vn = [_layer_norm(z_ref[t, :, D_GROUP:2 * D_GROUP], ag_ref[...]) for t in range(DEC_SEQ)]
    for i in range(DEC_SEQ):
        v_ref[:, blk(i)] = vn[i]
        zz = ab4_ref[i:i + 1, :]
        for j in range(i + 1):
            zz = zz + aw4_ref[i * DEC_SEQ + j:i * DEC_SEQ + j + 1, :] * vn[j]
        y_ref[i, :, 0:D_GROUP] = (z_ref[i, :, 0:D_GROUP] * zz).astype(BF16)

    nprev = CONV_B_WIDTH - 1
    hb = [z_ref[t, :, 2 * D_GROUP:3 * D_GROUP] * _sigmoid(z_ref[t, :, 3 * D_GROUP:4 * D_GROUP])
          for t in range(DEC_SEQ)]
    ext_b = lambda r: cb_ref[:, blk(r)] if r < nprev else hb[r - nprev]
    for t in range(DEC_SEQ):
        acc = jnp.zeros((bblk, D_GROUP), F32)
        for k in range(CONV_B_WIDTH):
            acc = acc + bw_ref[k:k + 1, :] * ext_b(t + k)
        acc = acc + bb_ref[...]
        for g in range(N_SUB):
            ls = slice(g * D_SUB, (g + 1) * D_SUB)
            yb = _group_norm_silu(acc[:, ls], gng_ref[:, ls], gnb_ref[:, ls])
            y_ref[t, :, D_GROUP + g * D_SUB:D_GROUP + (g + 1) * D_SUB] = yb.astype(BF16)
    for r in range(nprev):
        nb_ref[:, blk(r)] = ext_b(r + DEC_SEQ)

    cx = [z_ref[t, :, 4 * D_GROUP:5 * D_GROUP] for t in range(DEC_SEQ)]
    ext_c = lambda r: cp_ref[:, blk(r)] if r < POOL_PREV else cx[r - POOL_PREV]
    for t in range(DEC_SEQ):
        for g, win in enumerate(POOL_WINDOWS):
            ls = slice(g * D_SUB, (g + 1) * D_SUB)
            s = cx[t][:, ls]
            for i in range(1, win):
                s = s + ext_c(POOL_PREV + t - i)[:, ls]
            cnt = float(min(PAST_LEN + t + 1, win))
            pooled[t * bblk:(t + 1) * bblk, ls] = (s / cnt - cx[t][:, ls]).astype(BF16)
    for g in range(N_SUB):
        ls = slice(g * D_SUB, (g + 1) * D_SUB)
        yc = _dot(pooled[:, ls], clin_ref[g].astype(BF16)) * cs_ref[:, ls]
        for t in range(DEC_SEQ):
            y_ref[t, :, 2 * D_GROUP + g * D_SUB:2 * D_GROUP + (g + 1) * D_SUB] = (
                yc[t * bblk:(t + 1) * bblk].astype(BF16))
    for r in range(POOL_PREV):
        np_ref[:, blk(r)] = ext_c(r + DEC_SEQ)

    nsp = SCONV_WIDTH - 1
    hd = [z_ref[t, :, 7 * D_GROUP:8 * D_GROUP] * z_ref[t, :, 5 * D_GROUP:6 * D_GROUP] for t in range(DEC_SEQ)]
    ext_d = lambda r: csc_ref[:, blk(r)] if r < nsp else hd[r - nsp]
    for t in range(DEC_SEQ):
        conv = dw_ref[0:1, :] * ext_d(t) + dw_ref[1:2, :] * ext_d(t + 1) + dw_ref[2:3, :] * ext_d(t + 2)
        y_ref[t, :, 3 * D_GROUP:4 * D_GROUP] = (z_ref[t, :, 6 * D_GROUP:7 * D_GROUP] * conv).astype(BF16)
    for r in range(nsp):
        nsc_ref[:, blk(r)] = ext_d(r + DEC_SEQ)


N_MIXER_SAMPLE_IN = 14


def _mixer_sample_stacked_kernel(*refs):
    n = N_MIXER_SAMPLE_IN
    _mixer_sample_kernel(*refs[:n], *refs[n + 4:])


def mixer_sample(z, cb, cp, csc, ag, aw4, ab4, bw, bb, gng, gnb, clin, cs, dw, l, stacks):
    bblk = BB_MIX
    full = lambda a: _layer_param(a, l)
    st = lambda a: pl.BlockSpec((None, bblk, a.shape[2]), lambda b: (l, b, 0))
    untouched = pl.BlockSpec(memory_space=pl.ANY)
    res = pl.pallas_call(
        _mixer_sample_stacked_kernel,
        grid=(DEC_BATCH // bblk,),
        in_specs=[pl.BlockSpec((DEC_SEQ, bblk, D_IN), lambda b: (0, b, 0)),
                  st(cb), st(cp), st(csc),
                  full(ag), full(aw4), full(ab4), full(bw), full(bb), full(gng), full(gnb),
                  full(clin), full(cs), full(dw)] + [untouched] * len(stacks),
        out_specs=[pl.BlockSpec((DEC_SEQ, bblk, D_MODEL), lambda b: (0, b, 0))] + [st(a) for a in stacks],
        out_shape=[jax.ShapeDtypeStruct((DEC_SEQ, DEC_BATCH, D_MODEL), BF16)]
                  + [jax.ShapeDtypeStruct(a.shape, F32) for a in stacks],
        input_output_aliases={N_MIXER_SAMPLE_IN + k: 1 + k for k in range(len(stacks))},
        scratch_shapes=[pltpu.VMEM((DEC_SEQ * bblk, D_GROUP), BF16)],
        compiler_params=_params("parallel"),
        name="mixer_sample",
    )(z, cb, cp, csc, ag, aw4, ab4, bw, bb, gng, gnb, clin, cs, dw, *stacks)
    return res[0], tuple(res[1:])


def _stage_a(y, h, wout_ref, gpost_ref, gpre_ref, wxq_ref):
    h1 = h + _rms(_dot(y, wout_ref[...]), gpost_ref[...])
    q = _dot(_rms(h1, gpre_ref[...]).astype(BF16), wxq_ref[...])
    return h1, q


def _stage_c(o, h1, wxo_ref, gpost_ref, gffn_ref):
    h2 = h1 + _rms(_dot(o, wxo_ref[...]), gpost_ref[...])
    return h2, _rms(h2, gffn_ref[...]).astype(BF16)


def _softmax_rows(s):
    e = jnp.exp(s - jnp.max(s, axis=-1, keepdims=True))
    return e / jnp.sum(e, axis=-1, keepdims=True)


_NT = (((1,), (1,)), ((), ()))


def _mid_prompt_kernel(y_ref, h_ref, wout_ref, gmp_ref, gxp_ref, wxq_ref, mk_ref, mv_ref, wxo_ref,
                       gxo_ref, gffn_ref, h2_ref, xn_ref):
    k = mk_ref[...].astype(BF16)
    v = mv_ref[...].astype(BF16)
    halves = [slice(r * SUB_MID, (r + 1) * SUB_MID) for r in range(y_ref.shape[0] // SUB_MID)]

    def out_proj(rs):
        return _dot(y_ref[rs, :], wout_ref[...])

    def residual_and_query_in(rs, mix):
        h1 = h_ref[rs, :] + _rms(mix, gmp_ref[...])
        return h1, _rms(h1, gxp_ref[...]).astype(BF16)

    def attention(xq):
        q = _dot(xq, wxq_ref[...]).astype(BF16)
        heads = []
        for hd in range(N_XHEADS):
            ls = slice(hd * D_XHEAD, (hd + 1) * D_XHEAD)
            s = lax.dot_general(q[:, ls], k[:, ls], _NT, preferred_element_type=F32) * (D_XHEAD ** -0.5)
            heads.append(_dot(_softmax_rows(s).astype(BF16), v[:, ls]))
        return _dot(jnp.concatenate(heads, axis=-1).astype(BF16), wxo_ref[...])

    def finish(rs, h1, xa):
        h2 = h1 + _rms(xa, gxo_ref[...])
        h2_ref[rs, :] = h2
        xn_ref[rs, :] = _rms(h2, gffn_ref[...]).astype(BF16)

    mix = [out_proj(rs) for rs in halves]
    state = [residual_and_query_in(halves[0], mix[0])]
    xa = []
    for r, rs in enumerate(halves):
        xa.append(attention(state[r][1]))
        if r + 1 < len(halves):
            state.append(residual_and_query_in(halves[r + 1], mix[r + 1]))
        if r > 0:
            finish(halves[r - 1], state[r - 1][0], xa[r - 1])
    finish(halves[-1], state[-1][0], xa[-1])


def _resident(w):
    return pl.BlockSpec(w.shape, lambda i: (0, 0), pipeline_mode=pl.Buffered(1))


def mid_prompt(y, h, wout, gmp, gxp, wxq, mk, mv, wxo, gxo, gffn, l):
    m = y.shape[0]
    tm = TM
    per_seq = SEQ // tm
    rows = lambda c: pl.BlockSpec((tm, c), lambda i: (i, 0))
    full = lambda a: _layer_param(a, l)
    mem = pl.BlockSpec((None, N_MEM, D_X), lambda i: (l, i // per_seq, 0))
    return pl.pallas_call(
        _mid_prompt_kernel,
        grid=(m // tm,),
        in_specs=[rows(D_MODEL), rows(D_MODEL), _resident(wout), full(gmp), full(gxp),
                  _resident(wxq), mem, mem, _resident(wxo), full(gxo), full(gffn)],
        out_specs=[rows(D_MODEL), rows(D_MODEL)],
        out_shape=[jax.ShapeDtypeStruct((m, D_MODEL), F32), jax.ShapeDtypeStruct((m, D_MODEL), BF16)],
        compiler_params=_params("parallel"),
        name="mid_prompt",
    )(y, h, wout, gmp, gxp, wxq, mk, mv, wxo, gxo, gffn)


def _mid_a_kernel(y_ref, h_ref, wout_ref, gmp_ref, gxp_ref, wxq_ref, h1_ref, q_ref):
    h1, q = _stage_a(y_ref[...], h_ref[...], wout_ref, gmp_ref, gxp_ref, wxq_ref)
    h1_ref[...] = h1
    q_ref[...] = q.astype(BF16)


def mid_a(y, h, wout, gmp, gxp, wxq, l):
    m = y.shape[0]
    tm = TM
    rows = lambda c: pl.BlockSpec((tm, c), lambda i: (i, 0))
    full = lambda a: _layer_param(a, l)
    return pl.pallas_call(
        _mid_a_kernel,
        grid=(m // tm,),
        in_specs=[rows(D_MODEL), rows(D_MODEL), _resident(wout), full(gmp), full(gxp),
                  _resident(wxq)],
        out_specs=[rows(D_MODEL), rows(D_X)],
        out_shape=[jax.ShapeDtypeStruct((m, D_MODEL), F32), jax.ShapeDtypeStruct((m, D_X), BF16)],
        compiler_params=_params("parallel"),
        name="mid_a",
    )(y, h, wout, gmp, gxp, wxq)


def _mid_c_kernel(o_ref, h1_ref, wxo_ref, gxo_ref, gffn_ref, h2_ref, xn_ref):
    h2, xn = _stage_c(o_ref[...], h1_ref[...], wxo_ref, gxo_ref, gffn_ref)
    h2_ref[...] = h2
    xn_ref[...] = xn


def mid_c(o, h1, wxo, gxo, gffn, l):
    m = o.shape[0]
    tm = TM
    rows = lambda c: pl.BlockSpec((tm, c), lambda i: (i, 0))
    full = lambda a: _layer_param(a, l)
    return pl.pallas_call(
        _mid_c_kernel,
        grid=(m // tm,),
        in_specs=[rows(D_X), rows(D_MODEL), _resident(wxo), full(gxo), full(gffn)],
        out_specs=[rows(D_MODEL), rows(D_MODEL)],
        out_shape=[jax.ShapeDtypeStruct((m, D_MODEL), F32), jax.ShapeDtypeStruct((m, D_MODEL), BF16)],
        compiler_params=_params("parallel"),
        name="mid_c",
    )(o, h1, wxo, gxo, gffn)


def _attn_sample_kernel(q_ref, k_ref, v_ref, o_ref):
    bblk = q_ref.shape[0]
    lane_head = lax.broadcasted_iota(jnp.int32, (V7X_SUBLANES, D_X), 1) // D_XHEAD

    def scores(b):
        q8 = q_ref[b]
        qbd = jnp.concatenate([jnp.where(lane_head == hd, q8, jnp.zeros_like(q8)) for hd in range(N_XHEADS)], axis=0)
        return lax.dot_general(qbd, k_ref[b].astype(BF16), _NT, preferred_element_type=F32) * (D_XHEAD ** -0.5)

    def attend(b, s):
        of = _dot(_softmax_rows(s).astype(BF16), v_ref[b].astype(BF16))
        o8 = jnp.zeros((V7X_SUBLANES, D_X), F32)
        for hd in range(N_XHEADS):
            o8 = o8 + jnp.where(lane_head == hd, of[hd * V7X_SUBLANES:(hd + 1) * V7X_SUBLANES], 0.0)
        o_ref[b] = o8.astype(BF16)

    s = scores(0)
    for b in range(1, bblk):
        s_next = scores(b)
        attend(b - 1, s)
        s = s_next
    attend(bblk - 1, s)


def attn_sample(q, k, v, l):
    bblk = BB_ATT
    qs = pl.BlockSpec((bblk, V7X_SUBLANES, D_X), lambda b: (b, 0, 0))
    ms = pl.BlockSpec((None, bblk, N_MEM, D_X), lambda b: (l, b, 0, 0))
    return pl.pallas_call(
        _attn_sample_kernel,
        grid=(DEC_BATCH // bblk,),
        in_specs=[qs, ms, ms],
        out_specs=qs,
        out_shape=jax.ShapeDtypeStruct(q.shape, BF16),
        compiler_params=_params("parallel"),
        name="attn_sample",
    )(q, k, v)


def _gate(cg, cu):
    return (cg * _sigmoid(cg) * cu).astype(BF16)


def _shift_conv3(prev, h, fw_ref):
    rows = h.shape[0]
    ext = jnp.concatenate([prev, h], axis=0)
    return (fw_ref[0:1, :] * ext[HIST_F - 2:HIST_F - 2 + rows]
            + fw_ref[1:2, :] * ext[HIST_F - 1:HIST_F - 1 + rows]
            + fw_ref[2:3, :] * h)


def _up_prompt_kernel(*refs, per_seq, subs, cast_next):
    n_cast = N_CAST if cast_next else 0
    xn_ref, wg_ref, wu_ref, fwg_ref, fwu_ref = refs[:5]
    src = refs[5:5 + n_cast]
    o_ref, tg_ref, tu_ref = refs[5 + n_cast:8 + n_cast]
    dst = refs[8 + n_cast:8 + 2 * n_cast]
    carg, caru = refs[8 + 2 * n_cast:]
    i = pl.program_id(0)
    j = pl.program_id(1)

    @pl.when(i % per_seq == 0)
    def _():
        carg[j] = jnp.zeros(carg.shape[1:], F32)
        caru[j] = jnp.zeros(caru.shape[1:], F32)

    if cast_next:
        _cast_chunk(jnp.minimum(i * pl.num_programs(1) + j, CAST_CHUNKS - 1), src, dst)

    prev_g = carg[j]
    prev_u = caru[j]
    r0 = 0
    for sub in subs:
        rs = slice(r0, r0 + sub)
        r0 += sub
        hg = _dot(xn_ref[rs, :], wg_ref[...])
        hu = _dot(xn_ref[rs, :], wu_ref[...])
        o_ref[rs, :] = _gate(_shift_conv3(prev_g, hg, fwg_ref), _shift_conv3(prev_u, hu, fwu_ref))
        prev_g = hg[sub - HIST_F:]
        prev_u = hu[sub - HIST_F:]
    carg[j] = prev_g
    caru[j] = prev_u
    tg_ref[...] = prev_g
    tu_ref[...] = prev_u


def up_prompt(xn, wg, wu, fwg, fwu, l, next_sources=None):
    m, k = xn.shape
    tm, tn = TM_UP, TN_UP
    assert sum(SUBS_UP) == tm
    nj = FF_PAD // tn
    assert (m // tm) * nj >= CAST_CHUNKS
    wcol = pl.BlockSpec((k, tn), lambda i, j: (0, j))
    fcol = pl.BlockSpec((None, HIST_F, tn), lambda i, j: (l, 0, j))
    tail = pl.BlockSpec((HIST_F, tn), lambda i, j: (i, j))
    cast_in, cast_out, cast_shape = ([], [], [])
    if next_sources is not None:
        cast_in, cast_out, cast_shape = _cast_specs(l + 1, lambda i, j: jnp.minimum(i * nj + j, CAST_CHUNKS - 1))
    res = pl.pallas_call(
        functools.partial(_up_prompt_kernel, per_seq=SEQ // tm, subs=SUBS_UP, cast_next=next_sources is not None),
        grid=(m // tm, nj),
        in_specs=[pl.BlockSpec((tm, k), lambda i, j: (i, 0)), wcol, wcol, fcol, fcol] + cast_in,
        out_specs=[pl.BlockSpec((tm, tn), lambda i, j: (i, j)), tail, tail] + cast_out,
        out_shape=[jax.ShapeDtypeStruct((m, FF_PAD), BF16),
                   jax.ShapeDtypeStruct((m // tm * HIST_F, FF_PAD), F32),
                   jax.ShapeDtypeStruct((m // tm * HIST_F, FF_PAD), F32)] + cast_shape,
        scratch_shapes=[pltpu.VMEM((nj, HIST_F, tn), F32), pltpu.VMEM((nj, HIST_F, tn), F32)],
        compiler_params=_params("arbitrary", "arbitrary"),
        name="up_prompt",
    )(xn, wg, wu, fwg, fwu, *(next_sources or ()))
    return res[0], res[1], res[2], tuple(res[3:])


def _down_kernel(g_ref, w_ref, h_ref, gn_ref, o_ref, acc):
    j = pl.program_id(1)
    tn = w_ref.shape[1]
    acc[:, pl.ds(pl.multiple_of(j * tn, tn), tn)] = _dot(g_ref[...], w_ref[...])

    @pl.when(j == pl.num_programs(1) - 1)
    def _():
        o_ref[...] = h_ref[...] + _rms(acc[...], gn_ref[...])


def down(g, w, h, gn, l):
    m, k = g.shape
    tm, tn = TM, TN_DOWN
    return pl.pallas_call(
        _down_kernel,
        grid=(m // tm, D_MODEL // tn),
        in_specs=[pl.BlockSpec((tm, k), lambda i, j: (i, 0)),
                  pl.BlockSpec((k, tn), lambda i, j: (0, j)),
                  pl.BlockSpec((tm, D_MODEL), lambda i, j: (i, 0)),
                  _layer_param(gn, l)],
        out_specs=pl.BlockSpec((tm, D_MODEL), lambda i, j: (i, 0)),
        out_shape=jax.ShapeDtypeStruct((m, D_MODEL), F32),
        scratch_shapes=[pltpu.VMEM((tm, D_MODEL), F32)],
        compiler_params=_params("parallel", "arbitrary"),
        name="down",
    )(g, w, h, gn)


def _ffn_sample_kernel(xn_ref, wg_ref, wu_ref, fwg_ref, fwu_ref, p0g_ref, p1g_ref, p0u_ref, p1u_ref,
                       wd_ref, h_ref, gn_ref, o_ref, tg_ref, tu_ref):
    nb = DEC_BATCH
    j = pl.program_id(0)

    @pl.when(j == 0)
    def _():
        o_ref[...] = jnp.zeros(o_ref.shape, F32)

    xn = xn_ref[...]
    hg = _dot(xn, wg_ref[...])
    hu = _dot(xn, wu_ref[...])
    tg_ref[...] = hg[(DEC_SEQ - 2) * nb:, :]
    tu_ref[...] = hu[(DEC_SEQ - 2) * nb:, :]
    tn = wg_ref.shape[1]
    valid = j * tn + lax.broadcasted_iota(jnp.int32, (nb, tn), 1) < D_FF
    plane = lambda ref: jnp.where(valid, ref[...], 0.0)
    ext_g = [plane(p0g_ref), plane(p1g_ref)] + [hg[t * nb:(t + 1) * nb] for t in range(DEC_SEQ)]
    ext_u = [plane(p0u_ref), plane(p1u_ref)] + [hu[t * nb:(t + 1) * nb] for t in range(DEC_SEQ)]
    for t in range(DEC_SEQ):
        cg = fwg_ref[0:1, :] * ext_g[t] + fwg_ref[1:2, :] * ext_g[t + 1] + fwg_ref[2:3, :] * ext_g[t + 2]
        cu = fwu_ref[0:1, :] * ext_u[t] + fwu_ref[1:2, :] * ext_u[t + 1] + fwu_ref[2:3, :] * ext_u[t + 2]
        ts = slice(t * nb, (t + 1) * nb)
        o_ref[ts, :] += _dot(_gate(cg, cu), wd_ref[...])

    @pl.when(j == pl.num_programs(0) - 1)
    def _():
        o_ref[...] = h_ref[...] + _rms(o_ref[...], gn_ref[...])


N_FFN_SAMPLE_IN = 12


def _ffn_sample_stacked_kernel(*refs):
    n = N_FFN_SAMPLE_IN
    _ffn_sample_kernel(*refs[:n], *refs[n + 2:])


def ffn_sample(xn, wg, wu, fwg, fwu, prev, wd, h, gn, l, tails):
    m, k = xn.shape
    tn = TN_UP
    nj = FF_PAD // tn
    wcol = pl.BlockSpec((k, tn), lambda j: (0, j))
    fcol = pl.BlockSpec((None, HIST_F, tn), lambda j: (l, 0, j))
    prev_row = lambda r, half: pl.BlockSpec((None, None, None, DEC_BATCH, tn), lambda j: (l, r, half, 0, j))
    whole = lambda r, c: pl.BlockSpec((r, c), lambda j: (0, 0))
    nt = (FFN_CONV_WIDTH - 1) * DEC_BATCH
    tail = pl.BlockSpec((None, nt, tn), lambda j: (l, 0, j))
    untouched = pl.BlockSpec(memory_space=pl.ANY)
    res = pl.pallas_call(
        _ffn_sample_stacked_kernel,
        grid=(nj,),
        in_specs=[whole(m, k), wcol, wcol, fcol, fcol, prev_row(0, 0), prev_row(1, 0), prev_row(0, 1), prev_row(1, 1),
                  pl.BlockSpec((tn, D_MODEL), lambda j: (j, 0)), whole(m, D_MODEL), _layer_param(gn, l),
                  untouched, untouched],
        out_specs=[whole(m, D_MODEL), tail, tail],
        out_shape=[jax.ShapeDtypeStruct((m, D_MODEL), F32)] + [jax.ShapeDtypeStruct(a.shape, F32) for a in tails],
        input_output_aliases={N_FFN_SAMPLE_IN: 1, N_FFN_SAMPLE_IN + 1: 2},
        compiler_params=_params("arbitrary"),
        name="ffn_sample",
    )(xn, wg, wu, fwg, fwu, prev, prev, prev, prev, wd, h, gn, *tails)
    return res[0], tuple(res[1:])


def kernel(x_prompt, x_sample, cache_mem_k, cache_mem_v, state_conv_b, state_pool, state_sconv, state_ffn_conv, mem_prompt, g_mix_pre, g_mix_post, g_mem, g_x_pre, g_x_post, g_ffn_pre, g_ffn_post, w_in, w_out, a_norm_g, a_ws, a_bs, b_conv_w, b_conv_b, b_gn_g, b_gn_b, c_lin, c_scale, d_conv_w, w_xq, w_xk, w_xv, w_xo, w_up, f_conv_w, w_down):
    nb, ns = DEC_BATCH, DEC_SEQ
    w_sources = (w_in, w_out, w_xq, w_xo, w_up, w_up, w_down)
    w_kv_b = cast_weight_pair(w_xk, w_xv)
    layer_w = cast_layer(w_sources, 0)

    hp = x_prompt.reshape(BATCH * SEQ, D_MODEL)
    hs = jnp.transpose(x_sample, (1, 0, 2)).reshape(ns * nb, D_MODEL)
    mem = mem_prompt.reshape(BATCH * N_MEM, D_MODEL)
    st_b = state_conv_b.reshape(DEPTH, nb, -1)
    st_p = state_pool.reshape(DEPTH, nb, -1)
    st_s = state_sconv.reshape(DEPTH, nb, -1)

    row = lambda a: a.reshape(DEPTH, 1, -1)
    ag, bb, gng, gnb, cs = (row(a) for a in (a_norm_g, b_conv_b, b_gn_g, b_gn_b, c_scale))
    gpre, gmp, gxp, gxo, gfp, gfo = (row(a) for a in (g_mix_pre, g_mix_post, g_x_pre, g_x_post, g_ffn_pre,
                                                      g_ffn_post))
    abt = jnp.transpose(a_bs, (0, 2, 1))
    pad_rows = lambda a, rows: jnp.pad(a, ((0, 0), (0, rows - a.shape[1]), (0, 0)))
    bw = pad_rows(b_conv_w, HIST_B)
    dw = pad_rows(d_conv_w, HIST_D)
    fw = jnp.pad(f_conv_w.reshape(DEPTH, FFN_CONV_WIDTH, 2, D_FF),
                 ((0, 0), (0, HIST_F - FFN_CONV_WIDTH), (0, 0), (0, FF_PAD - D_FF)))
    fwg, fwu = fw[:, :, 0], fw[:, :, 1]
    aw4 = jnp.repeat(jnp.transpose(a_ws[:, :, :ns, :ns], (0, 2, 3, 1)).reshape(DEPTH, ns * ns, N_SUB), D_SUB, axis=2)
    ab4 = jnp.repeat(jnp.transpose(a_bs[:, :, :ns], (0, 2, 1)), D_SUB, axis=2)
    keep = FFN_CONV_WIDTH - 1
    prev_f = jnp.transpose(state_ffn_conv.reshape(DEPTH, nb, keep, 2, D_FF), (0, 2, 3, 1, 4))

    mk, mv = memory_kv(mem, row(g_mem), w_kv_b, TM)

    s_stacks = tuple(jnp.zeros((DEPTH, nb, c), F32)
                     for c in (st_b.shape[2], st_p.shape[2], st_s.shape[2], ns * D_GROUP))
    s_tails = tuple(jnp.zeros((DEPTH, keep * nb, FF_PAD), F32) for _ in range(2))
    p_states = []
    for l in range(DEPTH):
        w_in_b, w_out_b, w_xq_b, w_xo_b, w_g_b, w_u_b, w_down_b = layer_w
        y, nbp, npp, nsp = inmix_prompt(hp.reshape(BATCH, SEQ, D_MODEL), gpre, w_in_b, ag, a_ws, abt, bw, bb,
                                        gng, gnb, c_lin, cs, dw, l)
        h2, xn = mid_prompt(y.reshape(BATCH * SEQ, D_MODEL), hp, w_out_b, gmp, gxp, w_xq_b, mk, mv, w_xo_b, gxo,
                            gfp, l)
        gate, tg, tu, layer_w = up_prompt(xn, w_g_b, w_u_b, fwg, fwu, l, w_sources if l + 1 < DEPTH else None)
        hp = down(gate, w_down_b, h2, gfo, l)
        p_states.append((nbp, npp, nsp, tg, tu))

        zs = norm_matmul(hs, gpre, w_in_b, l, TM, TN_IN)
        ys, s_stacks = mixer_sample(zs.reshape(ns, nb, D_IN), st_b, st_p, st_s, ag, aw4, ab4, bw, bb,
                                    gng, gnb, c_lin, cs, dw, l, s_stacks)
        h1s, qs = mid_a(ys.reshape(ns * nb, D_MODEL), hs, w_out_b, gmp, gxp, w_xq_b, l)
        q8 = jnp.pad(jnp.transpose(qs.reshape(ns, nb, D_X), (1, 0, 2)), ((0, 0), (0, V7X_SUBLANES - ns), (0, 0)))
        o8 = attn_sample(q8, cache_mem_k, cache_mem_v, l)
        os_ = jnp.transpose(o8[:, :ns], (1, 0, 2)).reshape(ns * nb, D_X)
        h2s, xns = mid_c(os_, h1s, w_xo_b, gxo, gfp, l)
        hs, s_tails = ffn_sample(xns, w_g_b, w_u_b, fwg, fwu, prev_f, w_down_b, h2s, gfo, l, s_tails)

    nbp, npp, nsp, tg, tu = (jnp.stack(a) for a in zip(*p_states))
    tgs, tus = s_tails
    nbs, nps, nss, vs = s_stacks
    last = lambda a: a.reshape(DEPTH, BATCH, -1, HIST_F, FF_PAD)[:, :, -1, HIST_F - keep:, :D_FF]
    bmajor = lambda a: jnp.transpose(a.reshape(DEPTH, keep, nb, FF_PAD)[..., :D_FF], (0, 2, 1, 3))
    return (hp.reshape(BATCH, SEQ, D_MODEL),
            jnp.transpose(hs.reshape(ns, nb, D_MODEL), (1, 0, 2)),
            mk.reshape(DEPTH, BATCH, N_MEM, D_X),
            mv.reshape(DEPTH, BATCH, N_MEM, D_X),
            nbp[:, :, HIST_B - (CONV_B_WIDTH - 1):],
            npp[:, :, HIST_C - POOL_PREV:],
            nsp[:, :, HIST_D - (SCONV_WIDTH - 1):],
            jnp.concatenate([last(tg), last(tu)], axis=-1),
            nbs.reshape(DEPTH, nb, CONV_B_WIDTH - 1, D_GROUP),
            nps.reshape(DEPTH, nb, POOL_PREV, D_GROUP),
            nss.reshape(DEPTH, nb, SCONV_WIDTH - 1, D_GROUP),
            jnp.concatenate([bmajor(tgs), bmajor(tus)], axis=-1),
            vs.reshape(DEPTH, nb, ns, D_GROUP))
```
